```python
import math
import jax, jax.numpy as jnp
from jax import lax
import numpy as np

D_MODEL = 1024
BATCH = 32
SEQ = 256
DEPTH = 4
DEC_BATCH = 2
DEC_SEQ = 2048
PAST_LEN = 512

GRID_W = 64
N_BRANCH = 4
BRANCH_W = D_MODEL // 2
SSM_GROUP = 16
SSM_GROUPS = BRANCH_W // SSM_GROUP
SSM_STATE = 64
FFT_GROUPS = 4
FFT_GW = BRANCH_W // FFT_GROUPS
MLA_HEADS = 8
NOPE_DIM = 64
ROPE_DIM = 32
V_DIM = BRANCH_W // MLA_HEADS
QK_DIM = NOPE_DIM + ROPE_DIM
Q_RANK = 3 * D_MODEL // 8
KV_RANK = D_MODEL // 4
ROPE_THETA = 10000.0
CONV_W = 3
D_FF = -(-8 * D_MODEL // (3 * 256)) * 256
Q_BLOCK = 128
EPS = 1e-6

OFF_SSM = 0
OFF_FFT = OFF_SSM + BRANCH_W
OFF_CQ = OFF_FFT + BRANCH_W
OFF_CKV = OFF_CQ + Q_RANK
OFF_KPE = OFF_CKV + KV_RANK
OFF_CONV = OFF_KPE + ROPE_DIM
IN_COLS = OFF_CONV + 3 * BRANCH_W

kernel_name = 'hybrid_diffusion_trunk_step'


def rms_norm(x, g):
    xf = x.astype(jnp.float32)
    y = xf * lax.rsqrt(jnp.mean(xf * xf, axis=-1, keepdims=True) + EPS)
    return (y * g.astype(jnp.float32)).astype(x.dtype)


def modulation(cond, w_ada, b_ada):
    m = jax.nn.silu(cond) @ w_ada + b_ada
    return jnp.split(m[:, None, :], 6, axis=-1)


def axial_rope(n_tok):
    rows = n_tok // GRID_W
    row = jnp.repeat(jnp.arange(rows, dtype=jnp.float32), GRID_W)
    col = jnp.tile(jnp.arange(GRID_W, dtype=jnp.float32), rows)
    n_freq = ROPE_DIM // 4
    inv = ROPE_THETA ** (-jnp.arange(n_freq, dtype=jnp.float32) / n_freq)
    ang = jnp.concatenate([row[:, None] * inv, col[:, None] * inv], axis=-1)
    return jnp.cos(ang), jnp.sin(ang)


def apply_rope(x, cos, sin):
    xf = x.astype(jnp.float32).reshape(x.shape[:-1] + (ROPE_DIM // 2, 2))
    xe, xo = xf[..., 0], xf[..., 1]
    c, s = cos[:, None, :], sin[:, None, :]
    out = jnp.stack([xe * c - xo * s, xe * s + xo * c], axis=-1)
    return out.reshape(x.shape).astype(x.dtype)


def attention(q, k, v):
    b, lq, h, dk = q.shape
    nb = lq // Q_BLOCK
    qb = q.reshape(b, nb, Q_BLOCK, h, dk).transpose(1, 0, 2, 3, 4)
    scale = dk ** -0.5

    def one_block(qblk):
        s = jnp.einsum('bqhd,bkhd->bhqk', qblk, k).astype(jnp.float32) * scale
        p = jax.nn.softmax(s, axis=-1).astype(v.dtype)
        return jnp.einsum('bhqk,bkhd->bqhd', p, v)

    o = lax.map(one_block, qb)
    return o.transpose(1, 0, 2, 3, 4).reshape(b, lq, h, v.shape[-1])


def mla_queries(c_q, q_a_norm_g, w_uq, q_norm_g, rope_cs):
    b, l, _ = c_q.shape
    q = (rms_norm(c_q, q_a_norm_g) @ w_uq).reshape(b, l, MLA_HEADS, QK_DIM)
    q = rms_norm(q, q_norm_g)
    if rope_cs is not None:
        q = jnp.concatenate([q[..., :NOPE_DIM], apply_rope(q[..., NOPE_DIM:], *rope_cs)], axis=-1)
    return q


def mla_keys_values(ckv_n, k_pe, w_ukv, k_norm_g, rope_cs):
    b, l, _ = ckv_n.shape
    kv = (ckv_n @ w_ukv).reshape(b, l, MLA_HEADS, NOPE_DIM + V_DIM)
    k_nope, v = kv[..., :NOPE_DIM], kv[..., NOPE_DIM:]
    k_rot = jnp.broadcast_to(k_pe[:, :, None, :], (b, l, MLA_HEADS, ROPE_DIM)).astype(k_nope.dtype)
    k = rms_norm(jnp.concatenate([k_nope, k_rot], axis=-1), k_norm_g)
    if rope_cs is not None:
        k = jnp.concatenate([k[..., :NOPE_DIM], apply_rope(k[..., NOPE_DIM:], *rope_cs)], axis=-1)
    return k, v


def _cplx_affine_combine(e1, e2):
    a1r, a1i, b1r, b1i = e1
    a2r, a2i, b2r, b2i = e2
    return (a2r * a1r - a2i * a1i,
            a2r * a1i + a2i * a1r,
            a2r * b1r - a2i * b1i + b2r,
            a2r * b1i + a2i * b1r + b2i)


def ssm_scan(uf, h0, lam_re, lam_im, log_dt, b_re, b_im, c_re, c_im, reverse):
    f32 = jnp.float32
    lam_re, lam_im = lam_re.astype(f32), lam_im.astype(f32)
    dt = jnp.exp(log_dt.astype(f32))[:, None]
    mag = jnp.exp(lam_re * dt)
    a_re, a_im = mag * jnp.cos(lam_im * dt), mag * jnp.sin(lam_im * dt)
    den = lam_re * lam_re + lam_im * lam_im
    f_re = ((a_re - 1.0) * lam_re + a_im * lam_im) / den
    f_im = (a_im * lam_re - (a_re - 1.0) * lam_im) / den
    b_re, b_im = b_re.astype(f32), b_im.astype(f32)
    bb_re = f_re[..., None] * b_re - f_im[..., None] * b_im
    bb_im = f_re[..., None] * b_im + f_im[..., None] * b_re
    x_re = jnp.einsum('blgc,gpc->blgp', uf, bb_re)
    x_im = jnp.einsum('blgc,gpc->blgp', uf, bb_im)
    a_re_t = jnp.broadcast_to(a_re, x_re.shape)
    a_im_t = jnp.broadcast_to(a_im, x_re.shape)
    acc_re, acc_im, h_re, h_im = lax.associative_scan(
        _cplx_affine_combine, (a_re_t, a_im_t, x_re, x_im), axis=1, reverse=reverse)
    if h0 is not None:
        s_re, s_im = h0[..., 0][:, None], h0[..., 1][:, None]
        h_re, h_im = (h_re + acc_re * s_re - acc_im * s_im,
                      h_im + acc_re * s_im + acc_im * s_re)
    y = (jnp.einsum('blgp,gcp->blgc', h_re, c_re.astype(f32))
         - jnp.einsum('blgp,gcp->blgc', h_im, c_im.astype(f32)))
    end = 0 if reverse else -1
    final = jnp.stack([h_re[:, end], h_im[:, end]], axis=-1)
    return y, final


def s5_branch(u, h0, lam_re, lam_im, log_dt, b_re, b_im, c_re, c_im, d_skip, w_glu):
    b, l, _ = u.shape
    uf = u.astype(jnp.float32).reshape(b, l, SSM_GROUPS, SSM_GROUP)
    ys, finals = [], []
    for d, rev in ((0, False), (1, True)):
        init = None if h0 is None else h0[:, d].astype(jnp.float32)
        y_d, h_d = ssm_scan(uf, init, lam_re[d], lam_im[d], log_dt[d], b_re[d], b_im[d],
                            c_re[d], c_im[d], rev)
        ys.append(y_d)
        finals.append(h_d)
    y = ys[0] + ys[1] + uf * d_skip.astype(jnp.float32).reshape(SSM_GROUPS, SSM_GROUP)
    y = jax.nn.gelu(y.reshape(b, l, BRANCH_W)).astype(u.dtype)
    y = y * jax.nn.sigmoid(y @ w_glu)
    return y, jnp.stack(finals, axis=1).astype(u.dtype)


def fourier_branch(u):
    b, l, _ = u.shape
    ug = u.astype(jnp.float32).reshape(b, l, FFT_GROUPS, FFT_GW)
    f = jnp.fft.fft2(ug, axes=(1, 3), norm='ortho')
    return jnp.real(f).reshape(b, l, BRANCH_W).astype(u.dtype)


def short_conv(z, conv_w):
    rhs = conv_w[:, None, :].astype(z.dtype)
    return lax.conv_general_dilated(z, rhs, window_strides=(1,), padding=((1, 1),),
                                    dimension_numbers=('NWC', 'WIO', 'NWC'),
                                    feature_group_count=z.shape[-1])


def mixer(xn, lp, ctx, rope_cs):
    b, l, _ = xn.shape
    z = xn @ lp['w_in']
    u_ssm = z[..., OFF_SSM:OFF_SSM + BRANCH_W]
    u_fft = z[..., OFF_FFT:OFF_FFT + BRANCH_W]
    c_q = z[..., OFF_CQ:OFF_CQ + Q_RANK]
    c_kv = z[..., OFF_CKV:OFF_CKV + KV_RANK]
    k_pe = z[..., OFF_KPE:OFF_KPE + ROPE_DIM]
    h_in = z[..., OFF_CONV:OFF_CONV + BRANCH_W]
    g_b = z[..., OFF_CONV + BRANCH_W:OFF_CONV + 2 * BRANCH_W]
    g_c = z[..., OFF_CONV + 2 * BRANCH_W:OFF_CONV + 3 * BRANCH_W]

    ckv_n = rms_norm(c_kv, lp['kv_a_norm_g'])
    q = mla_queries(c_q, lp['q_a_norm_g'], lp['w_uq'], lp['q_norm_g'], rope_cs)
    k, v = mla_keys_values(ckv_n, k_pe, lp['w_ukv'], lp['k_norm_g'], rope_cs)
    if ctx is None:
        h0 = None
    else:
        ckv_ctx, kpe_ctx, h0 = ctx
        k_c, v_c = mla_keys_values(ckv_ctx, kpe_ctx, lp['w_ukv'], lp['k_norm_g'], None)
        k = jnp.concatenate([k, k_c.astype(k.dtype)], axis=1)
        v = jnp.concatenate([v, v_c.astype(v.dtype)], axis=1)
    y_attn = attention(q, k, v).reshape(b, l, MLA_HEADS * V_DIM)

    y_ssm, ssm_final = s5_branch(u_ssm, h0, lp['ssm_lam_re'], lp['ssm_lam_im'], lp['ssm_log_dt'],
                                 lp['ssm_b_re'], lp['ssm_b_im'], lp['ssm_c_re'], lp['ssm_c_im'],
                                 lp['ssm_d'], lp['w_glu'])
    y_fft = fourier_branch(u_fft)
    y_conv = g_b * short_conv(g_c * h_in, lp['conv_w'])

    branches = jnp.stack([y_ssm, y_fft, y_attn.astype(y_ssm.dtype), y_conv], axis=2)
    proj = jnp.einsum('blkw,kwd->blkd', branches, lp['w_branch'])
    gates = jax.nn.sigmoid(xn @ lp['w_gate'] + lp['b_gate']).reshape(b, l, N_BRANCH, D_MODEL)
    out = jnp.sum(gates * proj, axis=2) @ lp['w_out']
    return out, ckv_n, k_pe, ssm_final


def swiglu(xn, w_ffn_in, w_ffn_out):
    gu = xn @ w_ffn_in
    g, u = gu[..., :D_FF], gu[..., D_FF:]
    return (jax.nn.silu(g) * u) @ w_ffn_out


def layer(x, cond, lp, ctx, rope_cs):
    sh_m, sc_m, g_m, sh_f, sc_f, g_f = modulation(cond, lp['w_ada'], lp['b_ada'])
    xn = rms_norm(x, lp['norm_mix_g']) * (1 + sc_m) + sh_m
    mix, ckv_n, k_pe, ssm_final = mixer(xn, lp, ctx, rope_cs)
    x = x + g_m * mix
    xn = rms_norm(x, lp['norm_ffn_g']) * (1 + sc_f) + sh_f
    x = x + g_f * swiglu(xn, lp['w_ffn_in'], lp['w_ffn_out'])
    return x, ckv_n, k_pe, ssm_final


def setup_inputs(seed: int = 0) -> dict:
    key = jax.random.key(seed)
    ks = iter(jax.random.split(key, 48))
    f32 = jnp.float32

    def nrm(shape, scale):
        return jax.random.normal(next(ks), shape, f32) * scale

    n_idx = jnp.arange(SSM_STATE, dtype=f32)
    sdir = (DEPTH, 2, SSM_GROUPS)
    return {
        'x_prompt': nrm((BATCH, SEQ, D_MODEL), 1.0),
        'x_sample': nrm((DEC_BATCH, DEC_SEQ, D_MODEL), 1.0),
        'cache_ckv': nrm((DEC_BATCH, DEPTH, PAST_LEN, KV_RANK), 1.0),
        'cache_kpe': nrm((DEC_BATCH, DEPTH, PAST_LEN, ROPE_DIM), 1.0),
        'state_ssm': nrm((DEC_BATCH, DEPTH, 2, SSM_GROUPS, SSM_STATE, 2), 0.1),
        'c': nrm((DEC_BATCH, D_MODEL), 1.0),
        'c_ctx': nrm((D_MODEL,), 1.0),
        'norm_mix_g': 1.0 + nrm((DEPTH, D_MODEL), 0.02),
        'norm_ffn_g': 1.0 + nrm((DEPTH, D_MODEL), 0.02),
        'w_ada': nrm((DEPTH, D_MODEL, 6 * D_MODEL), 0.5 * D_MODEL ** -0.5),
        'b_ada': nrm((DEPTH, 6 * D_MODEL), 0.01),
        'w_in': nrm((DEPTH, D_MODEL, IN_COLS), D_MODEL ** -0.5),
        'q_a_norm_g': 1.0 + nrm((DEPTH, Q_RANK), 0.02),
        'kv_a_norm_g': 1.0 + nrm((DEPTH, KV_RANK), 0.02),
        'w_uq': nrm((DEPTH, Q_RANK, MLA_HEADS * QK_DIM), Q_RANK ** -0.5),
        'w_ukv': nrm((DEPTH, KV_RANK, MLA_HEADS * (NOPE_DIM + V_DIM)), KV_RANK ** -0.5),
        'q_norm_g': 1.0 + nrm((DEPTH, QK_DIM), 0.02),
        'k_norm_g': 1.0 + nrm((DEPTH, QK_DIM), 0.02),
        'ssm_lam_re': -0.5 + nrm(sdir + (SSM_STATE,), 0.01),
        'ssm_lam_im': jnp.pi * n_idx + nrm(sdir + (SSM_STATE,), 0.01),
        'ssm_log_dt': jax.random.uniform(next(ks), sdir, f32, math.log(1e-3), math.log(1e-1)),
        'ssm_b_re': nrm(sdir + (SSM_STATE, SSM_GROUP), (2 * SSM_GROUP) ** -0.5),
        'ssm_b_im': nrm(sdir + (SSM_STATE, SSM_GROUP), (2 * SSM_GROUP) ** -0.5),
        'ssm_c_re': nrm(sdir + (SSM_GROUP, SSM_STATE), (2 * SSM_STATE) ** -0.5 * 4.0),
        'ssm_c_im': nrm(sdir + (SSM_GROUP, SSM_STATE), (2 * SSM_STATE) ** -0.5 * 4.0),
        'ssm_d': nrm((DEPTH, BRANCH_W), 1.0),
        'w_glu': nrm((DEPTH, BRANCH_W, BRANCH_W), BRANCH_W ** -0.5),
        'conv_w': nrm((DEPTH, CONV_W, BRANCH_W), CONV_W ** -0.5),
        'w_branch': nrm((DEPTH, N_BRANCH, BRANCH_W, D_MODEL), BRANCH_W ** -0.5),
        'w_gate': nrm((DEPTH, D_MODEL, N_BRANCH * D_MODEL), D_MODEL ** -0.5),
        'b_gate': nrm((DEPTH, N_BRANCH * D_MODEL), 0.01),
        'w_out': nrm((DEPTH, D_MODEL, D_MODEL), D_MODEL ** -0.5),
        'w_ffn_in': nrm((DEPTH, D_MODEL, 2 * D_FF), D_MODEL ** -0.5),
        'w_ffn_out': nrm((DEPTH, D_FF, D_MODEL), D_FF ** -0.5),
    }


def reference(x_prompt, x_sample, cache_ckv, cache_kpe, state_ssm, c, c_ctx,
              norm_mix_g, norm_ffn_g, w_ada, b_ada, w_in, q_a_norm_g, kv_a_norm_g,
              w_uq, w_ukv, q_norm_g, k_norm_g, ssm_lam_re, ssm_lam_im, ssm_log_dt,
              ssm_b_re, ssm_b_im, ssm_c_re, ssm_c_im, ssm_d, w_glu, conv_w,
              w_branch, w_gate, b_gate, w_out, w_ffn_in, w_ffn_out):
    rope_cs = axial_rope(x_sample.shape[1])
    cond_ctx = c_ctx[None, :]
    y_p, y_s = x_prompt, x_sample
    ckv_list, kpe_list, ssm_list = [], [], []
    for l in range(DEPTH):
        lp = {
            'norm_mix_g': norm_mix_g[l], 'norm_ffn_g': norm_ffn_g[l],
            'w_ada': w_ada[l], 'b_ada': b_ada[l], 'w_in': w_in[l],
            'q_a_norm_g': q_a_norm_g[l], 'kv_a_norm_g': kv_a_norm_g[l],
            'w_uq': w_uq[l], 'w_ukv': w_ukv[l], 'q_norm_g': q_norm_g[l], 'k_norm_g': k_norm_g[l],
            'ssm_lam_re': ssm_lam_re[l], 'ssm_lam_im': ssm_lam_im[l], 'ssm_log_dt': ssm_log_dt[l],
            'ssm_b_re': ssm_b_re[l], 'ssm_b_im': ssm_b_im[l],
            'ssm_c_re': ssm_c_re[l], 'ssm_c_im': ssm_c_im[l],
            'ssm_d': ssm_d[l], 'w_glu': w_glu[l], 'conv_w': conv_w[l],
            'w_branch': w_branch[l], 'w_gate': w_gate[l], 'b_gate': b_gate[l], 'w_out': w_out[l],
            'w_ffn_in': w_ffn_in[l], 'w_ffn_out': w_ffn_out[l],
        }
        y_p, ckv_n, k_pe, ssm_final = layer(y_p, cond_ctx, lp, None, None)
        ckv_list.append(ckv_n)
        kpe_list.append(k_pe)
        ssm_list.append(ssm_final)
        y_s, _, _, _ = layer(y_s, c, lp, (cache_ckv[:, l], cache_kpe[:, l], state_ssm[:, l]), rope_cs)
    new_ckv = jnp.stack(ckv_list, axis=1)
    new_kpe = jnp.stack(kpe_list, axis=1)
    new_ssm = jnp.stack(ssm_list, axis=1)
    return (y_p, y_s, new_ckv, new_kpe, new_ssm)
```

```python
import functools
import math

import numpy as np
import jax
import jax.numpy as jnp
from jax import lax
from jax.experimental import pallas as pl
from jax.experimental.pallas import tpu as pltpu

F32 = jnp.float32
BF16 = jnp.bfloat16

D_MODEL = 1024
BATCH = 32
SEQ = 256
DEPTH = 4
DEC_BATCH = 2
DEC_SEQ = 2048
PAST_LEN = 512
GRID_W = 64
N_BRANCH = 4
BRANCH_W = 512
SSM_GROUP = 16
SSM_GROUPS = 32
SSM_STATE = 64
FFT_GROUPS = 4
FFT_GW = 128
MLA_HEADS = 8
NOPE_DIM = 64
ROPE_DIM = 32
V_DIM = 64
QK_DIM = 96
Q_RANK = 384
KV_RANK = 256
ROPE_THETA = 10000.0
D_FF = 2816
EPS = 1e-6

OFF_SSM = 0
OFF_FFT = 512
OFF_CQ = 1024
OFF_CKV = 1408
OFF_KPE = 1664
OFF_CONV = 1696

LANES = 128
SUBLANES = 8
VMEM_LIMIT_BYTES = 56 * 1024 * 1024

T = 256
N_CTX = BATCH * SEQ
N_LAT = DEC_BATCH * DEC_SEQ
N_TOK = N_CTX + N_LAT
NT_CTX = N_CTX // T
NT_LAT_SEQ = DEC_SEQ // T
NT = N_TOK // T
N_SEQ = BATCH + DEC_BATCH
HEAD_PAD = LANES

Z_SSM = 0
Z_FFT = 512
Z_CQ = 1024
Z_CKV = 1408
Z_KPE = 1664
Z_KROT = 1792
Z_HIN = 1920
Z_GB = 2432
Z_GC = 2944
Z_COLS = 3456

SCAN_STEPS = T // SUBLANES
SLAB_LANES = 256
N_SLABS = SSM_GROUPS * SSM_STATE // SLAB_LANES
ROPE_PERM = np.concatenate([np.arange(0, ROPE_DIM, 2), np.arange(1, ROPE_DIM, 2)])


def _cparams(sem):
    return pltpu.CompilerParams(dimension_semantics=sem, vmem_limit_bytes=VMEM_LIMIT_BYTES)


def _const_spec(shape):
    nd = len(shape)
    return pl.BlockSpec(shape, lambda *_: (0,) * nd, pipeline_mode=pl.Buffered(1))


def _dot(a, b):
    return jnp.dot(a, b, preferred_element_type=F32)


def _dot_nt(a, b):
    return lax.dot_general(a, b, (((1,), (1,)), ((), ())), preferred_element_type=F32)


def _sigmoid(x):
    return 1.0 / (1.0 + jnp.exp(-x))


def _rms(x, g):
    return x * lax.rsqrt(jnp.mean(x * x, axis=-1, keepdims=True) + EPS) * g


def _cond_row(i):
    return jnp.maximum(i - (NT_CTX - NT_LAT_SEQ), 0) // NT_LAT_SEQ


def _mod_body(c_ref, w_ref, b_ref, o_ref):
    c = c_ref[...]
    s = c * _sigmoid(c)
    w = w_ref[...]
    s_hi = s.astype(BF16)
    s_lo = (s - s_hi.astype(F32)).astype(BF16)
    w_hi = w.astype(BF16)
    w_lo = (w - w_hi.astype(F32)).astype(BF16)
    o_ref[...] = _dot(s_hi, w_hi) + _dot(s_lo, w_hi) + _dot(s_hi, w_lo) + b_ref[...]


def _modulation(cond8, w_ada, b_ada):
    n_col = 6 * D_MODEL // 1024
    return pl.pallas_call(
        _mod_body,
        grid=(DEPTH, n_col),
        in_specs=[
            pl.BlockSpec((SUBLANES, D_MODEL), lambda l, j: (0, 0)),
            pl.BlockSpec((None, D_MODEL, 1024), lambda l, j: (l, 0, j)),
            pl.BlockSpec((None, 1, 1024), lambda l, j: (l, 0, j)),
        ],
        out_specs=pl.BlockSpec((None, SUBLANES, 1024), lambda l, j: (l, 0, j)),
        out_shape=jax.ShapeDtypeStruct((DEPTH, SUBLANES, 6 * D_MODEL), F32),
        compiler_params=_cparams(("parallel", "parallel")),
        name="modulation",
    )(cond8, w_ada, b_ada.reshape(DEPTH, 1, 6 * D_MODEL))


def _ssm_prep_body(lre_c, lim_c, ldt_c, bre_ref, bim_ref, lre_r, lim_r, ldt_r,
                   bbre_ref, bbim_ref, pwr_ref, pwi_ref):
    lam_re = lre_c[...]
    lam_im = lim_c[...]
    dt = jnp.exp(ldt_c[...])
    mag = jnp.exp(lam_re * dt)
    a_re = mag * jnp.cos(lam_im * dt)
    a_im = mag * jnp.sin(lam_im * dt)
    den = lam_re * lam_re + lam_im * lam_im
    f_re = ((a_re - 1.0) * lam_re + a_im * lam_im) / den
    f_im = (a_im * lam_re - (a_re - 1.0) * lam_im) / den
    b_re = bre_ref[...]
    b_im = bim_ref[...]
    bbre_ref[...] = f_re * b_re - f_im * b_im
    bbim_ref[...] = f_re * b_im + f_im * b_re
    n = (lax.broadcasted_iota(jnp.int32, (SCAN_STEPS, SSM_GROUPS * SSM_STATE), 0) + 1).astype(F32)
    dt_r = jnp.exp(ldt_r[...])
    e = jnp.exp(n * (lre_r[...] * dt_r))
    th = n * (lim_r[...] * dt_r)
    pwr_ref[...] = e * jnp.cos(th)
    pwi_ref[...] = e * jnp.sin(th)


def _ssm_prep(lam_re, lam_im, log_dt, b_re, b_im):
    ld = DEPTH * 2
    gp = SSM_GROUPS * SSM_STATE
    ldt = jnp.broadcast_to(log_dt.reshape(ld, SSM_GROUPS, 1), (ld, SSM_GROUPS, SSM_STATE))
    col = lambda a: a.reshape(ld, gp, 1)
    row = lambda a: a.reshape(ld, 1, gp)
    cspec = pl.BlockSpec((None, gp, 1), lambda i: (i, 0, 0))
    rspec = pl.BlockSpec((None, 1, gp), lambda i: (i, 0, 0))
    bspec = pl.BlockSpec((None, gp, SSM_GROUP), lambda i: (i, 0, 0))
    pspec = pl.BlockSpec((None, SCAN_STEPS, gp), lambda i: (i, 0, 0))
    return pl.pallas_call(
        _ssm_prep_body,
        grid=(ld,),
        in_specs=[cspec, cspec, cspec, bspec, bspec, rspec, rspec, rspec],
        out_specs=[bspec, bspec, pspec, pspec],
        out_shape=[jax.ShapeDtypeStruct((ld, gp, SSM_GROUP), F32)] * 2
        + [jax.ShapeDtypeStruct((ld, SCAN_STEPS, gp), F32)] * 2,
        compiler_params=_cparams(("parallel",)),
        name="ssm_prep",
    )(col(lam_re), col(lam_im), col(ldt), b_re.reshape(ld, gp, SSM_GROUP), b_im.reshape(ld, gp, SSM_GROUP),
      row(lam_re), row(lam_im), row(ldt))


def _ssm_tables(bb_re, bb_im, pw_re, pw_im, c_re, c_im):
    ld = DEPTH * 2
    eye = jnp.eye(4, dtype=F32)
    par = (jnp.arange(N_SLABS) % 2).astype(F32)

    def b_tab(bb):
        t = bb.reshape(ld, N_SLABS, 4, SSM_STATE, SSM_GROUP).transpose(0, 1, 2, 4, 3)
        blk = jnp.einsum('lsgcp,gh->lsgchp', t, eye).reshape(ld, N_SLABS, 64, SLAB_LANES)
        top = blk * (1.0 - par)[None, :, None, None]
        bot = blk * par[None, :, None, None]
        return jnp.concatenate([top, bot], axis=2)

    def c_tab(c):
        t = c.reshape(ld, N_SLABS, 4, SSM_GROUP, SSM_STATE).transpose(0, 1, 2, 4, 3)
        blk = jnp.einsum('lsgpc,gh->lsgphc', t, eye).reshape(ld, N_SLABS, SLAB_LANES, 64)
        left = blk * (1.0 - par)[None, :, None, None]
        right = blk * par[None, :, None, None]
        return jnp.concatenate([left, right], axis=3)

    bcat = jnp.concatenate([b_tab(bb_re), b_tab(bb_im)], axis=3).astype(BF16)
    ccat = jnp.concatenate([c_tab(c_re.reshape(ld, SSM_GROUPS, SSM_GROUP, SSM_STATE)),
                            -c_tab(c_im.reshape(ld, SSM_GROUPS, SSM_GROUP, SSM_STATE))],
                           axis=2).astype(BF16)

    def p_tab(pw):
        t = pw.reshape(ld, SCAN_STEPS, N_SLABS, SLAB_LANES).transpose(0, 2, 1, 3)
        return jnp.broadcast_to(t[:, :, :, None, :], (ld, N_SLABS, SCAN_STEPS, SUBLANES, SLAB_LANES))

    shp = lambda a: a.reshape((DEPTH, 2) + a.shape[1:])
    return shp(bcat), shp(ccat), shp(p_tab(pw_re)), shp(p_tab(pw_im))


def _scan_perms():
    r = np.arange(T)
    t_of_r = (r % SUBLANES) * SCAN_STEPS + r // SUBLANES
    pm = np.zeros((2, T, T), np.float32)
    pm[0, r, t_of_r] = 1.0
    pm[1, r, T - 1 - t_of_r] = 1.0
    pmt = np.transpose(pm, (0, 2, 1))
    pmt3 = np.concatenate([pmt, pmt, pmt], axis=2)
    return jnp.asarray(pm, BF16), jnp.asarray(pmt3, BF16)


def _ssm_body(u_ref, pm_ref, pmt3_ref, bcat_ref, ccat_ref, pwr_ref, pwi_ref, h0_ref,
              y_ref, fin_ref, up_s, xr_s, xi_s, yw_s, st_s):
    i = pl.program_id(1)
    first = jnp.logical_or(i < NT_CTX, i % NT_LAT_SEQ == 0)

    @pl.when(first)
    def _():
        st_s[...] = jnp.broadcast_to(h0_ref[...], st_s.shape)

    ub = u_ref[...].astype(BF16)
    pm = pm_ref[...]
    for w in range(4):
        up_s[w] = _dot(pm, ub[:, w * LANES:(w + 1) * LANES]).astype(BF16)
    yw_s[...] = jnp.zeros_like(yw_s)
    row = lax.broadcasted_iota(jnp.int32, (SUBLANES, SLAB_LANES), 0)

    def slab(s, carry):
        w = s // 2
        x = _dot(up_s[w], bcat_ref[s])
        xr_s[...] = x[:, :SLAB_LANES]
        xi_s[...] = x[:, SLAB_LANES:]
        ar = pwr_ref[s, 0]
        ai = pwi_ref[s, 0]
        hr = jnp.zeros((SUBLANES, SLAB_LANES), F32)
        hi = jnp.zeros((SUBLANES, SLAB_LANES), F32)
        for j in range(SCAN_STEPS):
            rows = slice(j * SUBLANES, (j + 1) * SUBLANES)
            nr = ar * hr - ai * hi + xr_s[rows, :]
            ni = ar * hi + ai * hr + xi_s[rows, :]
            hr, hi = nr, ni
            xr_s[rows, :] = hr
            xi_s[rows, :] = hi
        fpr = st_s[s, 0]
        fpi = st_s[s, 1]
        cr = jnp.where(row == 0, pltpu.roll(fpr, 1, 0), pltpu.roll(hr, 1, 0))
        ci = jnp.where(row == 0, pltpu.roll(fpi, 1, 0), pltpu.roll(hi, 1, 0))
        m_r = pwr_ref[s, SCAN_STEPS - 1]
        m_i = pwi_ref[s, SCAN_STEPS - 1]
        qr, qi = m_r, m_i
        for k in (1, 2, 4):
            sr = pltpu.roll(cr, k, 0)
            si = pltpu.roll(ci, k, 0)
            keep = row >= k
            cr, ci = (cr + jnp.where(keep, qr * sr - qi * si, 0.0),
                      ci + jnp.where(keep, qr * si + qi * sr, 0.0))
            qr, qi = qr * qr - qi * qi, 2.0 * qr * qi
        fr = m_r * cr - m_i * ci + hr
        fi = m_r * ci + m_i * cr + hi
        st_s[s, 0] = fr
        st_s[s, 1] = fi
        fin_ref[s, 0] = fr
        fin_ref[s, 1] = fi
        for j in range(SCAN_STEPS):
            rows = slice(j * SUBLANES, (j + 1) * SUBLANES)
            pr = pwr_ref[s, j]
            pi = pwi_ref[s, j]
            xr_s[rows, :] = xr_s[rows, :] + pr * cr - pi * ci
            xi_s[rows, :] = xi_s[rows, :] + pr * ci + pi * cr
        hb = jnp.concatenate([xr_s[...], xi_s[...]], axis=1).astype(BF16)
        yw_s[w] = yw_s[w] + _dot(hb, ccat_ref[s])
        return carry

    lax.fori_loop(0, N_SLABS, slab, 0)

    y = jnp.concatenate([yw_s[0], yw_s[1], yw_s[2], yw_s[3]], axis=1)
    y_hi = y.astype(BF16)
    r1 = y - y_hi.astype(F32)
    y_mid = r1.astype(BF16)
    y_lo = (r1 - y_mid.astype(F32)).astype(BF16)
    y3 = jnp.concatenate([y_hi, y_mid, y_lo], axis=0)
    y_ref[...] = _dot(pmt3_ref[...], y3)


def _ssm_tile(d, i):
    rev = (i // NT_LAT_SEQ) * NT_LAT_SEQ + (NT_LAT_SEQ - 1) - i % NT_LAT_SEQ
    return jnp.where(jnp.logical_and(d == 1, i >= NT_CTX), rev, i)


def _ssm_seq(i):
    return jnp.where(i < NT_CTX, i, NT_CTX + (i - NT_CTX) // NT_LAT_SEQ)


def _ssm_scan(u_ssm, pm, pmt3, bcat, ccat, pwr, pwi, h0):
    tab = lambda *shape: pl.BlockSpec((None,) + shape, lambda d, i: (d,) + (0,) * len(shape))
    return pl.pallas_call(
        _ssm_body,
        grid=(2, NT),
        in_specs=[
            pl.BlockSpec((T, BRANCH_W), lambda d, i: (_ssm_tile(d, i), 0)),
            tab(T, T), tab(T, 3 * T),
            tab(N_SLABS, LANES, 2 * SLAB_LANES), tab(N_SLABS, 2 * SLAB_LANES, LANES),
            tab(N_SLABS, SCAN_STEPS, SUBLANES, SLAB_LANES), tab(N_SLABS, SCAN_STEPS, SUBLANES, SLAB_LANES),
            pl.BlockSpec((None, None, N_SLABS, 2, 1, SLAB_LANES), lambda d, i: (_ssm_seq(i), d, 0, 0, 0, 0)),
        ],
        out_specs=[
            pl.BlockSpec((None, T, BRANCH_W), lambda d, i: (d, _ssm_tile(d, i), 0)),
            pl.BlockSpec((None, None, N_SLABS, 2, SUBLANES, SLAB_LANES),
                         lambda d, i: (_ssm_seq(i), d, 0, 0, 0, 0)),
        ],
        out_shape=[jax.ShapeDtypeStruct((2, N_TOK, BRANCH_W), F32),
                   jax.ShapeDtypeStruct((N_SEQ, 2, N_SLABS, 2, SUBLANES, SLAB_LANES), F32)],
        scratch_shapes=[
            pltpu.VMEM((4, T, LANES), BF16),
            pltpu.VMEM((T, SLAB_LANES), F32),
            pltpu.VMEM((T, SLAB_LANES), F32),
            pltpu.VMEM((4, T, LANES), F32),
            pltpu.VMEM((N_SLABS, 2, SUBLANES, SLAB_LANES), F32),
        ],
        compiler_params=_cparams(("arbitrary", "arbitrary")),
        name="ssm_scan",
    )(u_ssm, pm, pmt3, bcat, ccat, pwr, pwi, h0)


def _pre_body(x_ref, mod_ref, g_ref, win_ref, qag_ref, kvg_ref, wuq_ref, wuk_ref, wva_ref, wvb_ref,
              qng_ref, kng_ref, rc_ref, rs1_ref, rs2_ref,
              ussm_ref, ufft_ref, q_ref, k_ref, va_ref, vb_ref, ckv_ref, kpe_ref, vc_ref, gb_ref):
    x = x_ref[...]
    sh = mod_ref[:, 0:D_MODEL]
    sc = mod_ref[:, D_MODEL:2 * D_MODEL]
    xn = (_rms(x, g_ref[...]) * (1.0 + sc) + sh).astype(BF16)

    def proj(lo, hi):
        return _dot(xn, win_ref[:, lo:hi])

    ussm_ref[...] = proj(Z_SSM, Z_FFT)
    ufft_ref[...] = proj(Z_FFT, Z_CQ).astype(BF16)
    zk = proj(Z_KPE, Z_KROT)
    kpe_ref[...] = zk[:, 0:ROPE_DIM]
    krot = proj(Z_KROT, Z_HIN)
    h_in = proj(Z_HIN, Z_GB)
    gb_ref[...] = proj(Z_GB, Z_GC)
    vc_ref[...] = proj(Z_GC, Z_COLS) * h_in

    rc = rc_ref[...]
    rs1 = rs1_ref[...]
    rs2 = rs2_ref[...]

    def rope(v):
        return v * rc + pltpu.roll(v, LANES - ROPE_DIM // 2, 1) * rs1 + pltpu.roll(v, ROPE_DIM // 2, 1) * rs2

    def head_norm(v, g):
        ss = jnp.sum(v * v, axis=-1, keepdims=True)
        return v * lax.rsqrt(ss * (1.0 / QK_DIM) + EPS) * g

    cqn = _rms(proj(Z_CQ, Z_CKV), qag_ref[...]).astype(BF16)
    qf = _dot(cqn, wuq_ref[...])
    ckvn = _rms(proj(Z_CKV, Z_KPE), kvg_ref[...])
    ckv_ref[...] = ckvn
    ckvb = ckvn.astype(BF16)
    kf = _dot(ckvb, wuk_ref[...])
    va_ref[...] = _dot(ckvb, wva_ref[...]).astype(BF16)
    vb_ref[...] = _dot(ckvb, wvb_ref[...]).astype(BF16)
    for h in range(MLA_HEADS):
        lanes = slice(h * HEAD_PAD, (h + 1) * HEAD_PAD)
        qh = head_norm(qf[:, lanes], qng_ref[...])
        q_ref[:, lanes] = rope(qh).astype(BF16)
        kh = head_norm(kf[:, lanes] + krot, kng_ref[...])
        k_ref[:, lanes] = rope(kh).astype(BF16)


def _pre(x, mod_l, norm_g, w_in_r, qag, kvg, wuq, wuk, wva, wvb, qng, kng, rc, rs1, rs2):
    tile = lambda width: pl.BlockSpec((T, width), lambda i: (i, 0))
    rope_blk = lambda i: (jnp.where(i < NT_CTX, NT_LAT_SEQ, i % NT_LAT_SEQ), 0)
    rspec = pl.BlockSpec((T, LANES), rope_blk)
    out = lambda width, dt: jax.ShapeDtypeStruct((N_TOK, width), dt)
    return pl.pallas_call(
        _pre_body,
        grid=(NT,),
        in_specs=[
            tile(D_MODEL),
            pl.BlockSpec((None, 1, 6 * D_MODEL), lambda i: (_cond_row(i), 0, 0)),
            _const_spec((1, D_MODEL)),
            _const_spec((D_MODEL, Z_COLS)),
            _const_spec((1, Q_RANK)), _const_spec((1, KV_RANK)),
            _const_spec((Q_RANK, MLA_HEADS * HEAD_PAD)), _const_spec((KV_RANK, MLA_HEADS * HEAD_PAD)),
            _const_spec((KV_RANK, BRANCH_W)), _const_spec((KV_RANK, BRANCH_W)),
            _const_spec((1, HEAD_PAD)), _const_spec((1, HEAD_PAD)),
            rspec, rspec, rspec,
        ],
        out_specs=[tile(BRANCH_W), tile(BRANCH_W), tile(MLA_HEADS * HEAD_PAD), tile(MLA_HEADS * HEAD_PAD),
                   tile(BRANCH_W), tile(BRANCH_W), tile(KV_RANK), tile(ROPE_DIM), tile(BRANCH_W),
                   tile(BRANCH_W)],
        out_shape=[out(BRANCH_W, F32), out(BRANCH_W, BF16), out(MLA_HEADS * HEAD_PAD, BF16),
                   out(MLA_HEADS * HEAD_PAD, BF16), out(BRANCH_W, BF16), out(BRANCH_W, BF16),
                   out(KV_RANK, F32), out(ROPE_DIM, F32), out(BRANCH_W, F32), out(BRANCH_W, F32)],
        compiler_params=_cparams(("parallel",)),
        name="pre_mixer",
    )(x, mod_l, norm_g, w_in_r, qag, kvg, wuq, wuk, wva, wvb, qng, kng, rc, rs1, rs2)


def _kvc_body(ckv_ref, kr_ref, wuk_ref, wva_ref, wvb_ref, kng_ref, k_ref, va_ref, vb_ref):
    ckvb = ckv_ref[...].astype(BF16)
    kf = _dot(ckvb, wuk_ref[...])
    va_ref[...] = _dot(ckvb, wva_ref[...]).astype(BF16)
    vb_ref[...] = _dot(ckvb, wvb_ref[...]).astype(BF16)
    krot = kr_ref[...]
    g = kng_ref[...]
    for h in range(MLA_HEADS):
        lanes = slice(h * HEAD_PAD, (h + 1) * HEAD_PAD)
        kh = kf[:, lanes] + krot
        ss = jnp.sum(kh * kh, axis=-1, keepdims=True)
        k_ref[:, lanes] = (kh * lax.rsqrt(ss * (1.0 / QK_DIM) + EPS) * g).astype(BF16)


def _cache_kv(cache_ckv, cache_kr, wuk, wva, wvb, kng):
    nt = PAST_LEN // T
    wspec = lambda r, c: pl.BlockSpec((None, r, c), lambda l, b, t: (l, 0, 0))
    ospec = lambda c: pl.BlockSpec((None, None, T, c), lambda l, b, t: (l, b, t, 0))
    return pl.pallas_call(
        _kvc_body,
        grid=(DEPTH, DEC_BATCH, nt),
        in_specs=[
            pl.BlockSpec((None, None, T, KV_RANK), lambda l, b, t: (b, l, t, 0)),
            pl.BlockSpec((None, None, T, HEAD_PAD), lambda l, b, t: (b, l, t, 0)),
            wspec(KV_RANK, MLA_HEADS * HEAD_PAD), wspec(KV_RANK, BRANCH_W), wspec(KV_RANK, BRANCH_W),
            wspec(1, HEAD_PAD),
        ],
        out_specs=[ospec(MLA_HEADS * HEAD_PAD), ospec(BRANCH_W), ospec(BRANCH_W)],
        out_shape=[jax.ShapeDtypeStruct((DEPTH, DEC_BATCH, PAST_LEN, MLA_HEADS * HEAD_PAD), BF16),
                   jax.ShapeDtypeStruct((DEPTH, DEC_BATCH, PAST_LEN, BRANCH_W), BF16),
                   jax.ShapeDtypeStruct((DEPTH, DEC_BATCH, PAST_LEN, BRANCH_W), BF16)],
        compiler_params=_cparams(("parallel", "parallel", "parallel")),
        name="cache_kv",
    )(cache_ckv, cache_kr, wuk, wva, wvb, kng)


def _fnet_body(seq_len, u_ref, csl_ref, csc_ref, *rest):
    if len(rest) == 4:
        _, y_ref, v1_s, v2_s = rest
    else:
        y_ref, v1_s, v2_s = rest
    r = pl.program_id(1)

    @pl.when(r == 0)
    def _():
        csc = csc_ref[...]
        for c in range(seq_len // T):
            rows = slice(c * T, (c + 1) * T)
            for g in range(FFT_GROUPS):
                lanes = slice(g * FFT_GW, (g + 1) * FFT_GW)
                v = _dot(u_ref[rows, lanes], csc)
                v1_s[rows, lanes] = v[:, :FFT_GW].astype(BF16)
                v2_s[rows, lanes] = v[:, FFT_GW:].astype(BF16)

    y = _dot(csl_ref[:, :seq_len], v1_s[...]) + _dot(csl_ref[:, seq_len:], v2_s[...])
    y_ref[...] = y.astype(BF16)


def _fnet(u_fft, csl, csc, seq_len, n_seq, tile0, prev=None):
    nr = seq_len // T
    blk0 = tile0 * T // seq_len
    in_specs = [
        pl.BlockSpec((seq_len, BRANCH_W), lambda b, r: (blk0 + b, 0)),
        pl.BlockSpec((T, 2 * seq_len), lambda b, r: (r, 0)),
        _const_spec((FFT_GW, 2 * FFT_GW)),
    ]
    args = [u_fft, csl, csc]
    aliases = {}
    if prev is not None:
        in_specs.append(pl.BlockSpec(memory_space=pl.ANY))
        args.append(prev)
        aliases = {3: 0}
    return pl.pallas_call(
        functools.partial(_fnet_body, seq_len),
        grid=(n_seq, nr),
        in_specs=in_specs,
        out_specs=pl.BlockSpec((T, BRANCH_W), lambda b, r: (tile0 + b * nr + r, 0)),
        out_shape=jax.ShapeDtypeStruct((N_TOK, BRANCH_W), BF16),
        scratch_shapes=[pltpu.VMEM((seq_len, BRANCH_W), BF16), pltpu.VMEM((seq_len, BRANCH_W), BF16)],
        input_output_aliases=aliases,
        compiler_params=_cparams(("parallel", "arbitrary")),
        name="fnet_%d" % seq_len,
    )(*args)


def _dft_tables(n):
    k = jnp.arange(n, dtype=jnp.int32)
    m = (k[:, None] * k[None, :]) % n
    ang = m.astype(F32) * (2.0 * math.pi / n)
    s = 1.0 / math.sqrt(n)
    return jnp.cos(ang) * s, jnp.sin(ang) * s


def _attn_body(n_seg, q_ref, *refs):
    k_refs = refs[0:n_seg]
    va_refs = refs[n_seg:2 * n_seg]
    vb_refs = refs[2 * n_seg:3 * n_seg]
    o_ref = refs[-1]
    scale = QK_DIM ** -0.5
    for hp in range(MLA_HEADS // 2):
        out_lanes = slice(hp * LANES, (hp + 1) * LANES)
        acc = jnp.zeros((T, LANES), F32)
        for h, v_refs in ((2 * hp, va_refs), (2 * hp + 1, vb_refs)):
            lanes = slice(h * HEAD_PAD, (h + 1) * HEAD_PAD)
            qh = q_ref[:, lanes]
            ss = [_dot_nt(qh, kr[:, lanes]) * scale for kr in k_refs]
            m = functools.reduce(jnp.maximum, [jnp.max(s, axis=-1, keepdims=True) for s in ss])
            ps = [jnp.exp(s - m) for s in ss]
            l = functools.reduce(lambda a, b: a + b, [jnp.sum(p, axis=-1, keepdims=True) for p in ps])
            o = functools.reduce(lambda a, b: a + b,
                                 [_dot(p.astype(BF16), vr[:, out_lanes]) for p, vr in zip(ps, v_refs)])
            acc = acc + o / l
        o_ref[:, out_lanes] = acc.astype(BF16)


def _attention(q, segs, n_seq, nq, tile0, prev=None):
    in_specs = [pl.BlockSpec((T, MLA_HEADS * HEAD_PAD), lambda b, r: (tile0 + b * nq + r, 0))]
    args = [q]
    for idx in range(3):
        for seg in segs:
            arr = seg[idx]
            width = arr.shape[-1]
            in_specs.append(pl.BlockSpec(seg[3] + (width,), seg[4]))
            args.append(arr)
    aliases = {}
    if prev is not None:
        in_specs.append(pl.BlockSpec(memory_space=pl.ANY))
        args.append(prev)
        aliases = {len(args) - 1: 0}
    body = functools.partial(_attn_body, len(segs))
    if prev is not None:
        inner = body
        body = lambda *refs: inner(*refs[:-2], refs[-1])
    return pl.pallas_call(
        body,
        grid=(n_seq, nq),
        in_specs=in_specs,
        out_specs=pl.BlockSpec((T, BRANCH_W), lambda b, r: (tile0 + b * nq + r, 0)),
        out_shape=jax.ShapeDtypeStruct((N_TOK, BRANCH_W), BF16),
        input_output_aliases=aliases,
        compiler_params=_cparams(("parallel", "arbitrary")),
        name="attention_%d" % nq,
    )(*args)


def _mix_body(x_ref, mod_ref, g_ref, yf_ref, yb_ref, u_ref, dsk_ref, wglu_ref, fft_ref, att_ref,
              vc_ref, vp_ref, vn_ref, cw_ref, gb_ref, wbr_ref, wg_ref, bg_ref, wo_ref, o_ref):
    i = pl.program_id(0)
    x = x_ref[...]
    sh = mod_ref[:, 0:D_MODEL]
    sc = mod_ref[:, D_MODEL:2 * D_MODEL]
    gm = mod_ref[:, 2 * D_MODEL:3 * D_MODEL]
    xn = (_rms(x, g_ref[...]) * (1.0 + sc) + sh).astype(BF16)

    y = jax.nn.gelu(yf_ref[...] + yb_ref[...] + u_ref[...] * dsk_ref[...], approximate=True)
    y = y * _sigmoid(_dot(y.astype(BF16), wglu_ref[...]))

    v = vc_ref[...]
    row = lax.broadcasted_iota(jnp.int32, (T, BRANCH_W), 0)
    is_lat = i >= NT_CTX
    has_prev = jnp.logical_and(is_lat, i % NT_LAT_SEQ != 0)
    has_next = jnp.logical_and(is_lat, i % NT_LAT_SEQ != NT_LAT_SEQ - 1)
    halo_p = jnp.where(has_prev, vp_ref[SUBLANES - 1:SUBLANES, :], 0.0)
    halo_n = jnp.where(has_next, vn_ref[0:1, :], 0.0)
    v_prev = jnp.where(row == 0, halo_p, pltpu.roll(v, 1, 0))
    v_next = jnp.where(row == T - 1, halo_n, pltpu.roll(v, T - 1, 0))
    y_conv = gb_ref[...] * (cw_ref[0:1, :] * v_prev + cw_ref[1:2, :] * v + cw_ref[2:3, :] * v_next)

    branches = (y.astype(BF16), fft_ref[...], att_ref[...], y_conv.astype(BF16))
    merged = jnp.zeros((T, D_MODEL), F32)
    for k in range(N_BRANCH):
        cols = slice(k * D_MODEL, (k + 1) * D_MODEL)
        gate = _sigmoid(_dot(xn, wg_ref[:, cols]) + bg_ref[:, cols])
        merged = merged + gate * _dot(branches[k], wbr_ref[k])
    out = _dot(merged.astype(BF16), wo_ref[...])
    o_ref[...] = x + gm * out


def _mix(x, mod_l, norm_g, y_ssm, u_ssm, d_skip, w_glu, y_fft, y_att, vconv, conv_w, g_b,
         w_branch, w_gate, b_gate, w_out):
    tile = lambda width: pl.BlockSpec((T, width), lambda i: (i, 0))
    rb = T // SUBLANES
    n_rb = N_TOK // SUBLANES
    return pl.pallas_call(
        _mix_body,
        grid=(NT,),
        in_specs=[
            tile(D_MODEL),
            pl.BlockSpec((None, 1, 6 * D_MODEL), lambda i: (_cond_row(i), 0, 0)),
            _const_spec((1, D_MODEL)),
            pl.BlockSpec((None, T, BRANCH_W), lambda i: (0, i, 0)),
            pl.BlockSpec((None, T, BRANCH_W), lambda i: (1, i, 0)),
            tile(BRANCH_W),
            _const_spec((1, BRANCH_W)),
            _const_spec((BRANCH_W, BRANCH_W)),
            tile(BRANCH_W), tile(BRANCH_W), tile(BRANCH_W),
            pl.BlockSpec((SUBLANES, BRANCH_W), lambda i: (jnp.maximum(i * rb - 1, 0), 0)),
            pl.BlockSpec((SUBLANES, BRANCH_W), lambda i: (jnp.minimum(i * rb + rb, n_rb - 1), 0)),
            _const_spec((SUBLANES, BRANCH_W)),
            tile(BRANCH_W),
            _const_spec((N_BRANCH, BRANCH_W, D_MODEL)),
            _const_spec((D_MODEL, N_BRANCH * D_MODEL)),
            _const_spec((1, N_BRANCH * D_MODEL)),
            _const_spec((D_MODEL, D_MODEL)),
        ],
        out_specs=tile(D_MODEL),
        out_shape=jax.ShapeDtypeStruct((N_TOK, D_MODEL), F32),
        compiler_params=_cparams(("parallel",)),
        name="mixer_merge",
    )(x, mod_l, norm_g, y_ssm, y_ssm, u_ssm, d_skip, w_glu, y_fft, y_att, vconv, vconv, vconv, conv_w, g_b,
      w_branch, w_gate, b_gate, w_out)


FF_CHUNK = D_FF // 2


def _ffn_body(x_ref, mod_ref, g_ref, wi_ref, wo_ref, o_ref):
    x = x_ref[...]
    sh = mod_ref[:, 3 * D_MODEL:4 * D_MODEL]
    sc = mod_ref[:, 4 * D_MODEL:5 * D_MODEL]
    gf = mod_ref[:, 5 * D_MODEL:6 * D_MODEL]
    xn = (_rms(x, g_ref[...]) * (1.0 + sc) + sh).astype(BF16)
    acc = jnp.zeros((T, D_MODEL), F32)
    for c in range(D_FF // FF_CHUNK):
        lo = c * FF_CHUNK
        g = _dot(xn, wi_ref[:, lo:lo + FF_CHUNK])
        u = _dot(xn, wi_ref[:, D_FF + lo:D_FF + lo + FF_CHUNK])
        h = (g * _sigmoid(g) * u).astype(BF16)
        acc = acc + _dot(h, wo_ref[lo:lo + FF_CHUNK, :])
    o_ref[...] = x + gf * acc


def _ffn(x, mod_l, norm_g, w_in, w_out):
    tile = pl.BlockSpec((T, D_MODEL), lambda i: (i, 0))
    return pl.pallas_call(
        _ffn_body,
        grid=(NT,),
        in_specs=[
            tile,
            pl.BlockSpec((None, 1, 6 * D_MODEL), lambda i: (_cond_row(i), 0, 0)),
            _const_spec((1, D_MODEL)),
            _const_spec((D_MODEL, 2 * D_FF)),
            _const_spec((D_FF, D_MODEL)),
        ],
        out_specs=tile,
        out_shape=jax.ShapeDtypeStruct((N_TOK, D_MODEL), F32),
        compiler_params=_cparams(("parallel",)),
        name="ffn",
    )(x, mod_l, norm_g, w_in, w_out)


def _pad_heads(w, lo, width):
    rows = w.shape[0]
    stride = w.shape[1] // MLA_HEADS
    wh = w.reshape(rows, MLA_HEADS, stride)[:, :, lo:lo + width]
    return jnp.pad(wh, ((0, 0), (0, 0), (0, HEAD_PAD - width))).reshape(rows, MLA_HEADS * HEAD_PAD)


def _layer_weights(l, w_in, w_uq, w_ukv, q_norm_g, k_norm_g):
    wi = w_in[l]
    kpe = wi[:, OFF_KPE:OFF_KPE + ROPE_DIM]
    z = lambda n: jnp.zeros((D_MODEL, n), F32)
    w_in_r = jnp.concatenate([
        wi[:, OFF_SSM:OFF_KPE],
        kpe, z(LANES - ROPE_DIM),
        z(NOPE_DIM), kpe[:, ROPE_PERM], z(LANES - QK_DIM),
        wi[:, OFF_CONV:OFF_CONV + 3 * BRANCH_W],
    ], axis=1).astype(BF16)
    wq = w_uq[l].reshape(Q_RANK, MLA_HEADS, QK_DIM)
    wq = jnp.concatenate([wq[:, :, :NOPE_DIM], wq[:, :, NOPE_DIM:][:, :, ROPE_PERM]], axis=2)
    wuq = jnp.pad(wq, ((0, 0), (0, 0), (0, HEAD_PAD - QK_DIM))).reshape(Q_RANK, MLA_HEADS * HEAD_PAD).astype(BF16)
    wuk = _pad_heads(w_ukv[l], 0, NOPE_DIM).astype(BF16)
    wv = w_ukv[l].reshape(KV_RANK, MLA_HEADS, NOPE_DIM + V_DIM)[:, :, NOPE_DIM:]
    even = (jnp.arange(MLA_HEADS) % 2 == 0)[None, :, None]
    wva = jnp.where(even, wv, 0.0).reshape(KV_RANK, BRANCH_W).astype(BF16)
    wvb = jnp.where(even, 0.0, wv).reshape(KV_RANK, BRANCH_W).astype(BF16)

    def gain(g):
        gp = jnp.concatenate([g[:NOPE_DIM], g[NOPE_DIM:][ROPE_PERM], jnp.zeros((HEAD_PAD - QK_DIM,), F32)])
        return gp.reshape(1, HEAD_PAD)

    return w_in_r, wuq, wuk, wva, wvb, gain(q_norm_g[l]), gain(k_norm_g[l])


def _rope_tables():
    rows = DEC_SEQ // GRID_W
    r = jnp.repeat(jnp.arange(rows, dtype=F32), GRID_W)
    c = jnp.tile(jnp.arange(GRID_W, dtype=F32), rows)
    n_freq = ROPE_DIM // 4
    inv = ROPE_THETA ** (-jnp.arange(n_freq, dtype=F32) / n_freq)
    ang = jnp.concatenate([r[:, None] * inv, c[:, None] * inv], axis=-1)
    cos, sin = jnp.cos(ang), jnp.sin(ang)
    half = ROPE_DIM // 2
    ones = jnp.ones((DEC_SEQ, NOPE_DIM), F32)
    zeros = jnp.zeros((DEC_SEQ, NOPE_DIM), F32)
    tail1 = jnp.ones((DEC_SEQ, HEAD_PAD - QK_DIM), F32)
    tail0 = jnp.zeros((DEC_SEQ, HEAD_PAD - QK_DIM), F32)
    zh = jnp.zeros((DEC_SEQ, half), F32)
    rc = jnp.concatenate([ones, cos, cos, tail1], axis=1)
    rs1 = jnp.concatenate([zeros, -sin, zh, tail0], axis=1)
    rs2 = jnp.concatenate([zeros, zh, sin, tail0], axis=1)
    ident = lambda v: jnp.full((T, HEAD_PAD), v, F32)
    return (jnp.concatenate([rc, ident(1.0)], axis=0), jnp.concatenate([rs1, ident(0.0)], axis=0),
            jnp.concatenate([rs2, ident(0.0)], axis=0))


def kernel(x_prompt, x_sample, cache_ckv, cache_kpe, state_ssm, c, c_ctx, norm_mix_g, norm_ffn_g, w_ada, b_ada,
           w_in, q_a_norm_g, kv_a_norm_g, w_uq, w_ukv, q_norm_g, k_norm_g, ssm_lam_re, ssm_lam_im, ssm_log_dt,
           ssm_b_re, ssm_b_im, ssm_c_re, ssm_c_im, ssm_d, w_glu, conv_w, w_branch, w_gate, b_gate, w_out,
           w_ffn_in, w_ffn_out):
    x = jnp.concatenate([x_prompt.reshape(N_CTX, D_MODEL), x_sample.reshape(N_LAT, D_MODEL)], axis=0)

    cond8 = jnp.concatenate([c_ctx[None, :], c, jnp.zeros((SUBLANES - 1 - DEC_BATCH, D_MODEL), F32)], axis=0)
    mod = _modulation(cond8, w_ada, b_ada).reshape(DEPTH, SUBLANES, 1, 6 * D_MODEL)

    bb_re, bb_im, pw_re, pw_im = _ssm_prep(ssm_lam_re, ssm_lam_im, ssm_log_dt, ssm_b_re, ssm_b_im)
    bcat, ccat, pwr, pwi = _ssm_tables(bb_re, bb_im, pw_re, pw_im, ssm_c_re, ssm_c_im)
    pm, pmt3 = _scan_perms()
    st = state_ssm.transpose(1, 0, 2, 5, 3, 4).reshape(DEPTH, DEC_BATCH, 2, 2, N_SLABS, 1, SLAB_LANES)
    st = st.transpose(0, 1, 2, 4, 3, 5, 6)
    h0_all = jnp.concatenate([jnp.zeros((DEPTH, BATCH) + st.shape[2:], F32), st], axis=1)

    rc, rs1, rs2 = _rope_tables()
    cl, sl = _dft_tables(SEQ)
    csl_ctx = jnp.concatenate([cl, -sl], axis=1).astype(BF16)
    cl, sl = _dft_tables(DEC_SEQ)
    csl_lat = jnp.concatenate([cl, -sl], axis=1).astype(BF16)
    cc, sc = _dft_tables(FFT_GW)
    csc = jnp.concatenate([cc, sc], axis=1).astype(BF16)

    lw = [_layer_weights(l, w_in, w_uq, w_ukv, q_norm_g, k_norm_g) for l in range(DEPTH)]
    cache_kr = jnp.pad(cache_kpe[..., ROPE_PERM], ((0, 0), (0, 0), (0, 0), (NOPE_DIM, HEAD_PAD - QK_DIM)))
    kc, vca, vcb = _cache_kv(cache_ckv, cache_kr,
                             jnp.stack([w[2] for w in lw]), jnp.stack([w[3] for w in lw]),
                             jnp.stack([w[4] for w in lw]), jnp.stack([w[6] for w in lw]))

    conv_w8 = jnp.pad(conv_w, ((0, 0), (0, SUBLANES - 3), (0, 0)))
    lat_blk = N_CTX // DEC_SEQ
    ckv_list, kpe_list, fin_list = [], [], []
    for l in range(DEPTH):
        w_in_r, wuq, wuk, wva, wvb, qng, kng = lw[l]
        (u_ssm, u_fft, q, k, va, vb, ckv_n, kpe, vconv, g_b) = _pre(
            x, mod[l], norm_mix_g[l][None, :], w_in_r, q_a_norm_g[l][None, :], kv_a_norm_g[l][None, :],
            wuq, wuk, wva, wvb, qng, kng, rc, rs1, rs2)
        ckv_list.append(ckv_n[:N_CTX].reshape(BATCH, SEQ, KV_RANK))
        kpe_list.append(kpe[:N_CTX].reshape(BATCH, SEQ, ROPE_DIM))

        y_ssm, fin = _ssm_scan(u_ssm, pm, pmt3, bcat[l], ccat[l], pwr[l], pwi[l], h0_all[l])
        fin_list.append(fin[:BATCH, :, :, :, SUBLANES - 1, :])

        y_fft = _fnet(u_fft, csl_ctx, csc, SEQ, BATCH, 0)
        y_fft = _fnet(u_fft, csl_lat, csc, DEC_SEQ, DEC_BATCH, NT_CTX, prev=y_fft)

        ctx_seg = (k, va, vb, (T,), lambda b, r: (b, 0))
        y_att = _attention(q, [ctx_seg], BATCH, 1, 0)
        lat_seg = (k, va, vb, (DEC_SEQ,), lambda b, r: (lat_blk + b, 0))
        cache_seg = (kc[l], vca[l], vcb[l], (None, PAST_LEN), lambda b, r: (b, 0, 0))
        y_att = _attention(q, [lat_seg, cache_seg], DEC_BATCH, NT_LAT_SEQ, NT_CTX, prev=y_att)

        x = _mix(x, mod[l], norm_mix_g[l][None, :], y_ssm, u_ssm, ssm_d[l][None, :], w_glu[l].astype(BF16),
                 y_fft, y_att, vconv, conv_w8[l], g_b, w_branch[l].astype(BF16), w_gate[l].astype(BF16),
                 b_gate[l][None, :], w_out[l].astype(BF16))
        x = _ffn(x, mod[l], norm_ffn_g[l][None, :], w_ffn_in[l].astype(BF16), w_ffn_out[l].astype(BF16))

    y_p = x[:N_CTX].reshape(BATCH, SEQ, D_MODEL)
    y_s = x[N_CTX:].reshape(DEC_BATCH, DEC_SEQ, D_MODEL)
    new_ckv = jnp.stack(ckv_list, axis=1)
    new_kpe = jnp.stack(kpe_list, axis=1)
    fin = jnp.stack(fin_list, axis=1)
    fin = fin.reshape(BATCH, DEPTH, 2, N_SLABS, 2, 4, SSM_STATE).transpose(0, 1, 2, 3, 5, 6, 4)
    new_ssm = fin.reshape(BATCH, DEPTH, 2, SSM_GROUPS, SSM_STATE, 2)
    return (y_p, y_s, new_ckv, new_kpe, new_ssm)
```

```python
import functools
import math

import numpy as np
import jax
import jax.numpy as jnp
from jax import lax
from jax.experimental import pallas as pl
from jax.experimental.pallas import tpu as pltpu

F32 = jnp.float32
BF16 = jnp.bfloat16

D_MODEL = 1024
BATCH = 32
SEQ = 256
DEPTH = 4
DEC_BATCH = 2
DEC_SEQ = 2048
PAST_LEN = 512
GRID_W = 64
N_BRANCH = 4
BRANCH_W = 512
SSM_GROUP = 16
SSM_GROUPS = 32
SSM_STATE = 64
FFT_GROUPS = 4
FFT_GW = 128
MLA_HEADS = 8
NOPE_DIM = 64
ROPE_DIM = 32
V_DIM = 64
QK_DIM = 96
Q_RANK = 384
KV_RANK = 256
ROPE_THETA = 10000.0
D_FF = 2816
EPS = 1e-6

OFF_SSM = 0
OFF_FFT = 512
OFF_CQ = 1024
OFF_CKV = 1408
OFF_KPE = 1664
OFF_CONV = 1696

LANES = 128
SUBLANES = 8
VMEM_LIMIT_BYTES = 56 * 1024 * 1024

T = 256
N_CTX = BATCH * SEQ
N_LAT = DEC_BATCH * DEC_SEQ
N_TOK = N_CTX + N_LAT
NT_CTX = N_CTX // T
NT_LAT_SEQ = DEC_SEQ // T
NT = N_TOK // T
N_SEQ = BATCH + DEC_BATCH
HEAD_PAD = LANES

Z_SSM = 0
Z_FFT = 512
Z_CQ = 1024
Z_CKV = 1408
Z_KPE = 1664
Z_KROT = 1792
Z_KSW = 1920
Z_HIN = 2048
Z_GB = 2560
Z_GC = 3072
Z_COLS = 3584

SCAN_STEPS = T // SUBLANES
SLAB_LANES = 256
N_SLABS = SSM_GROUPS * SSM_STATE // SLAB_LANES
ROPE_PERM = np.concatenate([np.arange(0, ROPE_DIM, 2), np.arange(1, ROPE_DIM, 2)])
ROPE_PARTNER = np.concatenate([np.arange(1, ROPE_DIM, 2), np.arange(0, ROPE_DIM, 2)])


def _cparams(sem):
    return pltpu.CompilerParams(dimension_semantics=sem, vmem_limit_bytes=VMEM_LIMIT_BYTES)


def _const_spec(shape):
    nd = len(shape)
    return pl.BlockSpec(shape, lambda *_: (0,) * nd, pipeline_mode=pl.Buffered(1))


def _dot(a, b):
    return jnp.dot(a, b, preferred_element_type=F32)


def _dot_nt(a, b):
    return lax.dot_general(a, b, (((1,), (1,)), ((), ())), preferred_element_type=F32)


def _sigmoid(x):
    return 1.0 / (1.0 + jnp.exp(-x))


def _rms(x, g):
    return x * lax.rsqrt(jnp.mean(x * x, axis=-1, keepdims=True) + EPS) * g


def _cond_row(i):
    return jnp.maximum(i - (NT_CTX - NT_LAT_SEQ), 0) // NT_LAT_SEQ


def _mod_body(c_ref, w_ref, b_ref, o_ref):
    c = c_ref[...]
    s = c * _sigmoid(c)
    w = w_ref[...]
    s_hi = s.astype(BF16)
    s_lo = (s - s_hi.astype(F32)).astype(BF16)
    w_hi = w.astype(BF16)
    w_lo = (w - w_hi.astype(F32)).astype(BF16)
    o_ref[...] = _dot(s_hi, w_hi) + _dot(s_lo, w_hi) + _dot(s_hi, w_lo) + b_ref[...]


def _modulation(cond8, w_ada, b_ada):
    n_col = 6 * D_MODEL // 1024
    return pl.pallas_call(
        _mod_body,
        grid=(DEPTH, n_col),
        in_specs=[
            pl.BlockSpec((SUBLANES, D_MODEL), lambda l, j: (0, 0)),
            pl.BlockSpec((None, D_MODEL, 1024), lambda l, j: (l, 0, j)),
            pl.BlockSpec((None, 1, 1024), lambda l, j: (l, 0, j)),
        ],
        out_specs=pl.BlockSpec((None, SUBLANES, 1024), lambda l, j: (l, 0, j)),
        out_shape=jax.ShapeDtypeStruct((DEPTH, SUBLANES, 6 * D_MODEL), F32),
        compiler_params=_cparams(("parallel", "parallel")),
        name="modulation",
    )(cond8, w_ada, b_ada.reshape(DEPTH, 1, 6 * D_MODEL))


def _ssm_prep_body(lre_c, lim_c, ldt_c, bre_ref, bim_ref, lre_r, lim_r, ldt_r,
                   bbre_ref, bbim_ref, pwr_ref, pwi_ref):
    lam_re = lre_c[...]
    lam_im = lim_c[...]
    dt = jnp.exp(ldt_c[...])
    mag = jnp.exp(lam_re * dt)
    a_re = mag * jnp.cos(lam_im * dt)
    a_im = mag * jnp.sin(lam_im * dt)
    den = lam_re * lam_re + lam_im * lam_im
    f_re = ((a_re - 1.0) * lam_re + a_im * lam_im) / den
    f_im = (a_im * lam_re - (a_re - 1.0) * lam_im) / den
    b_re = bre_ref[...]
    b_im = bim_ref[...]
    bbre_ref[...] = f_re * b_re - f_im * b_im
    bbim_ref[...] = f_re * b_im + f_im * b_re
    n = (lax.broadcasted_iota(jnp.int32, (SCAN_STEPS, SSM_GROUPS * SSM_STATE), 0) + 1).astype(F32)
    dt_r = jnp.exp(ldt_r[...])
    e = jnp.exp(n * (lre_r[...] * dt_r))
    th = n * (lim_r[...] * dt_r)
    pwr_ref[...] = e * jnp.cos(th)
    pwi_ref[...] = e * jnp.sin(th)


def _ssm_prep(lam_re, lam_im, log_dt, b_re, b_im):
    ld = DEPTH * 2
    gp = SSM_GROUPS * SSM_STATE
    ldt = jnp.broadcast_to(log_dt.reshape(ld, SSM_GROUPS, 1), (ld, SSM_GROUPS, SSM_STATE))
    col = lambda a: a.reshape(ld, gp, 1)
    row = lambda a: a.reshape(ld, 1, gp)
    cspec = pl.BlockSpec((None, gp, 1), lambda i: (i, 0, 0))
    rspec = pl.BlockSpec((None, 1, gp), lambda i: (i, 0, 0))
    bspec = pl.BlockSpec((None, gp, SSM_GROUP), lambda i: (i, 0, 0))
    pspec = pl.BlockSpec((None, SCAN_STEPS, gp), lambda i: (i, 0, 0))
    return pl.pallas_call(
        _ssm_prep_body,
        grid=(ld,),
        in_specs=[cspec, cspec, cspec, bspec, bspec, rspec, rspec, rspec],
        out_specs=[bspec, bspec, pspec, pspec],
        out_shape=[jax.ShapeDtypeStruct((ld, gp, SSM_GROUP), F32)] * 2
        + [jax.ShapeDtypeStruct((ld, SCAN_STEPS, gp), F32)] * 2,
        compiler_params=_cparams(("parallel",)),
        name="ssm_prep",
    )(col(lam_re), col(lam_im), col(ldt), b_re.reshape(ld, gp, SSM_GROUP), b_im.reshape(ld, gp, SSM_GROUP),
      row(lam_re), row(lam_im), row(ldt))


def _ssm_tables(bb_re, bb_im, pw_re, pw_im, c_re, c_im):
    ld = DEPTH * 2
    eye = jnp.eye(4, dtype=F32)
    par = (jnp.arange(N_SLABS) % 2).astype(F32)

    def b_tab(bb):
        t = bb.reshape(ld, N_SLABS, 4, SSM_STATE, SSM_GROUP).transpose(0, 1, 2, 4, 3)
        blk = jnp.einsum('lsgcp,gh->lsgchp', t, eye).reshape(ld, N_SLABS, 64, SLAB_LANES)
        top = blk * (1.0 - par)[None, :, None, None]
        bot = blk * par[None, :, None, None]
        return jnp.concatenate([top, bot], axis=2)

    def c_tab(c):
        t = c.reshape(ld, N_SLABS, 4, SSM_GROUP, SSM_STATE).transpose(0, 1, 2, 4, 3)
        blk = jnp.einsum('lsgpc,gh->lsgphc', t, eye).reshape(ld, N_SLABS, SLAB_LANES, 64)
        left = blk * (1.0 - par)[None, :, None, None]
        right = blk * par[None, :, None, None]
        return jnp.concatenate([left, right], axis=3)

    bcat = jnp.concatenate([b_tab(bb_re), b_tab(bb_im)], axis=3).astype(BF16)
    ccat = jnp.concatenate([c_tab(c_re.reshape(ld, SSM_GROUPS, SSM_GROUP, SSM_STATE)),
                            -c_tab(c_im.reshape(ld, SSM_GROUPS, SSM_GROUP, SSM_STATE))],
                           axis=2).astype(BF16)

    def p_tab(pw):
        t = pw.reshape(ld, SCAN_STEPS, N_SLABS, SLAB_LANES).transpose(0, 2, 1, 3)
        return jnp.broadcast_to(t[:, :, :, None, :], (ld, N_SLABS, SCAN_STEPS, SUBLANES, SLAB_LANES))

    shp = lambda a: a.reshape((DEPTH, 2) + a.shape[1:])
    return shp(bcat), shp(ccat), shp(p_tab(pw_re)), shp(p_tab(pw_im))


def _scan_perms():
    r = np.arange(T)
    t_of_r = (r % SUBLANES) * SCAN_STEPS + r // SUBLANES
    pm = np.zeros((2, T, T), np.float32)
    pm[0, r, t_of_r] = 1.0
    pm[1, r, T - 1 - t_of_r] = 1.0
    pmt = np.transpose(pm, (0, 2, 1))
    pmt3 = np.concatenate([pmt, pmt, pmt], axis=2)
    return jnp.asarray(pm, BF16), jnp.asarray(pmt3, BF16)


def _ssm_body(u_ref, pm_ref, pmt3_ref, bcat_ref, ccat_ref, pwr_ref, pwi_ref, h0_ref,
              y_ref, fin_ref, up_s, xr_all, xi_all, yw_s, st_s):
    i = pl.program_id(1)
    first = jnp.logical_or(i < NT_CTX, i % NT_LAT_SEQ == 0)

    @pl.when(first)
    def _():
        st_s[...] = jnp.broadcast_to(h0_ref[...], st_s.shape)

    ub = u_ref[...].astype(BF16)
    pm = pm_ref[...]
    for w in range(4):
        up_s[w] = _dot(pm, ub[:, w * LANES:(w + 1) * LANES]).astype(BF16)
    yw_s[...] = jnp.zeros_like(yw_s)
    row = lax.broadcasted_iota(jnp.int32, (SUBLANES, SLAB_LANES), 0)

    for s in range(N_SLABS):
        w = s // 2
        xr_s = xr_all.at[s]
        xi_s = xi_all.at[s]
        x = _dot(up_s[w], bcat_ref[s])
        xr_s[...] = x[:, :SLAB_LANES]
        xi_s[...] = x[:, SLAB_LANES:]
        ar = pwr_ref[s, 0]
        ai = pwi_ref[s, 0]
        hr = jnp.zeros((SUBLANES, SLAB_LANES), F32)
        hi = jnp.zeros((SUBLANES, SLAB_LANES), F32)
        for j in range(SCAN_STEPS):
            rows = slice(j * SUBLANES, (j + 1) * SUBLANES)
            nr = ar * hr - ai * hi + xr_s[rows, :]
            ni = ar * hi + ai * hr + xi_s[rows, :]
            hr, hi = nr, ni
            xr_s[rows, :] = hr
            xi_s[rows, :] = hi
        fpr = st_s[s, 0]
        fpi = st_s[s, 1]
        cr = jnp.where(row == 0, pltpu.roll(fpr, 1, 0), pltpu.roll(hr, 1, 0))
        ci = jnp.where(row == 0, pltpu.roll(fpi, 1, 0), pltpu.roll(hi, 1, 0))
        m_r = pwr_ref[s, SCAN_STEPS - 1]
        m_i = pwi_ref[s, SCAN_STEPS - 1]
        qr, qi = m_r, m_i
        for k in (1, 2, 4):
            sr = pltpu.roll(cr, k, 0)
            si = pltpu.roll(ci, k, 0)
            keep = row >= k
            cr, ci = (cr + jnp.where(keep, qr * sr - qi * si, 0.0),
                      ci + jnp.where(keep, qr * si + qi * sr, 0.0))
            qr, qi = qr * qr - qi * qi, 2.0 * qr * qi
        fr = m_r * cr - m_i * ci + hr
        fi = m_r * ci + m_i * cr + hi
        st_s[s, 0] = fr
        st_s[s, 1] = fi
        fin_ref[s, 0] = fr
        fin_ref[s, 1] = fi
        for j in range(SCAN_STEPS):
            rows = slice(j * SUBLANES, (j + 1) * SUBLANES)
            pr = pwr_ref[s, j]
            pi = pwi_ref[s, j]
            xr_s[rows, :] = xr_s[rows, :] + pr * cr - pi * ci
            xi_s[rows, :] = xi_s[rows, :] + pr * ci + pi * cr
        hb = jnp.concatenate([xr_s[...], xi_s[...]], axis=1).astype(BF16)
        yw_s[w] = yw_s[w] + _dot(hb, ccat_ref[s])

    y = jnp.concatenate([yw_s[0], yw_s[1], yw_s[2], yw_s[3]], axis=1)
    y_hi = y.astype(BF16)
    r1 = y - y_hi.astype(F32)
    y_mid = r1.astype(BF16)
    y_lo = (r1 - y_mid.astype(F32)).astype(BF16)
    y3 = jnp.concatenate([y_hi, y_mid, y_lo], axis=0)
    y_ref[...] = _dot(pmt3_ref[...], y3)


def _ssm_tile(d, i):
    rev = (i // NT_LAT_SEQ) * NT_LAT_SEQ + (NT_LAT_SEQ - 1) - i % NT_LAT_SEQ
    return jnp.where(jnp.logical_and(d == 1, i >= NT_CTX), rev, i)


def _ssm_seq(i):
    return jnp.where(i < NT_CTX, i, NT_CTX + (i - NT_CTX) // NT_LAT_SEQ)


def _ssm_scan(u_ssm, pm, pmt3, bcat, ccat, pwr, pwi, h0):
    tab = lambda *shape: pl.BlockSpec((None,) + shape, lambda d, i: (d,) + (0,) * len(shape))
    return pl.pallas_call(
        _ssm_body,
        grid=(2, NT),
        in_specs=[
            pl.BlockSpec((T, BRANCH_W), lambda d, i: (_ssm_tile(d, i), 0)),
            tab(T, T), tab(T, 3 * T),
            tab(N_SLABS, LANES, 2 * SLAB_LANES), tab(N_SLABS, 2 * SLAB_LANES, LANES),
            tab(N_SLABS, SCAN_STEPS, SUBLANES, SLAB_LANES), tab(N_SLABS, SCAN_STEPS, SUBLANES, SLAB_LANES),
            pl.BlockSpec((None, None, N_SLABS, 2, 1, SLAB_LANES), lambda d, i: (_ssm_seq(i), d, 0, 0, 0, 0)),
        ],
        out_specs=[
            pl.BlockSpec((None, T, BRANCH_W), lambda d, i: (d, _ssm_tile(d, i), 0)),
            pl.BlockSpec((None, None, N_SLABS, 2, SUBLANES, SLAB_LANES),
                         lambda d, i: (_ssm_seq(i), d, 0, 0, 0, 0)),
        ],
        out_shape=[jax.ShapeDtypeStruct((2, N_TOK, BRANCH_W), F32),
                   jax.ShapeDtypeStruct((N_SEQ, 2, N_SLABS, 2, SUBLANES, SLAB_LANES), F32)],
        scratch_shapes=[
            pltpu.VMEM((4, T, LANES), BF16),
            pltpu.VMEM((N_SLABS, T, SLAB_LANES), F32),
            pltpu.VMEM((N_SLABS, T, SLAB_LANES), F32),
            pltpu.VMEM((4, T, LANES), F32),
            pltpu.VMEM((N_SLABS, 2, SUBLANES, SLAB_LANES), F32),
        ],
        compiler_params=_cparams(("arbitrary", "arbitrary")),
        name="ssm_scan",
    )(u_ssm, pm, pmt3, bcat, ccat, pwr, pwi, h0)


def _pre_body(x_ref, mod_ref, g_ref, win_ref, qag_ref, kvg_ref, wuq_ref, wuqs_ref, wuk_ref, wva_ref, wvb_ref,
              qng_ref, qngs_ref, kng_ref, kngs_ref, rc_ref, rsw_ref,
              ussm_ref, ufft_ref, q_ref, k_ref, va_ref, vb_ref, ckv_ref, kpe_ref, vc_ref, gb_ref):
    x = x_ref[...]
    sh = mod_ref[:, 0:D_MODEL]
    sc = mod_ref[:, D_MODEL:2 * D_MODEL]
    xn = (_rms(x, g_ref[...]) * (1.0 + sc) + sh).astype(BF16)

    def proj(lo, hi):
        return _dot(xn, win_ref[:, lo:hi])

    ussm_ref[...] = proj(Z_SSM, Z_FFT)
    ufft_ref[...] = proj(Z_FFT, Z_CQ).astype(BF16)
    zk = proj(Z_KPE, Z_KROT)
    kpe_ref[...] = zk[:, 0:ROPE_DIM]
    krot = proj(Z_KROT, Z_KSW)
    kswp = proj(Z_KSW, Z_HIN)
    h_in = proj(Z_HIN, Z_GB)
    gb_ref[...] = proj(Z_GB, Z_GC)
    vc_ref[...] = proj(Z_GC, Z_COLS) * h_in

    rc = rc_ref[...]
    rsw = rsw_ref[...]
    q_gc = qng_ref[...] * rc
    q_gs = qngs_ref[...] * rsw
    k_gc = kng_ref[...] * rc
    k_sw = kswp * (kngs_ref[...] * rsw)

    def rstd(v):
        return lax.rsqrt(jnp.sum(v * v, axis=-1, keepdims=True) * (1.0 / QK_DIM) + EPS)

    cqn = _rms(proj(Z_CQ, Z_CKV), qag_ref[...]).astype(BF16)
    qf = _dot(cqn, wuq_ref[...])
    qs = _dot(cqn, wuqs_ref[...])
    ckvn = _rms(proj(Z_CKV, Z_KPE), kvg_ref[...])
    ckv_ref[...] = ckvn
    ckvb = ckvn.astype(BF16)
    kf = _dot(ckvb, wuk_ref[...])
    va_ref[...] = _dot(ckvb, wva_ref[...]).astype(BF16)
    vb_ref[...] = _dot(ckvb, wvb_ref[...]).astype(BF16)
    for h in range(MLA_HEADS):
        lanes = slice(h * HEAD_PAD, (h + 1) * HEAD_PAD)
        qh = qf[:, lanes]
        q_ref[:, lanes] = ((qh * q_gc + qs[:, lanes] * q_gs) * rstd(qh)).astype(BF16)
        kh = kf[:, lanes] + krot
        k_ref[:, lanes] = ((kh * k_gc + k_sw) * rstd(kh)).astype(BF16)


def _pre(x, mod_l, norm_g, w_in_r, qag, kvg, wuq, wuqs, wuk, wva, wvb, qng, qngs, kng, kngs, rc, rsw):
    tile = lambda width: pl.BlockSpec((T, width), lambda i: (i, 0))
    rope_blk = lambda i: (jnp.where(i < NT_CTX, NT_LAT_SEQ, i % NT_LAT_SEQ), 0)
    rspec = pl.BlockSpec((T, LANES), rope_blk)
    out = lambda width, dt: jax.ShapeDtypeStruct((N_TOK, width), dt)
    return pl.pallas_call(
        _pre_body,
        grid=(NT,),
        in_specs=[
            tile(D_MODEL),
            pl.BlockSpec((None, 1, 6 * D_MODEL), lambda i: (_cond_row(i), 0, 0)),
            _const_spec((1, D_MODEL)),
            _const_spec((D_MODEL, Z_COLS)),
            _const_spec((1, Q_RANK)), _const_spec((1, KV_RANK)),
            _const_spec((Q_RANK, MLA_HEADS * HEAD_PAD)), _const_spec((Q_RANK, MLA_HEADS * HEAD_PAD)),
            _const_spec((KV_RANK, MLA_HEADS * HEAD_PAD)),
            _const_spec((KV_RANK, BRANCH_W)), _const_spec((KV_RANK, BRANCH_W)),
            _const_spec((1, HEAD_PAD)), _const_spec((1, HEAD_PAD)),
            _const_spec((1, HEAD_PAD)), _const_spec((1, HEAD_PAD)),
            rspec, rspec,
        ],
        out_specs=[tile(BRANCH_W), tile(BRANCH_W), tile(MLA_HEADS * HEAD_PAD), tile(MLA_HEADS * HEAD_PAD),
                   tile(BRANCH_W), tile(BRANCH_W), tile(KV_RANK), tile(ROPE_DIM), tile(BRANCH_W),
                   tile(BRANCH_W)],
        out_shape=[out(BRANCH_W, F32), out(BRANCH_W, BF16), out(MLA_HEADS * HEAD_PAD, BF16),
                   out(MLA_HEADS * HEAD_PAD, BF16), out(BRANCH_W, BF16), out(BRANCH_W, BF16),
                   out(KV_RANK, F32), out(ROPE_DIM, F32), out(BRANCH_W, F32), out(BRANCH_W, F32)],
        compiler_params=_cparams(("parallel",)),
        name="pre_mixer",
    )(x, mod_l, norm_g, w_in_r, qag, kvg, wuq, wuqs, wuk, wva, wvb, qng, qngs, kng, kngs, rc, rsw)


def _kvc_body(ckv_ref, kr_ref, wuk_ref, wva_ref, wvb_ref, kng_ref, k_ref, va_ref, vb_ref):
    ckvb = ckv_ref[...].astype(BF16)
    kf = _dot(ckvb, wuk_ref[...])
    va_ref[...] = _dot(ckvb, wva_ref[...]).astype(BF16)
    vb_ref[...] = _dot(ckvb, wvb_ref[...]).astype(BF16)
    krot = kr_ref[...]
    g = kng_ref[...]
    for h in range(MLA_HEADS):
        lanes = slice(h * HEAD_PAD, (h + 1) * HEAD_PAD)
        kh = kf[:, lanes] + krot
        ss = jnp.sum(kh * kh, axis=-1, keepdims=True)
        k_ref[:, lanes] = (kh * lax.rsqrt(ss * (1.0 / QK_DIM) + EPS) * g).astype(BF16)


def _cache_kv(cache_ckv, cache_kr, wuk, wva, wvb, kng):
    nt = PAST_LEN // T
    wspec = lambda r, c: pl.BlockSpec((None, r, c), lambda l, b, t: (l, 0, 0))
    ospec = lambda c: pl.BlockSpec((None, None, T, c), lambda l, b, t: (l, b, t, 0))
    return pl.pallas_call(
        _kvc_body,
        grid=(DEPTH, DEC_BATCH, nt),
        in_specs=[
            pl.BlockSpec((None, None, T, KV_RANK), lambda l, b, t: (b, l, t, 0)),
            pl.BlockSpec((None, None, T, HEAD_PAD), lambda l, b, t: (b, l, t, 0)),
            wspec(KV_RANK, MLA_HEADS * HEAD_PAD), wspec(KV_RANK, BRANCH_W), wspec(KV_RANK, BRANCH_W),
            wspec(1, HEAD_PAD),
        ],
        out_specs=[ospec(MLA_HEADS * HEAD_PAD), ospec(BRANCH_W), ospec(BRANCH_W)],
        out_shape=[jax.ShapeDtypeStruct((DEPTH, DEC_BATCH, PAST_LEN, MLA_HEADS * HEAD_PAD), BF16),
                   jax.ShapeDtypeStruct((DEPTH, DEC_BATCH, PAST_LEN, BRANCH_W), BF16),
                   jax.ShapeDtypeStruct((DEPTH, DEC_BATCH, PAST_LEN, BRANCH_W), BF16)],
        compiler_params=_cparams(("parallel", "parallel", "parallel")),
        name="cache_kv",
    )(cache_ckv, cache_kr, wuk, wva, wvb, kng)


def _fnet_body(seq_len, u_ref, csl_ref, csc_ref, *rest):
    if len(rest) == 4:
        _, y_ref, v1_s, v2_s = rest
    else:
        y_ref, v1_s, v2_s = rest
    r = pl.program_id(1)

    @pl.when(r == 0)
    def _():
        csc = csc_ref[...]
        for c in range(seq_len // T):
            rows = slice(c * T, (c + 1) * T)
            for g in range(FFT_GROUPS):
                lanes = slice(g * FFT_GW, (g + 1) * FFT_GW)
                v = _dot(u_ref[rows, lanes], csc)
                v1_s[rows, lanes] = v[:, :FFT_GW].astype(BF16)
                v2_s[rows, lanes] = v[:, FFT_GW:].astype(BF16)

    y = _dot(csl_ref[:, :seq_len], v1_s[...]) + _dot(csl_ref[:, seq_len:], v2_s[...])
    y_ref[...] = y.astype(BF16)


def _fnet(u_fft, csl, csc, seq_len, n_seq, tile0, prev=None):
    nr = seq_len // T
    blk0 = tile0 * T // seq_len
    in_specs = [
        pl.BlockSpec((seq_len, BRANCH_W), lambda b, r: (blk0 + b, 0)),
        pl.BlockSpec((T, 2 * seq_len), lambda b, r: (r, 0)),
        _const_spec((FFT_GW, 2 * FFT_GW)),
    ]
    args = [u_fft, csl, csc]
    aliases = {}
    if prev is not None:
        in_specs.append(pl.BlockSpec(memory_space=pl.ANY))
        args.append(prev)
        aliases = {3: 0}
    return pl.pallas_call(
        functools.partial(_fnet_body, seq_len),
        grid=(n_seq, nr),
        in_specs=in_specs,
        out_specs=pl.BlockSpec((T, BRANCH_W), lambda b, r: (tile0 + b * nr + r, 0)),
        out_shape=jax.ShapeDtypeStruct((N_TOK, BRANCH_W), BF16),
        scratch_shapes=[pltpu.VMEM((seq_len, BRANCH_W), BF16), pltpu.VMEM((seq_len, BRANCH_W), BF16)],
        input_output_aliases=aliases,
        compiler_params=_cparams(("parallel", "arbitrary")),
        name="fnet_%d" % seq_len,
    )(*args)


def _dft_tables(n):
    k = jnp.arange(n, dtype=jnp.int32)
    m = (k[:, None] * k[None, :]) % n
    ang = m.astype(F32) * (2.0 * math.pi / n)
    s = 1.0 / math.sqrt(n)
    return jnp.cos(ang) * s, jnp.sin(ang) * s


def _attn_body(n_seg, q_ref, *refs):
    k_refs = refs[0:n_seg]
    va_refs = refs[n_seg:2 * n_seg]
    vb_refs = refs[2 * n_seg:3 * n_seg]
    o_ref = refs[-1]
    scale = QK_DIM ** -0.5
    for hp in range(MLA_HEADS // 2):
        out_lanes = slice(hp * LANES, (hp + 1) * LANES)
        acc = jnp.zeros((T, LANES), F32)
        for h, v_refs in ((2 * hp, va_refs), (2 * hp + 1, vb_refs)):
            lanes = slice(h * HEAD_PAD, (h + 1) * HEAD_PAD)
            qh = q_ref[:, lanes]
            ss = [_dot_nt(qh, kr[:, lanes]) * scale for kr in k_refs]
            m = functools.reduce(jnp.maximum, [jnp.max(s, axis=-1, keepdims=True) for s in ss])
            ps = [jnp.exp(s - m) for s in ss]
            l = functools.reduce(lambda a, b: a + b, [jnp.sum(p, axis=-1, keepdims=True) for p in ps])
            o = functools.reduce(lambda a, b: a + b,
                                 [_dot(p.astype(BF16), vr[:, out_lanes]) for p, vr in zip(ps, v_refs)])
            acc = acc + o / l
        o_ref[:, out_lanes] = acc.astype(BF16)


def _attention(q, segs, n_seq, nq, tile0, prev=None):
    in_specs = [pl.BlockSpec((T, MLA_HEADS * HEAD_PAD), lambda b, r: (tile0 + b * nq + r, 0))]
    args = [q]
    for idx in range(3):
        for seg in segs:
            arr = seg[idx]
            width = arr.shape[-1]
            in_specs.append(pl.BlockSpec(seg[3] + (width,), seg[4]))
            args.append(arr)
    aliases = {}
    if prev is not None:
        in_specs.append(pl.BlockSpec(memory_space=pl.ANY))
        args.append(prev)
        aliases = {len(args) - 1: 0}
    body = functools.partial(_attn_body, len(segs))
    if prev is not None:
        inner = body
        body = lambda *refs: inner(*refs[:-2], refs[-1])
    return pl.pallas_call(
        body,
        grid=(n_seq, nq),
        in_specs=in_specs,
        out_specs=pl.BlockSpec((T, BRANCH_W), lambda b, r: (tile0 + b * nq + r, 0)),
        out_shape=jax.ShapeDtypeStruct((N_TOK, BRANCH_W), BF16),
        input_output_aliases=aliases,
        compiler_params=_cparams(("parallel", "arbitrary")),
        name="attention_%d" % nq,
    )(*args)


def _mix_body(x_ref, mod_ref, g_ref, yf_ref, yb_ref, u_ref, dsk_ref, wglu_ref, fft_ref, att_ref,
              vc_ref, vp_ref, vn_ref, cw_ref, gb_ref, wbr_ref, wg_ref, bg_ref, wo_ref, o_ref):
    i = pl.program_id(0)
    x = x_ref[...]
    sh = mod_ref[:, 0:D_MODEL]
    sc = mod_ref[:, D_MODEL:2 * D_MODEL]
    gm = mod_ref[:, 2 * D_MODEL:3 * D_MODEL]
    xn = (_rms(x, g_ref[...]) * (1.0 + sc) + sh).astype(BF16)

    y = jax.nn.gelu(yf_ref[...] + yb_ref[...] + u_ref[...] * dsk_ref[...], approximate=True)
    y = y * _sigmoid(_dot(y.astype(BF16), wglu_ref[...]))

    v = vc_ref[...]
    row = lax.broadcasted_iota(jnp.int32, (T, BRANCH_W), 0)
    is_lat = i >= NT_CTX
    has_prev = jnp.logical_and(is_lat, i % NT_LAT_SEQ != 0)
    has_next = jnp.logical_and(is_lat, i % NT_LAT_SEQ != NT_LAT_SEQ - 1)
    halo_p = jnp.where(has_prev, vp_ref[SUBLANES - 1:SUBLANES, :], 0.0)
    halo_n = jnp.where(has_next, vn_ref[0:1, :], 0.0)
    v_prev = jnp.where(row == 0, halo_p, pltpu.roll(v, 1, 0))
    v_next = jnp.where(row == T - 1, halo_n, pltpu.roll(v, T - 1, 0))
    y_conv = gb_ref[...] * (cw_ref[0:1, :] * v_prev + cw_ref[1:2, :] * v + cw_ref[2:3, :] * v_next)

    branches = (y.astype(BF16), fft_ref[...], att_ref[...], y_conv.astype(BF16))
    merged = jnp.zeros((T, D_MODEL), F32)
    for k in range(N_BRANCH):
        cols = slice(k * D_MODEL, (k + 1) * D_MODEL)
        gate = _sigmoid(_dot(xn, wg_ref[:, cols]) + bg_ref[:, cols])
        merged = merged + gate * _dot(branches[k], wbr_ref[k])
    out = _dot(merged.astype(BF16), wo_ref[...])
    o_ref[...] = x + gm * out


def _mix(x, mod_l, norm_g, y_ssm, u_ssm, d_skip, w_glu, y_fft, y_att, vconv, conv_w, g_b,
         w_branch, w_gate, b_gate, w_out):
    tile = lambda width: pl.BlockSpec((T, width), lambda i: (i, 0))
    rb = T // SUBLANES
    n_rb = N_TOK // SUBLANES
    return pl.pallas_call(
        _mix_body,
        grid=(NT,),
        in_specs=[
            tile(D_MODEL),
            pl.BlockSpec((None, 1, 6 * D_MODEL), lambda i: (_cond_row(i), 0, 0)),
            _const_spec((1, D_MODEL)),
            pl.BlockSpec((None, T, BRANCH_W), lambda i: (0, i, 0)),
            pl.BlockSpec((None, T, BRANCH_W), lambda i: (1, i, 0)),
            tile(BRANCH_W),
            _const_spec((1, BRANCH_W)),
            _const_spec((BRANCH_W, BRANCH_W)),
            tile(BRANCH_W), tile(BRANCH_W), tile(BRANCH_W),
            pl.BlockSpec((SUBLANES, BRANCH_W), lambda i: (jnp.maximum(i * rb - 1, 0), 0)),
            pl.BlockSpec((SUBLANES, BRANCH_W), lambda i: (jnp.minimum(i * rb + rb, n_rb - 1), 0)),
            _const_spec((SUBLANES, BRANCH_W)),
            tile(BRANCH_W),
            _const_spec((N_BRANCH, BRANCH_W, D_MODEL)),
            _const_spec((D_MODEL, N_BRANCH * D_MODEL)),
            _const_spec((1, N_BRANCH * D_MODEL)),
            _const_spec((D_MODEL, D_MODEL)),
        ],
        out_specs=tile(D_MODEL),
        out_shape=jax.ShapeDtypeStruct((N_TOK, D_MODEL), F32),
        compiler_params=_cparams(("parallel",)),
        name="mixer_merge",
    )(x, mod_l, norm_g, y_ssm, y_ssm, u_ssm, d_skip, w_glu, y_fft, y_att, vconv, vconv, vconv, conv_w, g_b,
      w_branch, w_gate, b_gate, w_out)


FF_CHUNK = D_FF // 2


def _ffn_body(x_ref, mod_ref, g_ref, wi_ref, wo_ref, o_ref):
    x = x_ref[...]
    sh = mod_ref[:, 3 * D_MODEL:4 * D_MODEL]
    sc = mod_ref[:, 4 * D_MODEL:5 * D_MODEL]
    gf = mod_ref[:, 5 * D_MODEL:6 * D_MODEL]
    xn = (_rms(x, g_ref[...]) * (1.0 + sc) + sh).astype(BF16)
    acc = jnp.zeros((T, D_MODEL), F32)
    for c in range(D_FF // FF_CHUNK):
        lo = c * FF_CHUNK
        g = _dot(xn, wi_ref[:, lo:lo + FF_CHUNK])
        u = _dot(xn, wi_ref[:, D_FF + lo:D_FF + lo + FF_CHUNK])
        h = (g * _sigmoid(g) * u).astype(BF16)
        acc = acc + _dot(h, wo_ref[lo:lo + FF_CHUNK, :])
    o_ref[...] = x + gf * acc


def _ffn(x, mod_l, norm_g, w_in, w_out):
    tile = pl.BlockSpec((T, D_MODEL), lambda i: (i, 0))
    return pl.pallas_call(
        _ffn_body,
        grid=(NT,),
        in_specs=[
            tile,
            pl.BlockSpec((None, 1, 6 * D_MODEL), lambda i: (_cond_row(i), 0, 0)),
            _const_spec((1, D_MODEL)),
            _const_spec((D_MODEL, 2 * D_FF)),
            _const_spec((D_FF, D_MODEL)),
        ],
        out_specs=tile,
        out_shape=jax.ShapeDtypeStruct((N_TOK, D_MODEL), F32),
        compiler_params=_cparams(("parallel",)),
        name="ffn",
    )(x, mod_l, norm_g, w_in, w_out)


def _pad_heads(w, lo, width):
    rows = w.shape[0]
    stride = w.shape[1] // MLA_HEADS
    wh = w.reshape(rows, MLA_HEADS, stride)[:, :, lo:lo + width]
    return jnp.pad(wh, ((0, 0), (0, 0), (0, HEAD_PAD - width))).reshape(rows, MLA_HEADS * HEAD_PAD)


def _layer_weights(l, w_in, w_uq, w_ukv, q_norm_g, k_norm_g):
    wi = w_in[l]
    kpe = wi[:, OFF_KPE:OFF_KPE + ROPE_DIM]
    z = lambda n: jnp.zeros((D_MODEL, n), F32)
    w_in_r = jnp.concatenate([
        wi[:, OFF_SSM:OFF_KPE],
        kpe, z(LANES - ROPE_DIM),
        z(NOPE_DIM), kpe[:, ROPE_PERM], z(LANES - QK_DIM),
        z(NOPE_DIM), kpe[:, ROPE_PARTNER], z(LANES - QK_DIM),
        wi[:, OFF_CONV:OFF_CONV + 3 * BRANCH_W],
    ], axis=1).astype(BF16)
    wq = w_uq[l].reshape(Q_RANK, MLA_HEADS, QK_DIM)
    head_pad = lambda a: jnp.pad(a, ((0, 0), (0, 0), (0, HEAD_PAD - QK_DIM))).reshape(
        Q_RANK, MLA_HEADS * HEAD_PAD).astype(BF16)
    wuq = head_pad(jnp.concatenate([wq[:, :, :NOPE_DIM], wq[:, :, NOPE_DIM:][:, :, ROPE_PERM]], axis=2))
    wuqs = head_pad(jnp.concatenate([jnp.zeros_like(wq[:, :, :NOPE_DIM]),
                                     wq[:, :, NOPE_DIM:][:, :, ROPE_PARTNER]], axis=2))
    wuk = _pad_heads(w_ukv[l], 0, NOPE_DIM).astype(BF16)
    wv = w_ukv[l].reshape(KV_RANK, MLA_HEADS, NOPE_DIM + V_DIM)[:, :, NOPE_DIM:]
    even = (jnp.arange(MLA_HEADS) % 2 == 0)[None, :, None]
    wva = jnp.where(even, wv, 0.0).reshape(KV_RANK, BRANCH_W).astype(BF16)
    wvb = jnp.where(even, 0.0, wv).reshape(KV_RANK, BRANCH_W).astype(BF16)

    def gain(g, perm):
        gp = jnp.concatenate([g[:NOPE_DIM], g[NOPE_DIM:][perm], jnp.zeros((HEAD_PAD - QK_DIM,), F32)])
        return gp.reshape(1, HEAD_PAD)

    return (w_in_r, wuq, wuk, wva, wvb, gain(q_norm_g[l], ROPE_PERM), gain(k_norm_g[l], ROPE_PERM),
            wuqs, gain(q_norm_g[l], ROPE_PARTNER), gain(k_norm_g[l], ROPE_PARTNER))


def _rope_tables():
    rows = DEC_SEQ // GRID_W
    r = jnp.repeat(jnp.arange(rows, dtype=F32), GRID_W)
    c = jnp.tile(jnp.arange(GRID_W, dtype=F32), rows)
    n_freq = ROPE_DIM // 4
    inv = ROPE_THETA ** (-jnp.arange(n_freq, dtype=F32) / n_freq)
    ang = jnp.concatenate([r[:, None] * inv, c[:, None] * inv], axis=-1)
    cos, sin = jnp.cos(ang), jnp.sin(ang)
    half = ROPE_DIM // 2
    ones = jnp.ones((DEC_SEQ, NOPE_DIM), F32)
    zeros = jnp.zeros((DEC_SEQ, NOPE_DIM), F32)
    tail1 = jnp.ones((DEC_SEQ, HEAD_PAD - QK_DIM), F32)
    tail0 = jnp.zeros((DEC_SEQ, HEAD_PAD - QK_DIM), F32)
    rc = jnp.concatenate([ones, cos, cos, tail1], axis=1)
    rsw = jnp.concatenate([zeros, -sin, sin, tail0], axis=1)
    ident = lambda v: jnp.full((T, HEAD_PAD), v, F32)
    return jnp.concatenate([rc, ident(1.0)], axis=0), jnp.concatenate([rsw, ident(0.0)], axis=0)


def kernel(x_prompt, x_sample, cache_ckv, cache_kpe, state_ssm, c, c_ctx, norm_mix_g, norm_ffn_g, w_ada, b_ada,
           w_in, q_a_norm_g, kv_a_norm_g, w_uq, w_ukv, q_norm_g, k_norm_g, ssm_lam_re, ssm_lam_im, ssm_log_dt,
           ssm_b_re, ssm_b_im, ssm_c_re, ssm_c_im, ssm_d, w_glu, conv_w, w_branch, w_gate, b_gate, w_out,
           w_ffn_in, w_ffn_out):
    x = jnp.concatenate([x_prompt.reshape(N_CTX, D_MODEL), x_sample.reshape(N_LAT, D_MODEL)], axis=0)

    cond8 = jnp.concatenate([c_ctx[None, :], c, jnp.zeros((SUBLANES - 1 - DEC_BATCH, D_MODEL), F32)], axis=0)
    mod = _modulation(cond8, w_ada, b_ada).reshape(DEPTH, SUBLANES, 1, 6 * D_MODEL)

    bb_re, bb_im, pw_re, pw_im = _ssm_prep(ssm_lam_re, ssm_lam_im, ssm_log_dt, ssm_b_re, ssm_b_im)
    bcat, ccat, pwr, pwi = _ssm_tables(bb_re, bb_im, pw_re, pw_im, ssm_c_re, ssm_c_im)
    pm, pmt3 = _scan_perms()
    st = state_ssm.transpose(1, 0, 2, 5, 3, 4).reshape(DEPTH, DEC_BATCH, 2, 2, N_SLABS, 1, SLAB_LANES)
    st = st.transpose(0, 1, 2, 4, 3, 5, 6)
    h0_all = jnp.concatenate([jnp.zeros((DEPTH, BATCH) + st.shape[2:], F32), st], axis=1)

    rc, rsw = _rope_tables()
    cl, sl = _dft_tables(SEQ)
    csl_ctx = jnp.concatenate([cl, -sl], axis=1).astype(BF16)
    cl, sl = _dft_tables(DEC_SEQ)
    csl_lat = jnp.concatenate([cl, -sl], axis=1).astype(BF16)
    cc, sc = _dft_tables(FFT_GW)
    csc = jnp.concatenate([cc, sc], axis=1).astype(BF16)

    lw = [_layer_weights(l, w_in, w_uq, w_ukv, q_norm_g, k_norm_g) for l in range(DEPTH)]
    cache_kr = jnp.pad(cache_kpe[..., ROPE_PERM], ((0, 0), (0, 0), (0, 0), (NOPE_DIM, HEAD_PAD - QK_DIM)))
    kc, vca, vcb = _cache_kv(cache_ckv, cache_kr,
                             jnp.stack([w[2] for w in lw]), jnp.stack([w[3] for w in lw]),
                             jnp.stack([w[4] for w in lw]), jnp.stack([w[6] for w in lw]))

    conv_w8 = jnp.pad(conv_w, ((0, 0), (0, SUBLANES - 3), (0, 0)))
    lat_blk = N_CTX // DEC_SEQ
    ckv_list, kpe_list, fin_list = [], [], []
    for l in range(DEPTH):
        w_in_r, wuq, wuk, wva, wvb, qng, kng, wuqs, qngs, kngs = lw[l]
        (u_ssm, u_fft, q, k, va, vb, ckv_n, kpe, vconv, g_b) = _pre(
            x, mod[l], norm_mix_g[l][None, :], w_in_r, q_a_norm_g[l][None, :], kv_a_norm_g[l][None, :],
            wuq, wuqs, wuk, wva, wvb, qng, qngs, kng, kngs, rc, rsw)
        ckv_list.append(ckv_n[:N_CTX].reshape(BATCH, SEQ, KV_RANK))
        kpe_list.append(kpe[:N_CTX].reshape(BATCH, SEQ, ROPE_DIM))

        y_ssm, fin = _ssm_scan(u_ssm, pm, pmt3, bcat[l], ccat[l], pwr[l], pwi[l], h0_all[l])
        fin_list.append(fin[:BATCH, :, :, :, SUBLANES - 1, :])

        y_fft = _fnet(u_fft, csl_ctx, csc, SEQ, BATCH, 0)
        y_fft = _fnet(u_fft, csl_lat, csc, DEC_SEQ, DEC_BATCH, NT_CTX, prev=y_fft)

        ctx_seg = (k, va, vb, (T,), lambda b, r: (b, 0))
        y_att = _attention(q, [ctx_seg], BATCH, 1, 0)
        lat_seg = (k, va, vb, (DEC_SEQ,), lambda b, r: (lat_blk + b, 0))
        cache_seg = (kc[l], vca[l], vcb[l], (None, PAST_LEN), lambda b, r: (b, 0, 0))
        y_att = _attention(q, [lat_seg, cache_seg], DEC_BATCH, NT_LAT_SEQ, NT_CTX, prev=y_att)

        x = _mix(x, mod[l], norm_mix_g[l][None, :], y_ssm, u_ssm, ssm_d[l][None, :], w_glu[l].astype(BF16),
                 y_fft, y_att, vconv, conv_w8[l], g_b, w_branch[l].astype(BF16), w_gate[l].astype(BF16),
                 b_gate[l][None, :], w_out[l].astype(BF16))
        x = _ffn(x, mod[l], norm_ffn_g[l][None, :], w_ffn_in[l].astype(BF16), w_ffn_out[l].astype(BF16))

    y_p = x[:N_CTX].reshape(BATCH, SEQ, D_MODEL)
    y_s = x[N_CTX:].reshape(DEC_BATCH, DEC_SEQ, D_MODEL)
    new_ckv = jnp.stack(ckv_list, axis=1)
    new_kpe = jnp.stack(kpe_list, axis=1)
    fin = jnp.stack(fin_list, axis=1)
    fin = fin.reshape(BATCH, DEPTH, 2, N_SLABS, 2, 4, SSM_STATE).transpose(0, 1, 2, 3, 5, 6, 4)
    new_ssm = fin.reshape(BATCH, DEPTH, 2, SSM_GROUPS, SSM_STATE, 2)
    return (y_p, y_s, new_ckv, new_kpe, new_ssm)
```

```python
import functools
import math

import numpy as np
import jax
import jax.numpy as jnp
from jax import lax
from jax.experimental import pallas as pl
from jax.experimental.pallas import tpu as pltpu

F32 = jnp.float32
BF16 = jnp.bfloat16

D_MODEL = 1024
BATCH = 32
SEQ = 256
DEPTH = 4
DEC_BATCH = 2
DEC_SEQ = 2048
PAST_LEN = 512
GRID_W = 64
N_BRANCH = 4
BRANCH_W = 512
SSM_GROUP = 16
SSM_GROUPS = 32
SSM_STATE = 64
FFT_GROUPS = 4
FFT_GW = 128
MLA_HEADS = 8
NOPE_DIM = 64
ROPE_DIM = 32
V_DIM = 64
QK_DIM = 96
Q_RANK = 384
KV_RANK = 256
ROPE_THETA = 10000.0
D_FF = 2816
EPS = 1e-6

OFF_SSM = 0
OFF_FFT = 512
OFF_CQ = 1024
OFF_CKV = 1408
OFF_KPE = 1664
OFF_CONV = 1696

LANES = 128
SUBLANES = 8
VMEM_LIMIT_BYTES = 56 * 1024 * 1024

T = 256
N_CTX = BATCH * SEQ
N_LAT = DEC_BATCH * DEC_SEQ
N_TOK = N_CTX + N_LAT
NT_CTX = N_CTX // T
NT_LAT_SEQ = DEC_SEQ // T
NT = N_TOK // T
N_SEQ = BATCH + DEC_BATCH
HEAD_PAD = LANES

Z_SSM = 0
Z_FFT = 512
Z_CQ = 1024
Z_CKV = 1408
Z_KPE = 1664
Z_KROT = 1792
Z_KSW = 1920
Z_HIN = 2048
Z_GB = 2560
Z_GC = 3072
Z_COLS = 3584

SCAN_STEPS = T // SUBLANES
SLAB_LANES = 256
N_SLABS = SSM_GROUPS * SSM_STATE // SLAB_LANES
ROPE_PERM = np.concatenate([np.arange(0, ROPE_DIM, 2), np.arange(1, ROPE_DIM, 2)])
ROPE_PARTNER = np.concatenate([np.arange(1, ROPE_DIM, 2), np.arange(0, ROPE_DIM, 2)])


def _cparams(sem):
    return pltpu.CompilerParams(dimension_semantics=sem, vmem_limit_bytes=VMEM_LIMIT_BYTES)


def _const_spec(shape):
    nd = len(shape)
    return pl.BlockSpec(shape, lambda *_: (0,) * nd, pipeline_mode=pl.Buffered(1))


def _layer_spec(l, shape):
    nd = len(shape)
    return pl.BlockSpec((None,) + tuple(shape), lambda *_: (l,) + (0,) * nd, pipeline_mode=pl.Buffered(1))


def _mod_spec(l):
    return pl.BlockSpec((None, None, 1, 6 * D_MODEL), lambda i: (l, _cond_row(i), 0, 0))


def _ctx_tile_spec(width):
    return pl.BlockSpec((T, width), lambda i: (jnp.minimum(i, NT_CTX - 1), 0))


def _lat_tile_spec(width):
    return pl.BlockSpec((T, width), lambda i: (jnp.maximum(i - NT_CTX, 0), 0))


def _load_pair(i, ctx_ref, lat_ref):
    return jnp.where(i < NT_CTX, ctx_ref[...], lat_ref[...])


def _store_pair(i, ctx_ref, lat_ref, val):
    @pl.when(i < NT_CTX)
    def _():
        ctx_ref[...] = val

    lat_ref[...] = val


def _dot(a, b):
    return jnp.dot(a, b, preferred_element_type=F32)


def _dot_nt(a, b):
    return lax.dot_general(a, b, (((1,), (1,)), ((), ())), preferred_element_type=F32)


def _sigmoid(x):
    return 1.0 / (1.0 + jnp.exp(-x))


def _rms(x, g):
    return x * lax.rsqrt(jnp.mean(x * x, axis=-1, keepdims=True) + EPS) * g


def _cond_row(i):
    return jnp.maximum(i - (NT_CTX - NT_LAT_SEQ), 0) // NT_LAT_SEQ


def _mod_body(c_ref, w_ref, b_ref, o_ref):
    c = c_ref[...]
    s = c * _sigmoid(c)
    w = w_ref[...]
    s_hi = s.astype(BF16)
    s_lo = (s - s_hi.astype(F32)).astype(BF16)
    w_hi = w.astype(BF16)
    w_lo = (w - w_hi.astype(F32)).astype(BF16)
    o_ref[...] = _dot(s_hi, w_hi) + _dot(s_lo, w_hi) + _dot(s_hi, w_lo) + b_ref[...]


def _modulation(cond8, w_ada, b_ada):
    n_col = 6 * D_MODEL // 1024
    return pl.pallas_call(
        _mod_body,
        grid=(DEPTH, n_col),
        in_specs=[
            pl.BlockSpec((SUBLANES, D_MODEL), lambda l, j: (0, 0)),
            pl.BlockSpec((None, D_MODEL, 1024), lambda l, j: (l, 0, j)),
            pl.BlockSpec((None, 1, 1024), lambda l, j: (l, 0, j)),
        ],
        out_specs=pl.BlockSpec((None, SUBLANES, 1024), lambda l, j: (l, 0, j)),
        out_shape=jax.ShapeDtypeStruct((DEPTH, SUBLANES, 6 * D_MODEL), F32),
        compiler_params=_cparams(("parallel", "parallel")),
        name="modulation",
    )(cond8, w_ada, b_ada.reshape(DEPTH, 1, 6 * D_MODEL))


def _ssm_prep_body(lre_c, lim_c, ldt_c, bre_ref, bim_ref, lre_r, lim_r, ldt_r,
                   bbre_ref, bbim_ref, pwr_ref, pwi_ref):
    lam_re = lre_c[...]
    lam_im = lim_c[...]
    dt = jnp.exp(ldt_c[...])
    mag = jnp.exp(lam_re * dt)
    a_re = mag * jnp.cos(lam_im * dt)
    a_im = mag * jnp.sin(lam_im * dt)
    den = lam_re * lam_re + lam_im * lam_im
    f_re = ((a_re - 1.0) * lam_re + a_im * lam_im) / den
    f_im = (a_im * lam_re - (a_re - 1.0) * lam_im) / den
    b_re = bre_ref[...]
    b_im = bim_ref[...]
    bbre_ref[...] = f_re * b_re - f_im * b_im
    bbim_ref[...] = f_re * b_im + f_im * b_re
    n = (lax.broadcasted_iota(jnp.int32, (SCAN_STEPS, SSM_GROUPS * SSM_STATE), 0) + 1).astype(F32)
    dt_r = jnp.exp(ldt_r[...])
    e = jnp.exp(n * (lre_r[...] * dt_r))
    th = n * (lim_r[...] * dt_r)
    pwr_ref[...] = e * jnp.cos(th)
    pwi_ref[...] = e * jnp.sin(th)


def _ssm_prep(lam_re, lam_im, log_dt, b_re, b_im):
    ld = DEPTH * 2
    gp = SSM_GROUPS * SSM_STATE
    ldt = jnp.broadcast_to(log_dt.reshape(ld, SSM_GROUPS, 1), (ld, SSM_GROUPS, SSM_STATE))
    col = lambda a: a.reshape(ld, gp, 1)
    row = lambda a: a.reshape(ld, 1, gp)
    cspec = pl.BlockSpec((None, gp, 1), lambda i: (i, 0, 0))
    rspec = pl.BlockSpec((None, 1, gp), lambda i: (i, 0, 0))
    bspec = pl.BlockSpec((None, gp, SSM_GROUP), lambda i: (i, 0, 0))
    pspec = pl.BlockSpec((None, SCAN_STEPS, gp), lambda i: (i, 0, 0))
    return pl.pallas_call(
        _ssm_prep_body,
        grid=(ld,),
        in_specs=[cspec, cspec, cspec, bspec, bspec, rspec, rspec, rspec],
        out_specs=[bspec, bspec, pspec, pspec],
        out_shape=[jax.ShapeDtypeStruct((ld, gp, SSM_GROUP), F32)] * 2
        + [jax.ShapeDtypeStruct((ld, SCAN_STEPS, gp), F32)] * 2,
        compiler_params=_cparams(("parallel",)),
        name="ssm_prep",
    )(col(lam_re), col(lam_im), col(ldt), b_re.reshape(ld, gp, SSM_GROUP), b_im.reshape(ld, gp, SSM_GROUP),
      row(lam_re), row(lam_im), row(ldt))


def _ssm_tables(bb_re, bb_im, pw_re, pw_im, c_re, c_im):
    ld = DEPTH * 2
    eye = jnp.eye(4, dtype=F32)
    par = (jnp.arange(N_SLABS) % 2).astype(F32)

    def b_tab(bb):
        t = bb.reshape(ld, N_SLABS, 4, SSM_STATE, SSM_GROUP).transpose(0, 1, 2, 4, 3)
        blk = jnp.einsum('lsgcp,gh->lsgchp', t, eye).reshape(ld, N_SLABS, 64, SLAB_LANES)
        top = blk * (1.0 - par)[None, :, None, None]
        bot = blk * par[None, :, None, None]
        return jnp.concatenate([top, bot], axis=2)

    def c_tab(c):
        t = c.reshape(ld, N_SLABS, 4, SSM_GROUP, SSM_STATE).transpose(0, 1, 2, 4, 3)
        blk = jnp.einsum('lsgpc,gh->lsgphc', t, eye).reshape(ld, N_SLABS, SLAB_LANES, 64)
        left = blk * (1.0 - par)[None, :, None, None]
        right = blk * par[None, :, None, None]
        return jnp.concatenate([left, right], axis=3)

    bcat = jnp.concatenate([b_tab(bb_re), b_tab(bb_im)], axis=3).astype(BF16)
    ccat = jnp.concatenate([c_tab(c_re.reshape(ld, SSM_GROUPS, SSM_GROUP, SSM_STATE)),
                            -c_tab(c_im.reshape(ld, SSM_GROUPS, SSM_GROUP, SSM_STATE))],
                           axis=2).astype(BF16)

    def p_tab(pw):
        t = pw.reshape(ld, SCAN_STEPS, N_SLABS, SLAB_LANES).transpose(0, 2, 1, 3)
        return jnp.broadcast_to(t[:, :, :, None, :], (ld, N_SLABS, SCAN_STEPS, SUBLANES, SLAB_LANES))

    shp = lambda a: a.reshape((DEPTH, 2) + a.shape[1:])
    return shp(bcat), shp(ccat), shp(p_tab(pw_re)), shp(p_tab(pw_im))


def _scan_perms():
    r = np.arange(T)
    t_of_r = (r % SUBLANES) * SCAN_STEPS + r // SUBLANES
    pm = np.zeros((2, T, T), np.float32)
    pm[0, r, t_of_r] = 1.0
    pm[1, r, T - 1 - t_of_r] = 1.0
    pmt = np.transpose(pm, (0, 2, 1))
    pmt3 = np.concatenate([pmt, pmt, pmt], axis=2)
    return jnp.asarray(pm, BF16), jnp.asarray(pmt3, BF16)


def _ssm_body(u_ref, pm_ref, pmt3_ref, bcat_ref, ccat_ref, pwr_ref, pwi_ref, h0_ref,
              y_ref, fin_ref, up_s, xr_all, xi_all, yw_s, st_s):
    i = pl.program_id(1)
    first = jnp.logical_or(i < NT_CTX, i % NT_LAT_SEQ == 0)

    @pl.when(first)
    def _():
        st_s[...] = jnp.broadcast_to(h0_ref[...], st_s.shape)

    ub = u_ref[...].astype(BF16)
    pm = pm_ref[...]
    for w in range(4):
        up_s[w] = _dot(pm, ub[:, w * LANES:(w + 1) * LANES]).astype(BF16)
    yw_s[...] = jnp.zeros_like(yw_s)
    row = lax.broadcasted_iota(jnp.int32, (SUBLANES, SLAB_LANES), 0)

    for s in range(N_SLABS):
        w = s // 2
        xr_s = xr_all.at[s]
        xi_s = xi_all.at[s]
        x = _dot(up_s[w], bcat_ref[s])
        xr_s[...] = x[:, :SLAB_LANES]
        xi_s[...] = x[:, SLAB_LANES:]
        ar = pwr_ref[s, 0]
        ai = pwi_ref[s, 0]
        hr = jnp.zeros((SUBLANES, SLAB_LANES), F32)
        hi = jnp.zeros((SUBLANES, SLAB_LANES), F32)
        for j in range(SCAN_STEPS):
            rows = slice(j * SUBLANES, (j + 1) * SUBLANES)
            nr = ar * hr - ai * hi + xr_s[rows, :]
            ni = ar * hi + ai * hr + xi_s[rows, :]
            hr, hi = nr, ni
            xr_s[rows, :] = hr
            xi_s[rows, :] = hi
        fpr = st_s[s, 0]
        fpi = st_s[s, 1]
        cr = jnp.where(row == 0, pltpu.roll(fpr, 1, 0), pltpu.roll(hr, 1, 0))
        ci = jnp.where(row == 0, pltpu.roll(fpi, 1, 0), pltpu.roll(hi, 1, 0))
        m_r = pwr_ref[s, SCAN_STEPS - 1]
        m_i = pwi_ref[s, SCAN_STEPS - 1]
        qr, qi = m_r, m_i
        for k in (1, 2, 4):
            sr = pltpu.roll(cr, k, 0)
            si = pltpu.roll(ci, k, 0)
            keep = row >= k
            cr, ci = (cr + jnp.where(keep, qr * sr - qi * si, 0.0),
                      ci + jnp.where(keep, qr * si + qi * sr, 0.0))
            qr, qi = qr * qr - qi * qi, 2.0 * qr * qi
        fr = m_r * cr - m_i * ci + hr
        fi = m_r * ci + m_i * cr + hi
        st_s[s, 0] = fr
        st_s[s, 1] = fi
        fin_ref[s, 0] = fr
        fin_ref[s, 1] = fi
        for j in range(SCAN_STEPS):
            rows = slice(j * SUBLANES, (j + 1) * SUBLANES)
            pr = pwr_ref[s, j]
            pi = pwi_ref[s, j]
            xr_s[rows, :] = xr_s[rows, :] + pr * cr - pi * ci
            xi_s[rows, :] = xi_s[rows, :] + pr * ci + pi * cr
        hb = jnp.concatenate([xr_s[...], xi_s[...]], axis=1).astype(BF16)
        yw_s[w] = yw_s[w] + _dot(hb, ccat_ref[s])

    y = jnp.concatenate([yw_s[0], yw_s[1], yw_s[2], yw_s[3]], axis=1)
    y_hi = y.astype(BF16)
    r1 = y - y_hi.astype(F32)
    y_mid = r1.astype(BF16)
    y_lo = (r1 - y_mid.astype(F32)).astype(BF16)
    y3 = jnp.concatenate([y_hi, y_mid, y_lo], axis=0)
    y_ref[...] = _dot(pmt3_ref[...], y3)


def _ssm_tile(d, i):
    rev = (i // NT_LAT_SEQ) * NT_LAT_SEQ + (NT_LAT_SEQ - 1) - i % NT_LAT_SEQ
    return jnp.where(jnp.logical_and(d == 1, i >= NT_CTX), rev, i)


def _ssm_seq(i):
    return jnp.where(i < NT_CTX, i, NT_CTX + (i - NT_CTX) // NT_LAT_SEQ)


def _ssm_scan(u_ssm, pm, pmt3, bcat, ccat, pwr, pwi, h0):
    tab = lambda *shape: pl.BlockSpec((None,) + shape, lambda d, i: (d,) + (0,) * len(shape))
    return pl.pallas_call(
        _ssm_body,
        grid=(2, NT),
        in_specs=[
            pl.BlockSpec((T, BRANCH_W), lambda d, i: (_ssm_tile(d, i), 0)),
            tab(T, T), tab(T, 3 * T),
            tab(N_SLABS, LANES, 2 * SLAB_LANES), tab(N_SLABS, 2 * SLAB_LANES, LANES),
            tab(N_SLABS, SCAN_STEPS, SUBLANES, SLAB_LANES), tab(N_SLABS, SCAN_STEPS, SUBLANES, SLAB_LANES),
            pl.BlockSpec((None, None, N_SLABS, 2, 1, SLAB_LANES), lambda d, i: (_ssm_seq(i), d, 0, 0, 0, 0)),
        ],
        out_specs=[
            pl.BlockSpec((None, T, BRANCH_W), lambda d, i: (d, _ssm_tile(d, i), 0)),
            pl.BlockSpec((None, None, N_SLABS, 2, SUBLANES, SLAB_LANES),
                         lambda d, i: (_ssm_seq(i), d, 0, 0, 0, 0)),
        ],
        out_shape=[jax.ShapeDtypeStruct((2, N_TOK, BRANCH_W), F32),
                   jax.ShapeDtypeStruct((N_SEQ, 2, N_SLABS, 2, SUBLANES, SLAB_LANES), F32)],
        scratch_shapes=[
            pltpu.VMEM((4, T, LANES), BF16),
            pltpu.VMEM((N_SLABS, T, SLAB_LANES), F32),
            pltpu.VMEM((N_SLABS, T, SLAB_LANES), F32),
            pltpu.VMEM((4, T, LANES), F32),
            pltpu.VMEM((N_SLABS, 2, SUBLANES, SLAB_LANES), F32),
        ],
        compiler_params=_cparams(("arbitrary", "arbitrary")),
        name="ssm_scan",
    )(u_ssm, pm, pmt3, bcat, ccat, pwr, pwi, h0)


def _pre_body(xp_ref, xs_ref, mod_ref, g_ref, win_ref, qag_ref, kvg_ref, wuq_ref, wuqs_ref, wuk_ref, wva_ref,
              wvb_ref, qng_ref, qngs_ref, kng_ref, kngs_ref, rc_ref, rsw_ref,
              ussm_ref, ufft_ref, q_ref, k_ref, va_ref, vb_ref, ckv_ref, kpe_ref, vc_ref, gb_ref):
    i = pl.program_id(0)
    x = _load_pair(i, xp_ref, xs_ref)
    sh = mod_ref[:, 0:D_MODEL]
    sc = mod_ref[:, D_MODEL:2 * D_MODEL]
    xn = (_rms(x, g_ref[...]) * (1.0 + sc) + sh).astype(BF16)

    def proj(lo, hi):
        return _dot(xn, win_ref[:, lo:hi])

    ussm_ref[...] = proj(Z_SSM, Z_FFT)
    ufft_ref[...] = proj(Z_FFT, Z_CQ).astype(BF16)
    zk = proj(Z_KPE, Z_KROT)
    krot = proj(Z_KROT, Z_KSW)
    kswp = proj(Z_KSW, Z_HIN)
    h_in = proj(Z_HIN, Z_GB)
    gb_ref[...] = proj(Z_GB, Z_GC)
    vc_ref[...] = proj(Z_GC, Z_COLS) * h_in

    rc = rc_ref[...]
    rsw = rsw_ref[...]
    q_gc = qng_ref[...] * rc
    q_gs = qngs_ref[...] * rsw
    k_gc = kng_ref[...] * rc
    k_sw = kswp * (kngs_ref[...] * rsw)

    def rstd(v):
        return lax.rsqrt(jnp.sum(v * v, axis=-1, keepdims=True) * (1.0 / QK_DIM) + EPS)

    cqn = _rms(proj(Z_CQ, Z_CKV), qag_ref[...]).astype(BF16)
    qf = _dot(cqn, wuq_ref[...])
    qs = _dot(cqn, wuqs_ref[...])
    ckvn = _rms(proj(Z_CKV, Z_KPE), kvg_ref[...])

    @pl.when(i < NT_CTX)
    def _():
        ckv_ref[...] = ckvn
        kpe_ref[...] = zk[:, 0:ROPE_DIM]

    ckvb = ckvn.astype(BF16)
    kf = _dot(ckvb, wuk_ref[...])
    va_ref[...] = _dot(ckvb, wva_ref[...]).astype(BF16)
    vb_ref[...] = _dot(ckvb, wvb_ref[...]).astype(BF16)
    for h in range(MLA_HEADS):
        lanes = slice(h * HEAD_PAD, (h + 1) * HEAD_PAD)
        qh = qf[:, lanes]
        q_ref[:, lanes] = ((qh * q_gc + qs[:, lanes] * q_gs) * rstd(qh)).astype(BF16)
        kh = kf[:, lanes] + krot
        k_ref[:, lanes] = ((kh * k_gc + k_sw) * rstd(kh)).astype(BF16)


def _pre(l, xp, xs, mod, p, rc, rsw):
    tile = lambda width: pl.BlockSpec((T, width), lambda i: (i, 0))
    rope_blk = lambda i: (jnp.where(i < NT_CTX, NT_LAT_SEQ, i % NT_LAT_SEQ), 0)
    rspec = pl.BlockSpec((T, LANES), rope_blk)
    out = lambda width, dt: jax.ShapeDtypeStruct((N_TOK, width), dt)
    names = ('norm_mix_g', 'w_in_r', 'q_a_norm_g', 'kv_a_norm_g', 'wuq', 'wuqs', 'wuk', 'wva', 'wvb',
             'qng', 'qngs', 'kng', 'kngs')
    return pl.pallas_call(
        _pre_body,
        grid=(NT,),
        in_specs=[_ctx_tile_spec(D_MODEL), _lat_tile_spec(D_MODEL), _mod_spec(l)]
        + [_layer_spec(l, p[n].shape[1:]) for n in names] + [rspec, rspec],
        out_specs=[tile(BRANCH_W), tile(BRANCH_W), tile(MLA_HEADS * HEAD_PAD), tile(MLA_HEADS * HEAD_PAD),
                   tile(BRANCH_W), tile(BRANCH_W), _ctx_tile_spec(KV_RANK), _ctx_tile_spec(ROPE_DIM),
                   tile(BRANCH_W), tile(BRANCH_W)],
        out_shape=[out(BRANCH_W, F32), out(BRANCH_W, BF16), out(MLA_HEADS * HEAD_PAD, BF16),
                   out(MLA_HEADS * HEAD_PAD, BF16), out(BRANCH_W, BF16), out(BRANCH_W, BF16),
                   jax.ShapeDtypeStruct((N_CTX, KV_RANK), F32), jax.ShapeDtypeStruct((N_CTX, ROPE_DIM), F32),
                   out(BRANCH_W, F32), out(BRANCH_W, F32)],
        compiler_params=_cparams(("arbitrary",)),
        name="pre_mixer",
    )(xp, xs, mod, *[p[n] for n in names], rc, rsw)


def _kvc_body(ckv_ref, kr_ref, wuk_ref, wva_ref, wvb_ref, kng_ref, k_ref, va_ref, vb_ref):
    ckvb = ckv_ref[...].astype(BF16)
    kf = _dot(ckvb, wuk_ref[...])
    va_ref[...] = _dot(ckvb, wva_ref[...]).astype(BF16)
    vb_ref[...] = _dot(ckvb, wvb_ref[...]).astype(BF16)
    krot = kr_ref[...]
    g = kng_ref[...]
    for h in range(MLA_HEADS):
        lanes = slice(h * HEAD_PAD, (h + 1) * HEAD_PAD)
        kh = kf[:, lanes] + krot
        ss = jnp.sum(kh * kh, axis=-1, keepdims=True)
        k_ref[:, lanes] = (kh * lax.rsqrt(ss * (1.0 / QK_DIM) + EPS) * g).astype(BF16)


def _cache_kv(cache_ckv, cache_kr, wuk, wva, wvb, kng):
    nt = PAST_LEN // T
    wspec = lambda r, c: pl.BlockSpec((None, r, c), lambda l, b, t: (l, 0, 0))
    ospec = lambda c: pl.BlockSpec((None, None, T, c), lambda l, b, t: (l, b, t, 0))
    return pl.pallas_call(
        _kvc_body,
        grid=(DEPTH, DEC_BATCH, nt),
        in_specs=[
            pl.BlockSpec((None, None, T, KV_RANK), lambda l, b, t: (b, l, t, 0)),
            pl.BlockSpec((None, None, T, HEAD_PAD), lambda l, b, t: (b, l, t, 0)),
            wspec(KV_RANK, MLA_HEADS * HEAD_PAD), wspec(KV_RANK, BRANCH_W), wspec(KV_RANK, BRANCH_W),
            wspec(1, HEAD_PAD),
        ],
        out_specs=[ospec(MLA_HEADS * HEAD_PAD), ospec(BRANCH_W), ospec(BRANCH_W)],
        out_shape=[jax.ShapeDtypeStruct((DEPTH, DEC_BATCH, PAST_LEN, MLA_HEADS * HEAD_PAD), BF16),
                   jax.ShapeDtypeStruct((DEPTH, DEC_BATCH, PAST_LEN, BRANCH_W), BF16),
                   jax.ShapeDtypeStruct((DEPTH, DEC_BATCH, PAST_LEN, BRANCH_W), BF16)],
        compiler_params=_cparams(("parallel", "parallel", "parallel")),
        name="cache_kv",
    )(cache_ckv, cache_kr, wuk, wva, wvb, kng)


def _fnet_body(seq_len, u_ref, csl_ref, csc_ref, y_ref, v1_s, v2_s):
    r = pl.program_id(1)

    @pl.when(r == 0)
    def _():
        csc = csc_ref[...]
        for c in range(seq_len // T):
            rows = slice(c * T, (c + 1) * T)
            for g in range(FFT_GROUPS):
                lanes = slice(g * FFT_GW, (g + 1) * FFT_GW)
                v = _dot(u_ref[rows, lanes], csc)
                v1_s[rows, lanes] = v[:, :FFT_GW].astype(BF16)
                v2_s[rows, lanes] = v[:, FFT_GW:].astype(BF16)

    y = _dot(csl_ref[:, :seq_len], v1_s[...]) + _dot(csl_ref[:, seq_len:], v2_s[...])
    y_ref[...] = y.astype(BF16)


def _fnet(u_fft, csl, csc, seq_len, n_seq, tile0):
    nr = seq_len // T
    blk0 = tile0 * T // seq_len
    return pl.pallas_call(
        functools.partial(_fnet_body, seq_len),
        grid=(n_seq, nr),
        in_specs=[
            pl.BlockSpec((seq_len, BRANCH_W), lambda b, r: (blk0 + b, 0)),
            pl.BlockSpec((T, 2 * seq_len), lambda b, r: (r, 0)),
            _const_spec((FFT_GW, 2 * FFT_GW)),
        ],
        out_specs=pl.BlockSpec((T, BRANCH_W), lambda b, r: (b * nr + r, 0)),
        out_shape=jax.ShapeDtypeStruct((n_seq * seq_len, BRANCH_W), BF16),
        scratch_shapes=[pltpu.VMEM((seq_len, BRANCH_W), BF16), pltpu.VMEM((seq_len, BRANCH_W), BF16)],
        compiler_params=_cparams(("arbitrary", "arbitrary")),
        name="fnet_%d" % seq_len,
    )(u_fft, csl, csc)


def _dft_tables(n, split):
    m = jnp.arange(n, dtype=jnp.int32)
    a = jnp.arange(n // split, dtype=jnp.int32) * split
    b = jnp.arange(split, dtype=jnp.int32)
    w = 2.0 * math.pi / n
    ang_a = ((a[:, None] * m[None, :]) % n).astype(F32) * w
    ang_b = ((b[:, None] * m[None, :]) % n).astype(F32) * w
    ca, sa = jnp.cos(ang_a)[:, None, :], jnp.sin(ang_a)[:, None, :]
    cb, sb = jnp.cos(ang_b)[None, :, :], jnp.sin(ang_b)[None, :, :]
    s = 1.0 / math.sqrt(n)
    cos = ((ca * cb - sa * sb) * s).reshape(n, n)
    sin = ((sa * cb + ca * sb) * s).reshape(n, n)
    return cos, sin


def _attn_body(n_seg, q_ref, *refs):
    k_refs = refs[0:n_seg]
    va_refs = refs[n_seg:2 * n_seg]
    vb_refs = refs[2 * n_seg:3 * n_seg]
    o_ref = refs[-1]
    scale = QK_DIM ** -0.5
    for hp in range(MLA_HEADS // 2):
        out_lanes = slice(hp * LANES, (hp + 1) * LANES)
        acc = jnp.zeros((T, LANES), F32)
        for h, v_refs in ((2 * hp, va_refs), (2 * hp + 1, vb_refs)):
            lanes = slice(h * HEAD_PAD, (h + 1) * HEAD_PAD)
            qh = q_ref[:, lanes]
            ss = [_dot_nt(qh, kr[:, lanes]) * scale for kr in k_refs]
            m = functools.reduce(jnp.maximum, [jnp.max(s, axis=-1, keepdims=True) for s in ss])
            ps = [jnp.exp(s - m) for s in ss]
            l = functools.reduce(lambda a, b: a + b, [jnp.sum(p, axis=-1, keepdims=True) for p in ps])
            o = functools.reduce(lambda a, b: a + b,
                                 [_dot(p.astype(BF16), vr[:, out_lanes]) for p, vr in zip(ps, v_refs)])
            acc = acc + o / l
        o_ref[:, out_lanes] = acc.astype(BF16)


def _attention(q, segs, n_seq, nq, tile0):
    in_specs = [pl.BlockSpec((T, MLA_HEADS * HEAD_PAD), lambda b, r: (tile0 + b * nq + r, 0))]
    args = [q]
    for idx in range(3):
        for seg in segs:
            arr = seg[idx]
            width = arr.shape[-1]
            in_specs.append(pl.BlockSpec(seg[3] + (width,), seg[4]))
            args.append(arr)
    return pl.pallas_call(
        functools.partial(_attn_body, len(segs)),
        grid=(n_seq, nq),
        in_specs=in_specs,
        out_specs=pl.BlockSpec((T, BRANCH_W), lambda b, r: (b * nq + r, 0)),
        out_shape=jax.ShapeDtypeStruct((n_seq * nq * T, BRANCH_W), BF16),
        compiler_params=_cparams(("arbitrary", "arbitrary")),
        name="attention_%d" % nq,
    )(*args)


def _mix_body(xp_ref, xs_ref, mod_ref, g_ref, yf_ref, yb_ref, u_ref, dsk_ref, wglu_ref, fc_ref, fl_ref,
              ac_ref, al_ref, vc_ref, vp_ref, vn_ref, cw_ref, gb_ref, wbr_ref, wg_ref, bg_ref, wo_ref,
              op_ref, os_ref):
    i = pl.program_id(0)
    x = _load_pair(i, xp_ref, xs_ref)
    sh = mod_ref[:, 0:D_MODEL]
    sc = mod_ref[:, D_MODEL:2 * D_MODEL]
    gm = mod_ref[:, 2 * D_MODEL:3 * D_MODEL]
    xn = (_rms(x, g_ref[...]) * (1.0 + sc) + sh).astype(BF16)

    y = jax.nn.gelu(yf_ref[...] + yb_ref[...] + u_ref[...] * dsk_ref[...], approximate=True)
    y = y * _sigmoid(_dot(y.astype(BF16), wglu_ref[...]))

    v = vc_ref[...]
    row = lax.broadcasted_iota(jnp.int32, (T, BRANCH_W), 0)
    is_lat = i >= NT_CTX
    has_prev = jnp.logical_and(is_lat, i % NT_LAT_SEQ != 0)
    has_next = jnp.logical_and(is_lat, i % NT_LAT_SEQ != NT_LAT_SEQ - 1)
    halo_p = jnp.where(has_prev, vp_ref[SUBLANES - 1:SUBLANES, :], 0.0)
    halo_n = jnp.where(has_next, vn_ref[0:1, :], 0.0)
    v_prev = jnp.where(row == 0, halo_p, pltpu.roll(v, 1, 0))
    v_next = jnp.where(row == T - 1, halo_n, pltpu.roll(v, T - 1, 0))
    y_conv = gb_ref[...] * (cw_ref[0:1, :] * v_prev + cw_ref[1:2, :] * v + cw_ref[2:3, :] * v_next)

    branches = (y.astype(BF16), _load_pair(i, fc_ref, fl_ref), _load_pair(i, ac_ref, al_ref),
                y_conv.astype(BF16))
    merged = jnp.zeros((T, D_MODEL), F32)
    for k in range(N_BRANCH):
        cols = slice(k * D_MODEL, (k + 1) * D_MODEL)
        gate = _sigmoid(_dot(xn, wg_ref[:, cols]) + bg_ref[:, cols])
        merged = merged + gate * _dot(branches[k], wbr_ref[k])
    out = _dot(merged.astype(BF16), wo_ref[...])
    _store_pair(i, op_ref, os_ref, x + gm * out)


def _mix(l, xp, xs, mod, p, y_ssm, u_ssm, fft_c, fft_l, att_c, att_l, vconv, g_b):
    tile = lambda width: pl.BlockSpec((T, width), lambda i: (i, 0))
    rb = T // SUBLANES
    n_rb = N_TOK // SUBLANES
    lspec = lambda n: _layer_spec(l, p[n].shape[1:])
    return pl.pallas_call(
        _mix_body,
        grid=(NT,),
        in_specs=[
            _ctx_tile_spec(D_MODEL), _lat_tile_spec(D_MODEL), _mod_spec(l), lspec('norm_mix_g'),
            pl.BlockSpec((None, T, BRANCH_W), lambda i: (0, i, 0)),
            pl.BlockSpec((None, T, BRANCH_W), lambda i: (1, i, 0)),
            tile(BRANCH_W), lspec('ssm_d'), lspec('w_glu'),
            _ctx_tile_spec(BRANCH_W), _lat_tile_spec(BRANCH_W), _ctx_tile_spec(BRANCH_W), _lat_tile_spec(BRANCH_W),
            tile(BRANCH_W),
            pl.BlockSpec((SUBLANES, BRANCH_W), lambda i: (jnp.maximum(i * rb - 1, 0), 0)),
            pl.BlockSpec((SUBLANES, BRANCH_W), lambda i: (jnp.minimum(i * rb + rb, n_rb - 1), 0)),
            lspec('conv_w'), tile(BRANCH_W),
            lspec('w_branch'), lspec('w_gate'), lspec('b_gate'), lspec('w_out'),
        ],
        out_specs=[_ctx_tile_spec(D_MODEL), _lat_tile_spec(D_MODEL)],
        out_shape=[jax.ShapeDtypeStruct((N_CTX, D_MODEL), F32), jax.ShapeDtypeStruct((N_LAT, D_MODEL), F32)],
        compiler_params=_cparams(("arbitrary",)),
        name="mixer_merge",
    )(xp, xs, mod, p['norm_mix_g'], y_ssm, y_ssm, u_ssm, p['ssm_d'], p['w_glu'], fft_c, fft_l, att_c, att_l,
      vconv, vconv, vconv, p['conv_w'], g_b, p['w_branch'], p['w_gate'], p['b_gate'], p['w_out'])


FF_CHUNK = D_FF // 2


def _ffn_body(xp_ref, xs_ref, mod_ref, g_ref, wi_ref, wo_ref, op_ref, os_ref):
    i = pl.program_id(0)
    x = _load_pair(i, xp_ref, xs_ref)
    sh = mod_ref[:, 3 * D_MODEL:4 * D_MODEL]
    sc = mod_ref[:, 4 * D_MODEL:5 * D_MODEL]
    gf = mod_ref[:, 5 * D_MODEL:6 * D_MODEL]
    xn = (_rms(x, g_ref[...]) * (1.0 + sc) + sh).astype(BF16)
    acc = jnp.zeros((T, D_MODEL), F32)
    for c in range(D_FF // FF_CHUNK):
        lo = c * FF_CHUNK
        g = _dot(xn, wi_ref[:, lo:lo + FF_CHUNK])
        u = _dot(xn, wi_ref[:, D_FF + lo:D_FF + lo + FF_CHUNK])
        h = (g * _sigmoid(g) * u).astype(BF16)
        acc = acc + _dot(h, wo_ref[lo:lo + FF_CHUNK, :])
    _store_pair(i, op_ref, os_ref, x + gf * acc)


def _ffn(l, xp, xs, mod, p):
    lspec = lambda n: _layer_spec(l, p[n].shape[1:])
    return pl.pallas_call(
        _ffn_body,
        grid=(NT,),
        in_specs=[_ctx_tile_spec(D_MODEL), _lat_tile_spec(D_MODEL), _mod_spec(l), lspec('norm_ffn_g'),
                  lspec('w_ffn_in'), lspec('w_ffn_out')],
        out_specs=[_ctx_tile_spec(D_MODEL), _lat_tile_spec(D_MODEL)],
        out_shape=[jax.ShapeDtypeStruct((N_CTX, D_MODEL), F32), jax.ShapeDtypeStruct((N_LAT, D_MODEL), F32)],
        compiler_params=_cparams(("arbitrary",)),
        name="ffn",
    )(xp, xs, mod, p['norm_ffn_g'], p['w_ffn_in'], p['w_ffn_out'])


def _prep_weights(norm_mix_g, norm_ffn_g, w_in, q_a_norm_g, kv_a_norm_g, w_uq, w_ukv, q_norm_g, k_norm_g,
                  ssm_d, w_glu, conv_w, w_branch, w_gate, b_gate, w_out, w_ffn_in, w_ffn_out):
    vec = lambda a: a.reshape(DEPTH, 1, a.shape[-1])
    kpe = w_in[:, :, OFF_KPE:OFF_KPE + ROPE_DIM]
    z = lambda n: jnp.zeros((DEPTH, D_MODEL, n), F32)
    w_in_r = jnp.concatenate([
        w_in[:, :, OFF_SSM:OFF_KPE],
        kpe, z(LANES - ROPE_DIM),
        z(NOPE_DIM), kpe[:, :, ROPE_PERM], z(LANES - QK_DIM),
        z(NOPE_DIM), kpe[:, :, ROPE_PARTNER], z(LANES - QK_DIM),
        w_in[:, :, OFF_CONV:OFF_CONV + 3 * BRANCH_W],
    ], axis=2).astype(BF16)
    wq = w_uq.reshape(DEPTH, Q_RANK, MLA_HEADS, QK_DIM)
    head_pad = lambda a: jnp.pad(a, ((0, 0), (0, 0), (0, 0), (0, HEAD_PAD - a.shape[-1]))).reshape(
        DEPTH, a.shape[1], MLA_HEADS * HEAD_PAD).astype(BF16)
    wuq = head_pad(jnp.concatenate([wq[..., :NOPE_DIM], wq[..., NOPE_DIM:][..., ROPE_PERM]], axis=3))
    wuqs = head_pad(jnp.concatenate([jnp.zeros_like(wq[..., :NOPE_DIM]), wq[..., NOPE_DIM:][..., ROPE_PARTNER]],
                                    axis=3))
    wkv = w_ukv.reshape(DEPTH, KV_RANK, MLA_HEADS, NOPE_DIM + V_DIM)
    wuk = head_pad(wkv[..., :NOPE_DIM])
    wv = wkv[..., NOPE_DIM:]
    even = (jnp.arange(MLA_HEADS) % 2 == 0)[None, None, :, None]
    wva = jnp.where(even, wv, 0.0).reshape(DEPTH, KV_RANK, BRANCH_W).astype(BF16)
    wvb = jnp.where(even, 0.0, wv).reshape(DEPTH, KV_RANK, BRANCH_W).astype(BF16)

    def gain(g, perm):
        gp = jnp.concatenate([g[:, :NOPE_DIM], g[:, NOPE_DIM:][:, perm],
                              jnp.zeros((DEPTH, HEAD_PAD - QK_DIM), F32)], axis=1)
        return gp.reshape(DEPTH, 1, HEAD_PAD)

    return dict(
        norm_mix_g=vec(norm_mix_g), norm_ffn_g=vec(norm_ffn_g), w_in_r=w_in_r,
        q_a_norm_g=vec(q_a_norm_g), kv_a_norm_g=vec(kv_a_norm_g),
        wuq=wuq, wuqs=wuqs, wuk=wuk, wva=wva, wvb=wvb,
        qng=gain(q_norm_g, ROPE_PERM), qngs=gain(q_norm_g, ROPE_PARTNER),
        kng=gain(k_norm_g, ROPE_PERM), kngs=gain(k_norm_g, ROPE_PARTNER),
        ssm_d=vec(ssm_d), w_glu=w_glu.astype(BF16),
        conv_w=jnp.pad(conv_w, ((0, 0), (0, SUBLANES - 3), (0, 0))),
        w_branch=w_branch.astype(BF16), w_gate=w_gate.astype(BF16), b_gate=vec(b_gate),
        w_out=w_out.astype(BF16), w_ffn_in=w_ffn_in.astype(BF16), w_ffn_out=w_ffn_out.astype(BF16),
    )


def _rope_tables():
    rows = DEC_SEQ // GRID_W
    r = jnp.repeat(jnp.arange(rows, dtype=F32), GRID_W)
    c = jnp.tile(jnp.arange(GRID_W, dtype=F32), rows)
    n_freq = ROPE_DIM // 4
    inv = ROPE_THETA ** (-jnp.arange(n_freq, dtype=F32) / n_freq)
    ang = jnp.concatenate([r[:, None] * inv, c[:, None] * inv], axis=-1)
    cos, sin = jnp.cos(ang), jnp.sin(ang)
    ones = jnp.ones((DEC_SEQ, NOPE_DIM), F32)
    zeros = jnp.zeros((DEC_SEQ, NOPE_DIM), F32)
    tail1 = jnp.ones((DEC_SEQ, HEAD_PAD - QK_DIM), F32)
    tail0 = jnp.zeros((DEC_SEQ, HEAD_PAD - QK_DIM), F32)
    rc = jnp.concatenate([ones, cos, cos, tail1], axis=1)
    rsw = jnp.concatenate([zeros, -sin, sin, tail0], axis=1)
    ident = lambda v: jnp.full((T, HEAD_PAD), v, F32)
    return jnp.concatenate([rc, ident(1.0)], axis=0), jnp.concatenate([rsw, ident(0.0)], axis=0)


def kernel(x_prompt, x_sample, cache_ckv, cache_kpe, state_ssm, c, c_ctx, norm_mix_g, norm_ffn_g, w_ada, b_ada,
           w_in, q_a_norm_g, kv_a_norm_g, w_uq, w_ukv, q_norm_g, k_norm_g, ssm_lam_re, ssm_lam_im, ssm_log_dt,
           ssm_b_re, ssm_b_im, ssm_c_re, ssm_c_im, ssm_d, w_glu, conv_w, w_branch, w_gate, b_gate, w_out,
           w_ffn_in, w_ffn_out):
    xp = x_prompt.reshape(N_CTX, D_MODEL)
    xs = x_sample.reshape(N_LAT, D_MODEL)

    cond8 = jnp.concatenate([c_ctx[None, :], c, jnp.zeros((SUBLANES - 1 - DEC_BATCH, D_MODEL), F32)], axis=0)
    mod = _modulation(cond8, w_ada, b_ada).reshape(DEPTH, SUBLANES, 1, 6 * D_MODEL)

    bb_re, bb_im, pw_re, pw_im = _ssm_prep(ssm_lam_re, ssm_lam_im, ssm_log_dt, ssm_b_re, ssm_b_im)
    bcat, ccat, pwr, pwi = _ssm_tables(bb_re, bb_im, pw_re, pw_im, ssm_c_re, ssm_c_im)
    pm, pmt3 = _scan_perms()
    st = state_ssm.transpose(1, 0, 2, 5, 3, 4).reshape(DEPTH, DEC_BATCH, 2, 2, N_SLABS, 1, SLAB_LANES)
    st = st.transpose(0, 1, 2, 4, 3, 5, 6)
    h0_all = jnp.concatenate([jnp.zeros((DEPTH, BATCH) + st.shape[2:], F32), st], axis=1)

    rc, rsw = _rope_tables()
    cl, sl = _dft_tables(SEQ, 16)
    csl_ctx = jnp.concatenate([cl, -sl], axis=1).astype(BF16)
    cl, sl = _dft_tables(DEC_SEQ, 64)
    csl_lat = jnp.concatenate([cl, -sl], axis=1).astype(BF16)
    cc, sc = _dft_tables(FFT_GW, 8)
    csc = jnp.concatenate([cc, sc], axis=1).astype(BF16)

    p = _prep_weights(norm_mix_g, norm_ffn_g, w_in, q_a_norm_g, kv_a_norm_g, w_uq, w_ukv, q_norm_g, k_norm_g,
                      ssm_d, w_glu, conv_w, w_branch, w_gate, b_gate, w_out, w_ffn_in, w_ffn_out)
    cache_kr = jnp.pad(cache_kpe[..., ROPE_PERM], ((0, 0), (0, 0), (0, 0), (NOPE_DIM, HEAD_PAD - QK_DIM)))
    kc, vca, vcb = _cache_kv(cache_ckv, cache_kr, p['wuk'], p['wva'], p['wvb'], p['kng'])

    lat_blk = N_CTX // DEC_SEQ
    ckv_list, kpe_list, fin_list = [], [], []
    for l in range(DEPTH):
        (u_ssm, u_fft, q, k, va, vb, ckv_n, kpe, vconv, g_b) = _pre(l, xp, xs, mod, p, rc, rsw)
        ckv_list.append(ckv_n.reshape(BATCH, SEQ, KV_RANK))
        kpe_list.append(kpe.reshape(BATCH, SEQ, ROPE_DIM))

        y_ssm, fin = _ssm_scan(u_ssm, pm, pmt3, bcat[l], ccat[l], pwr[l], pwi[l], h0_all[l])
        fin_list.append(fin[:BATCH, :, :, :, SUBLANES - 1, :])

        fft_c = _fnet(u_fft, csl_ctx, csc, SEQ, BATCH, 0)
        fft_l = _fnet(u_fft, csl_lat, csc, DEC_SEQ, DEC_BATCH, NT_CTX)

        ctx_seg = (k, va, vb, (T,), lambda b, r: (b, 0))
        att_c = _attention(q, [ctx_seg], BATCH, 1, 0)
        lat_seg = (k, va, vb, (DEC_SEQ,), lambda b, r: (lat_blk + b, 0))
        cache_seg = (kc[l], vca[l], vcb[l], (None, PAST_LEN), lambda b, r: (b, 0, 0))
        att_l = _attention(q, [lat_seg, cache_seg], DEC_BATCH, NT_LAT_SEQ, NT_CTX)

        xp, xs = _mix(l, xp, xs, mod, p, y_ssm, u_ssm, fft_c, fft_l, att_c, att_l, vconv, g_b)
        xp, xs = _ffn(l, xp, xs, mod, p)

    y_p = xp.reshape(BATCH, SEQ, D_MODEL)
    y_s = xs.reshape(DEC_BATCH, DEC_SEQ, D_MODEL)
    new_ckv = jnp.stack(ckv_list, axis=1)
    new_kpe = jnp.stack(kpe_list, axis=1)
    fin = jnp.stack(fin_list, axis=1)
    fin = fin.reshape(BATCH, DEPTH, 2, N_SLABS, 2, 4, SSM_STATE).transpose(0, 1, 2, 3, 5, 6, 4)
    new_ssm = fin.reshape(BATCH, DEPTH, 2, SSM_GROUPS, SSM_STATE, 2)
    return (y_p, y_s, new_ckv, new_kpe, new_ssm)
```

```python
import functools
import math

import numpy as np
import jax
import jax.numpy as jnp
from jax import lax
from jax.experimental import pallas as pl
from jax.experimental.pallas import tpu as pltpu

F32 = jnp.float32
BF16 = jnp.bfloat16

D_MODEL = 1024
BATCH = 32
SEQ = 256
DEPTH = 4
DEC_BATCH = 2
DEC_SEQ = 2048
PAST_LEN = 512
GRID_W = 64
N_BRANCH = 4
BRANCH_W = 512
SSM_GROUP = 16
SSM_GROUPS = 32
SSM_STATE = 64
FFT_GROUPS = 4
FFT_GW = 128
MLA_HEADS = 8
NOPE_DIM = 64
ROPE_DIM = 32
V_DIM = 64
QK_DIM = 96
Q_RANK = 384
KV_RANK = 256
ROPE_THETA = 10000.0
D_FF = 2816
EPS = 1e-6

OFF_SSM = 0
OFF_FFT = 512
OFF_CQ = 1024
OFF_CKV = 1408
OFF_KPE = 1664
OFF_CONV = 1696

LANES = 128
SUBLANES = 8
VMEM_LIMIT_BYTES = 56 * 1024 * 1024

T = 256
N_CTX = BATCH * SEQ
N_LAT = DEC_BATCH * DEC_SEQ
N_TOK = N_CTX + N_LAT
NT_CTX = N_CTX // T
NT_LAT_SEQ = DEC_SEQ // T
NT = N_TOK // T
N_SEQ = BATCH + DEC_BATCH
HEAD_PAD = LANES
ATTN_SCALE = QK_DIM ** -0.5

Z_SSM = 0
Z_FFT = 512
Z_CQ = 1024
Z_CKV = 1408
Z_KPE = 1664
Z_KROT = 1792
Z_KSW = 1920
Z_HIN = 2048
Z_GB = 2560
Z_GC = 3072
Z_COLS = 3584

SCAN_R = 4
SCAN_M = T // SUBLANES // SCAN_R
BLK_ROWS = T // SCAN_R
SCAN_G = 4
SUPER = SCAN_G * T
NS_CTX = N_CTX // SUPER
NS_LAT_SEQ = DEC_SEQ // SUPER
NS = N_TOK // SUPER
SLAB_LANES = 256
N_SLABS = SSM_GROUPS * SSM_STATE // SLAB_LANES
N_WIN = BRANCH_W // LANES
GC = SSM_GROUPS * SSM_GROUP
ROPE_PERM = np.concatenate([np.arange(0, ROPE_DIM, 2), np.arange(1, ROPE_DIM, 2)])
ROPE_PARTNER = np.concatenate([np.arange(1, ROPE_DIM, 2), np.arange(0, ROPE_DIM, 2)])


def _cparams(sem):
    return pltpu.CompilerParams(dimension_semantics=sem, vmem_limit_bytes=VMEM_LIMIT_BYTES)


def _const_spec(shape):
    nd = len(shape)
    return pl.BlockSpec(shape, lambda *_: (0,) * nd, pipeline_mode=pl.Buffered(1))


def _layer_spec(l, shape):
    nd = len(shape)
    return pl.BlockSpec((None,) + tuple(shape), lambda *_: (l,) + (0,) * nd, pipeline_mode=pl.Buffered(1))


def _mod_spec(l):
    return pl.BlockSpec((None, None, 1, 6 * D_MODEL), lambda i: (l, _cond_row(i), 0, 0))


def _ctx_tile_spec(width):
    return pl.BlockSpec((T, width), lambda i: (jnp.minimum(i, NT_CTX - 1), 0))


def _lat_tile_spec(width):
    return pl.BlockSpec((T, width), lambda i: (jnp.maximum(i - NT_CTX, 0), 0))


def _load_pair(i, ctx_ref, lat_ref):
    return jnp.where(i < NT_CTX, ctx_ref[...], lat_ref[...])


def _stream_specs(x, width):
    if isinstance(x, tuple):
        return [_ctx_tile_spec(width), _lat_tile_spec(width)], list(x)
    return [pl.BlockSpec((T, width), lambda i: (i, 0))], [x]


def _load_stream(i, refs):
    return refs[0][...] if len(refs) == 1 else _load_pair(i, *refs)


def _store_stream(i, refs, val):
    if len(refs) == 1:
        refs[0][...] = val
        return
    ctx_ref, lat_ref = refs

    @pl.when(i < NT_CTX)
    def _():
        ctx_ref[...] = val

    lat_ref[...] = val


def _dot(a, b):
    return jnp.dot(a, b, preferred_element_type=F32)


def _dot_nt(a, b):
    return lax.dot_general(a, b, (((1,), (1,)), ((), ())), preferred_element_type=F32)


def _sigmoid(x):
    return 1.0 / (1.0 + jnp.exp(-x))


def _rms(x, g):
    return x * lax.rsqrt(jnp.mean(x * x, axis=-1, keepdims=True) + EPS) * g


def _cond_row(i):
    return jnp.maximum(i - (NT_CTX - NT_LAT_SEQ), 0) // NT_LAT_SEQ


def _mod_body(c_ref, w_ref, b_ref, o_ref):
    c = c_ref[...]
    s = c * _sigmoid(c)
    w = w_ref[...]
    s_hi = s.astype(BF16)
    s_lo = (s - s_hi.astype(F32)).astype(BF16)
    w_hi = w.astype(BF16)
    w_lo = (w - w_hi.astype(F32)).astype(BF16)
    o_ref[...] = _dot(s_hi, w_hi) + _dot(s_lo, w_hi) + _dot(s_hi, w_lo) + b_ref[...]


def _modulation(cond8, w_ada, b_ada):
    n_col = 6 * D_MODEL // 1024
    return pl.pallas_call(
        _mod_body,
        grid=(DEPTH, n_col),
        in_specs=[
            pl.BlockSpec((SUBLANES, D_MODEL), lambda l, j: (0, 0)),
            pl.BlockSpec((None, D_MODEL, 1024), lambda l, j: (l, 0, j)),
            pl.BlockSpec((None, 1, 1024), lambda l, j: (l, 0, j)),
        ],
        out_specs=pl.BlockSpec((None, SUBLANES, 1024), lambda l, j: (l, 0, j)),
        out_shape=jax.ShapeDtypeStruct((DEPTH, SUBLANES, 6 * D_MODEL), F32),
        compiler_params=_cparams(("parallel", "parallel")),
        name="modulation",
    )(cond8, w_ada, b_ada.reshape(DEPTH, 1, 6 * D_MODEL))


def _dot_nt3(a, b):
    a_hi = a.astype(BF16)
    a_lo = (a - a_hi.astype(F32)).astype(BF16)
    b_hi = b.astype(BF16)
    b_lo = (b - b_hi.astype(F32)).astype(BF16)
    return _dot_nt(a_hi, b_hi) + _dot_nt(a_lo, b_hi) + _dot_nt(a_hi, b_lo)


def _ssm_prep_body(lre_ref, lim_ref, ldt_ref, bre_ref, bim_ref, cre_ref, cim_ref, lre_r, lim_r, ldt_r,
                   ere_ref, eim_ref, bst_ref, ckt_ref, kst_ref, pwr_ref, pwi_ref):
    lam_re = lre_ref[...]
    lam_im = lim_ref[...]
    dt = jnp.exp(ldt_ref[...])
    xr = lam_re * dt
    xi = lam_im * dt

    mag = jnp.exp(xr)
    a_re = mag * jnp.cos(xi)
    a_im = mag * jnp.sin(xi)
    pw = [(jnp.ones_like(a_re), jnp.zeros_like(a_re)), (a_re, a_im)]
    for _ in range(SCAN_R - 1):
        pr, pi = pw[-1]
        pw.append((pr * a_re - pi * a_im, pr * a_im + pi * a_re))
    den = lam_re * lam_re + lam_im * lam_im
    f_re = ((a_re - 1.0) * lam_re + a_im * lam_im) / den
    f_im = (a_im * lam_re - (a_re - 1.0) * lam_im) / den
    bb_re = f_re * bre_ref[...] - f_im * bim_ref[...]
    bb_im = f_re * bim_ref[...] + f_im * bre_ref[...]
    c_re = cre_ref[...]
    c_im = cim_ref[...]
    ere = ere_ref[...]
    eim = eim_ref[...]

    r_g = lax.broadcasted_iota(jnp.int32, (LANES, 4 * SLAB_LANES), 0) // SSM_GROUP
    col = lax.broadcasted_iota(jnp.int32, (LANES, 4 * SLAB_LANES), 1)
    c_g = 4 * (col // (2 * SLAB_LANES)) + (col % SLAB_LANES) // SSM_STATE
    same_group = r_g == c_g

    def embed(v_re, v_im, w):
        rows = slice(w * LANES, (w + 1) * LANES)
        blk = _dot(v_re[rows, :].astype(BF16), ere) + _dot(v_im[rows, :].astype(BF16), eim)
        return jnp.where(same_group, blk, 0.0).astype(BF16)

    for i in range(SCAN_R):
        pr, pi = pw[SCAN_R - 1 - i]
        g_re = pr * bb_re - pi * bb_im
        g_im = pr * bb_im + pi * bb_re
        qr, qi = pw[i + 1]
        w_re = c_re * qr - c_im * qi
        w_im = -(c_re * qi + c_im * qr)
        for w in range(N_WIN):
            bst_ref[w, i * LANES:(i + 1) * LANES, :] = embed(g_re, g_im, w)
            ckt_ref[w, i * LANES:(i + 1) * LANES, :] = embed(w_re, w_im, w)

    kr = lax.broadcasted_iota(jnp.int32, (LANES, LANES), 0) // SSM_GROUP
    kc = lax.broadcasted_iota(jnp.int32, (LANES, LANES), 1) // SSM_GROUP
    k_same = kr == kc
    zero_blk = jnp.zeros((LANES, LANES), BF16)
    for tau in range(SCAN_R):
        pr, pi = pw[tau]
        w_re = c_re * pr - c_im * pi
        w_im = c_re * pi + c_im * pr
        for w in range(N_WIN):
            rows = slice(w * LANES, (w + 1) * LANES)
            kt = _dot_nt3(bb_re[rows, :], w_re[rows, :]) - _dot_nt3(bb_im[rows, :], w_im[rows, :])
            ktm = jnp.where(k_same, kt, 0.0).astype(BF16)
            for i_in in range(SCAN_R - tau):
                i_out = i_in + tau
                kst_ref[w, i_in * LANES:(i_in + 1) * LANES, i_out * LANES:(i_out + 1) * LANES] = ktm
            if tau > 0:
                for i_out in range(tau):
                    i_in = i_out + SCAN_R - tau
                    kst_ref[w, i_in * LANES:(i_in + 1) * LANES, i_out * LANES:(i_out + 1) * LANES] = zero_blk

    shape2 = (SCAN_M, SSM_GROUPS * SSM_STATE)
    n = ((lax.broadcasted_iota(jnp.int32, shape2, 0) + 1) * SCAN_R).astype(F32)
    dt_r = jnp.exp(ldt_r[...])
    e = jnp.exp(n * (lre_r[...] * dt_r))
    th = n * (lim_r[...] * dt_r)
    p_re = e * jnp.cos(th)
    p_im = e * jnp.sin(th)
    for s in range(N_SLABS):
        lanes = slice(s * SLAB_LANES, (s + 1) * SLAB_LANES)
        for m in range(SCAN_M):
            pwr_ref[s, m] = jnp.broadcast_to(p_re[m:m + 1, lanes], (SUBLANES, SLAB_LANES))
            pwi_ref[s, m] = jnp.broadcast_to(p_im[m:m + 1, lanes], (SUBLANES, SLAB_LANES))


def _embed_consts():
    e = np.zeros((2, SSM_STATE, 4 * SLAB_LANES), np.float32)
    for ri in range(2):
        for sg in range(2):
            for g4 in range(4):
                base = sg * 2 * SLAB_LANES + ri * SLAB_LANES + g4 * SSM_STATE
                e[ri, np.arange(SSM_STATE), base + np.arange(SSM_STATE)] = 1.0
    return jnp.asarray(e[0], BF16), jnp.asarray(e[1], BF16)


def _ssm_prep(lam_re, lam_im, log_dt, b_re, b_im, c_re, c_im):
    ld = DEPTH * 2
    gp = SSM_GROUPS * SSM_STATE
    rep = lambda a: jnp.broadcast_to(a.reshape(ld, SSM_GROUPS, 1, -1),
                                     (ld, SSM_GROUPS, SSM_GROUP, SSM_STATE)).reshape(ld, GC, SSM_STATE)
    ldt = log_dt.reshape(ld, SSM_GROUPS, 1)
    row = lambda a: jnp.broadcast_to(a.reshape(ld, SSM_GROUPS, -1), (ld, SSM_GROUPS, SSM_STATE)).reshape(ld, 1, gp)
    bt = lambda b: b.reshape(ld, SSM_GROUPS, SSM_STATE, SSM_GROUP).transpose(0, 1, 3, 2).reshape(ld, GC, SSM_STATE)
    ct = lambda c: c.reshape(ld, GC, SSM_STATE)
    ere, eim = _embed_consts()
    gspec = pl.BlockSpec((None, GC, SSM_STATE), lambda i: (i, 0, 0))
    rspec = pl.BlockSpec((None, 1, gp), lambda i: (i, 0, 0))
    espec = _const_spec((SSM_STATE, 4 * SLAB_LANES))
    tspec = lambda r, c: pl.BlockSpec((None, N_WIN, r, c), lambda i: (i, 0, 0, 0))
    pspec = pl.BlockSpec((None, N_SLABS, SCAN_M, SUBLANES, SLAB_LANES), lambda i: (i, 0, 0, 0, 0))
    outs = pl.pallas_call(
        _ssm_prep_body,
        grid=(ld,),
        in_specs=[gspec] * 7 + [rspec] * 3 + [espec, espec],
        out_specs=[tspec(SCAN_R * LANES, 4 * SLAB_LANES), tspec(SCAN_R * LANES, 4 * SLAB_LANES),
                   tspec(SCAN_R * LANES, SCAN_R * LANES), pspec, pspec],
        out_shape=[jax.ShapeDtypeStruct((ld, N_WIN, SCAN_R * LANES, 4 * SLAB_LANES), BF16),
                   jax.ShapeDtypeStruct((ld, N_WIN, SCAN_R * LANES, 4 * SLAB_LANES), BF16),
                   jax.ShapeDtypeStruct((ld, N_WIN, SCAN_R * LANES, SCAN_R * LANES), BF16),
                   jax.ShapeDtypeStruct((ld, N_SLABS, SCAN_M, SUBLANES, SLAB_LANES), F32),
                   jax.ShapeDtypeStruct((ld, N_SLABS, SCAN_M, SUBLANES, SLAB_LANES), F32)],
        compiler_params=_cparams(("arbitrary",)),
        name="ssm_prep",
    )(rep(lam_re), rep(lam_im), rep(ldt), bt(b_re), bt(b_im), ct(c_re), ct(c_im),
      row(lam_re), row(lam_im), row(ldt), ere, eim)
    return [o.reshape((DEPTH, 2) + o.shape[1:]) for o in outs]


def _scan_perms():
    rho = np.arange(T)
    i, m, q = rho // BLK_ROWS, (rho % BLK_ROWS) // SUBLANES, rho % SUBLANES
    t_of_rho = q * (T // SUBLANES) + SCAN_R * m + i
    pm = np.zeros((2, T, T), np.float32)
    pm[0, rho, t_of_rho] = 1.0
    pm[1, rho, T - 1 - t_of_rho] = 1.0
    pmt = np.transpose(pm, (0, 2, 1))
    pmt2 = np.concatenate([pmt, pmt], axis=2)
    return jnp.asarray(pm, BF16), jnp.asarray(pmt2, BF16)


def _ssm_body(u_ref, pm_ref, pmt2_ref, bst_ref, ckt_ref, kst_ref, pwr_ref, pwi_ref, h0_ref,
              y_ref, fin_ref, up_s, uc_s, x_s, hp_s, st_s):
    d = pl.program_id(0)
    j = pl.program_id(1)
    is_ctx = j < NS_CTX
    first = jnp.logical_or(is_ctx, (j - NS_CTX) % NS_LAT_SEQ == 0)

    @pl.when(first)
    def _():
        st_s[...] = jnp.where(is_ctx, 0.0, jnp.broadcast_to(h0_ref[...], st_s.shape))

    def tile_rows(g):
        gg = jnp.where(d == 1, SCAN_G - 1 - g, g)
        return pl.ds(pl.multiple_of(gg * T, T), T), gg

    pm = pm_ref[...]
    for g in range(SCAN_G):
        up_s[g] = _dot(pm, u_ref[tile_rows(g)[0], :].astype(BF16)).astype(BF16)

    for w in range(N_WIN):
        lanes = slice(w * LANES, (w + 1) * LANES)
        lhs = jnp.concatenate(
            [jnp.concatenate([up_s[g, i * BLK_ROWS:(i + 1) * BLK_ROWS, lanes] for g in range(SCAN_G)], axis=0)
             for i in range(SCAN_R)], axis=1)
        uc_s[w] = lhs
        x_s[w] = _dot(lhs, bst_ref[w])

    row = lax.broadcasted_iota(jnp.int32, (SUBLANES, SLAB_LANES), 0)
    for s in range(N_SLABS):
        w, sg = s // 2, s % 2
        l_re = slice(sg * 2 * SLAB_LANES, sg * 2 * SLAB_LANES + SLAB_LANES)
        l_im = slice(sg * 2 * SLAB_LANES + SLAB_LANES, (sg + 1) * 2 * SLAB_LANES)
        ar = pwr_ref[s, 0]
        ai = pwi_ref[s, 0]
        m_r = pwr_ref[s, SCAN_M - 1]
        m_i = pwi_ref[s, SCAN_M - 1]
        for g in range(SCAN_G):
            base = g * BLK_ROWS
            hr = jnp.zeros((SUBLANES, SLAB_LANES), F32)
            hi = jnp.zeros((SUBLANES, SLAB_LANES), F32)
            for m in range(SCAN_M):
                rows = slice(base + m * SUBLANES, base + (m + 1) * SUBLANES)
                hr, hi = (ar * hr - ai * hi + x_s[w, rows, l_re], ar * hi + ai * hr + x_s[w, rows, l_im])
                if m < SCAN_M - 1:
                    nxt = slice(base + (m + 1) * SUBLANES, base + (m + 2) * SUBLANES)
                    hp_s[w, nxt, l_re] = hr
                    hp_s[w, nxt, l_im] = hi
            pr_ = st_s[s, 0]
            pi_ = st_s[s, 1]
            if g > 0:
                pr_ = jnp.where(is_ctx, 0.0, pr_)
                pi_ = jnp.where(is_ctx, 0.0, pi_)
            cr = jnp.where(row == 0, pltpu.roll(pr_, 1, 0), pltpu.roll(hr, 1, 0))
            ci = jnp.where(row == 0, pltpu.roll(pi_, 1, 0), pltpu.roll(hi, 1, 0))
            qr, qi = m_r, m_i
            for k in (1, 2, 4):
                sr = pltpu.roll(cr, k, 0)
                si = pltpu.roll(ci, k, 0)
                keep = row >= k
                cr, ci = (cr + jnp.where(keep, qr * sr - qi * si, 0.0),
                          ci + jnp.where(keep, qr * si + qi * sr, 0.0))
                qr, qi = qr * qr - qi * qi, 2.0 * qr * qi
            fr = m_r * cr - m_i * ci + hr
            fi = m_r * ci + m_i * cr + hi
            st_s[s, 0] = fr
            st_s[s, 1] = fi
            slot = tile_rows(g)[1]
            fin_ref[slot, s, 0] = fr[SUBLANES - 1:SUBLANES, :]
            fin_ref[slot, s, 1] = fi[SUBLANES - 1:SUBLANES, :]
            hp_s[w, base:base + SUBLANES, l_re] = cr
            hp_s[w, base:base + SUBLANES, l_im] = ci
            for m in range(SCAN_M - 1):
                rows = slice(base + (m + 1) * SUBLANES, base + (m + 2) * SUBLANES)
                pr = pwr_ref[s, m]
                pi = pwi_ref[s, m]
                hp_s[w, rows, l_re] = hp_s[w, rows, l_re] + pr * cr - pi * ci
                hp_s[w, rows, l_im] = hp_s[w, rows, l_im] + pr * ci + pi * cr

    yw = [_dot_nt(hp_s[w].astype(BF16), ckt_ref[w]) + _dot(uc_s[w], kst_ref[w]) for w in range(N_WIN)]
    pmt2 = pmt2_ref[...]
    for g in range(SCAN_G):
        rows = slice(g * BLK_ROWS, (g + 1) * BLK_ROWS)
        y = jnp.concatenate(
            [jnp.concatenate([yw[w][rows, i * LANES:(i + 1) * LANES] for w in range(N_WIN)], axis=1)
             for i in range(SCAN_R)], axis=0)
        y_hi = y.astype(BF16)
        y_lo = (y - y_hi.astype(F32)).astype(BF16)
        y_ref[tile_rows(g)[0], :] = _dot(pmt2, jnp.concatenate([y_hi, y_lo], axis=0))


def _ssm_super(d, j):
    jl = j - NS_CTX
    half = jl % NS_LAT_SEQ
    half = jnp.where(d == 1, NS_LAT_SEQ - 1 - half, half)
    return jnp.where(j < NS_CTX, j, NS_CTX + (jl // NS_LAT_SEQ) * NS_LAT_SEQ + half)


def _ssm_scan(u_ssm, pm, pmt2, bst, ckt, kst, pwr, pwi, h0):
    tab = lambda *shape: pl.BlockSpec((None,) + shape, lambda d, j: (d,) + (0,) * len(shape))
    return pl.pallas_call(
        _ssm_body,
        grid=(2, NS),
        in_specs=[
            pl.BlockSpec((SUPER, BRANCH_W), lambda d, j: (_ssm_super(d, j), 0)),
            tab(T, T), tab(T, 2 * T),
            tab(N_WIN, SCAN_R * LANES, 4 * SLAB_LANES), tab(N_WIN, SCAN_R * LANES, 4 * SLAB_LANES),
            tab(N_WIN, SCAN_R * LANES, SCAN_R * LANES),
            tab(N_SLABS, SCAN_M, SUBLANES, SLAB_LANES), tab(N_SLABS, SCAN_M, SUBLANES, SLAB_LANES),
            pl.BlockSpec((None, None, N_SLABS, 2, 1, SLAB_LANES),
                         lambda d, j: (d, jnp.maximum(j - NS_CTX, 0) // NS_LAT_SEQ, 0, 0, 0, 0)),
        ],
        out_specs=[
            pl.BlockSpec((None, SUPER, BRANCH_W), lambda d, j: (d, _ssm_super(d, j), 0)),
            pl.BlockSpec((None, None, SCAN_G, N_SLABS, 2, 1, SLAB_LANES), lambda d, j: (d, j, 0, 0, 0, 0, 0)),
        ],
        out_shape=[jax.ShapeDtypeStruct((2, N_TOK, BRANCH_W), F32),
                   jax.ShapeDtypeStruct((2, NS, SCAN_G, N_SLABS, 2, 1, SLAB_LANES), F32)],
        scratch_shapes=[
            pltpu.VMEM((SCAN_G, T, BRANCH_W), BF16),
            pltpu.VMEM((N_WIN, SCAN_G * BLK_ROWS, SCAN_R * LANES), BF16),
            pltpu.VMEM((N_WIN, SCAN_G * BLK_ROWS, 4 * SLAB_LANES), F32),
            pltpu.VMEM((N_WIN, SCAN_G * BLK_ROWS, 4 * SLAB_LANES), F32),
            pltpu.VMEM((N_SLABS, 2, SUBLANES, SLAB_LANES), F32),
        ],
        compiler_params=_cparams(("arbitrary", "arbitrary")),
        name="ssm_scan",
    )(u_ssm, pm, pmt2, bst, ckt, kst, pwr, pwi, h0)


def _pre_body(n_x, *refs):
    (mod_ref, g_ref, win_ref, qag_ref, kvg_ref, wuq_ref, wuqs_ref, wuk_ref, wva_ref,
     wvb_ref, qng_ref, qngs_ref, kng_ref, kngs_ref, rc_ref, rsw_ref,
     ussm_ref, ufft_ref, q_ref, k_ref, va_ref, vb_ref, ckv_ref, kpe_ref, vc_ref, gb_ref) = refs[n_x:]
    i = pl.program_id(0)
    x = _load_stream(i, refs[:n_x])
    sh = mod_ref[:, 0:D_MODEL]
    sc = mod_ref[:, D_MODEL:2 * D_MODEL]
    xn = (_rms(x, g_ref[...]) * (1.0 + sc) + sh).astype(BF16)

    def proj(lo, hi):
        return _dot(xn, win_ref[:, lo:hi])

    ussm_ref[...] = proj(Z_SSM, Z_FFT)
    ufft_ref[...] = proj(Z_FFT, Z_CQ).astype(BF16)
    zk = proj(Z_KPE, Z_KROT)
    krot = proj(Z_KROT, Z_KSW)
    kswp = proj(Z_KSW, Z_HIN)
    h_in = proj(Z_HIN, Z_GB)
    gb_ref[...] = proj(Z_GB, Z_GC)
    vc_ref[...] = proj(Z_GC, Z_COLS) * h_in

    rc = rc_ref[...]
    rsw = rsw_ref[...]
    q_gc = qng_ref[...] * rc * ATTN_SCALE
    q_gs = qngs_ref[...] * rsw * ATTN_SCALE
    k_gc = kng_ref[...] * rc
    k_sw = kswp * (kngs_ref[...] * rsw)

    def rstd(v):
        return lax.rsqrt(jnp.sum(v * v, axis=-1, keepdims=True) * (1.0 / QK_DIM) + EPS)

    cqn = _rms(proj(Z_CQ, Z_CKV), qag_ref[...]).astype(BF16)
    qf = _dot(cqn, wuq_ref[...])
    qs = _dot(cqn, wuqs_ref[...])
    ckvn = _rms(proj(Z_CKV, Z_KPE), kvg_ref[...])

    @pl.when(i < NT_CTX)
    def _():
        ckv_ref[...] = ckvn
        kpe_ref[...] = zk[:, 0:ROPE_DIM]

    ckvb = ckvn.astype(BF16)
    kf = _dot(ckvb, wuk_ref[...])
    va_ref[...] = _dot(ckvb, wva_ref[...]).astype(BF16)
    vb_ref[...] = _dot(ckvb, wvb_ref[...]).astype(BF16)
    for h in range(MLA_HEADS):
        lanes = slice(h * HEAD_PAD, (h + 1) * HEAD_PAD)
        qh = qf[:, lanes]
        q_ref[:, lanes] = ((qh * q_gc + qs[:, lanes] * q_gs) * rstd(qh)).astype(BF16)
        kh = kf[:, lanes] + krot
        k_ref[:, lanes] = ((kh * k_gc + k_sw) * rstd(kh)).astype(BF16)


def _pre(l, x, mod, p, rc, rsw):
    tile = lambda width: pl.BlockSpec((T, width), lambda i: (i, 0))
    rope_blk = lambda i: (jnp.where(i < NT_CTX, NT_LAT_SEQ, i % NT_LAT_SEQ), 0)
    rspec = pl.BlockSpec((T, LANES), rope_blk)
    out = lambda width, dt: jax.ShapeDtypeStruct((N_TOK, width), dt)
    names = ('norm_mix_g', 'w_in_r', 'q_a_norm_g', 'kv_a_norm_g', 'wuq', 'wuqs', 'wuk', 'wva', 'wvb',
             'qng', 'qngs', 'kng', 'kngs')
    x_specs, x_args = _stream_specs(x, D_MODEL)
    return pl.pallas_call(
        functools.partial(_pre_body, len(x_args)),
        grid=(NT,),
        in_specs=x_specs + [_mod_spec(l)]
        + [_layer_spec(l, p[n].shape[1:]) for n in names] + [rspec, rspec],
        out_specs=[tile(BRANCH_W), tile(BRANCH_W), tile(MLA_HEADS * HEAD_PAD), tile(MLA_HEADS * HEAD_PAD),
                   tile(BRANCH_W), tile(BRANCH_W), _ctx_tile_spec(KV_RANK), _ctx_tile_spec(ROPE_DIM),
                   tile(BRANCH_W), tile(BRANCH_W)],
        out_shape=[out(BRANCH_W, F32), out(BRANCH_W, BF16), out(MLA_HEADS * HEAD_PAD, BF16),
                   out(MLA_HEADS * HEAD_PAD, BF16), out(BRANCH_W, BF16), out(BRANCH_W, BF16),
                   jax.ShapeDtypeStruct((N_CTX, KV_RANK), F32), jax.ShapeDtypeStruct((N_CTX, ROPE_DIM), F32),
                   out(BRANCH_W, F32), out(BRANCH_W, F32)],
        compiler_params=_cparams(("arbitrary",)),
        name="pre_mixer",
    )(*x_args, mod, *[p[n] for n in names], rc, rsw)


def _kvc_body(ckv_ref, kr_ref, wuk_ref, wva_ref, wvb_ref, kng_ref, k_ref, va_ref, vb_ref):
    ckvb = ckv_ref[...].astype(BF16)
    kf = _dot(ckvb, wuk_ref[...])
    va_ref[...] = _dot(ckvb, wva_ref[...]).astype(BF16)
    vb_ref[...] = _dot(ckvb, wvb_ref[...]).astype(BF16)
    krot = kr_ref[...]
    g = kng_ref[...]
    for h in range(MLA_HEADS):
        lanes = slice(h * HEAD_PAD, (h + 1) * HEAD_PAD)
        kh = kf[:, lanes] + krot
        ss = jnp.sum(kh * kh, axis=-1, keepdims=True)
        k_ref[:, lanes] = (kh * lax.rsqrt(ss * (1.0 / QK_DIM) + EPS) * g).astype(BF16)


def _cache_kv(cache_ckv, cache_kr, wuk, wva, wvb, kng):
    nt = PAST_LEN // T
    wspec = lambda r, c: pl.BlockSpec((None, r, c), lambda l, b, t: (l, 0, 0))
    ospec = lambda c: pl.BlockSpec((None, None, T, c), lambda l, b, t: (l, b, t, 0))
    return pl.pallas_call(
        _kvc_body,
        grid=(DEPTH, DEC_BATCH, nt),
        in_specs=[
            pl.BlockSpec((None, None, T, KV_RANK), lambda l, b, t: (b, l, t, 0)),
            pl.BlockSpec((None, None, T, HEAD_PAD), lambda l, b, t: (b, l, t, 0)),
            wspec(KV_RANK, MLA_HEADS * HEAD_PAD), wspec(KV_RANK, BRANCH_W), wspec(KV_RANK, BRANCH_W),
            wspec(1, HEAD_PAD),
        ],
        out_specs=[ospec(MLA_HEADS * HEAD_PAD), ospec(BRANCH_W), ospec(BRANCH_W)],
        out_shape=[jax.ShapeDtypeStruct((DEPTH, DEC_BATCH, PAST_LEN, MLA_HEADS * HEAD_PAD), BF16),
                   jax.ShapeDtypeStruct((DEPTH, DEC_BATCH, PAST_LEN, BRANCH_W), BF16),
                   jax.ShapeDtypeStruct((DEPTH, DEC_BATCH, PAST_LEN, BRANCH_W), BF16)],
        compiler_params=_cparams(("parallel", "parallel", "parallel")),
        name="cache_kv",
    )(cache_ckv, cache_kr, wuk, wva, wvb, kng)


def _fnet_body(seq_len, u_ref, csl_ref, csc_ref, y_ref, v1_s, v2_s):
    r = pl.program_id(1)

    @pl.when(r == 0)
    def _():
        csc = csc_ref[...]
        for c in range(seq_len // T):
            rows = slice(c * T, (c + 1) * T)
            for g in range(FFT_GROUPS):
                lanes = slice(g * FFT_GW, (g + 1) * FFT_GW)
                v = _dot(u_ref[rows, lanes], csc)
                v1_s[rows, lanes] = v[:, :FFT_GW].astype(BF16)
                v2_s[rows, lanes] = v[:, FFT_GW:].astype(BF16)

    y = _dot(csl_ref[:, :seq_len], v1_s[...]) + _dot(csl_ref[:, seq_len:], v2_s[...])
    y_ref[...] = y.astype(BF16)


def _fnet(u_fft, csl, csc, seq_len, n_seq, tile0):
    nr = seq_len // T
    blk0 = tile0 * T // seq_len
    return pl.pallas_call(
        functools.partial(_fnet_body, seq_len),
        grid=(n_seq, nr),
        in_specs=[
            pl.BlockSpec((seq_len, BRANCH_W), lambda b, r: (blk0 + b, 0)),
            pl.BlockSpec((T, 2 * seq_len), lambda b, r: (r, 0)),
            _const_spec((FFT_GW, 2 * FFT_GW)),
        ],
        out_specs=pl.BlockSpec((T, BRANCH_W), lambda b, r: (b * nr + r, 0)),
        out_shape=jax.ShapeDtypeStruct((n_seq * seq_len, BRANCH_W), BF16),
        scratch_shapes=[pltpu.VMEM((seq_len, BRANCH_W), BF16), pltpu.VMEM((seq_len, BRANCH_W), BF16)],
        compiler_params=_cparams(("arbitrary", "arbitrary")),
        name="fnet_%d" % seq_len,
    )(u_fft, csl, csc)


def _dft_tables(n, split):
    m = jnp.arange(n, dtype=jnp.int32)
    a = jnp.arange(n // split, dtype=jnp.int32) * split
    b = jnp.arange(split, dtype=jnp.int32)
    w = 2.0 * math.pi / n
    ang_a = ((a[:, None] * m[None, :]) % n).astype(F32) * w
    ang_b = ((b[:, None] * m[None, :]) % n).astype(F32) * w
    ca, sa = jnp.cos(ang_a)[:, None, :], jnp.sin(ang_a)[:, None, :]
    cb, sb = jnp.cos(ang_b)[None, :, :], jnp.sin(ang_b)[None, :, :]
    s = 1.0 / math.sqrt(n)
    cos = ((ca * cb - sa * sb) * s).reshape(n, n)
    sin = ((sa * cb + ca * sb) * s).reshape(n, n)
    return cos, sin


def _attn_body(n_seg, q_ref, *refs):
    k_refs = refs[0:n_seg]
    va_refs = refs[n_seg:2 * n_seg]
    vb_refs = refs[2 * n_seg:3 * n_seg]
    o_ref = refs[-1]
    for hp in range(MLA_HEADS // 2):
        out_lanes = slice(hp * LANES, (hp + 1) * LANES)
        acc = jnp.zeros((T, LANES), F32)
        for h, v_refs in ((2 * hp, va_refs), (2 * hp + 1, vb_refs)):
            lanes = slice(h * HEAD_PAD, (h + 1) * HEAD_PAD)
            qh = q_ref[:, lanes]
            ss = [_dot_nt(qh, kr[:, lanes]) for kr in k_refs]
            m = functools.reduce(jnp.maximum, [jnp.max(s, axis=-1, keepdims=True) for s in ss])
            ps = [jnp.exp(s - m) for s in ss]
            l = functools.reduce(lambda a, b: a + b, [jnp.sum(p, axis=-1, keepdims=True) for p in ps])
            o = functools.reduce(lambda a, b: a + b,
                                 [_dot(p.astype(BF16), vr[:, out_lanes]) for p, vr in zip(ps, v_refs)])
            acc = acc + o / l
        o_ref[:, out_lanes] = acc.astype(BF16)


def _attention(q, segs, n_seq, nq, tile0):
    in_specs = [pl.BlockSpec((T, MLA_HEADS * HEAD_PAD), lambda b, r: (tile0 + b * nq + r, 0))]
    args = [q]
    for idx in range(3):
        for seg in segs:
            arr = seg[idx]
            width = arr.shape[-1]
            in_specs.append(pl.BlockSpec(seg[3] + (width,), seg[4]))
            args.append(arr)
    return pl.pallas_call(
        functools.partial(_attn_body, len(segs)),
        grid=(n_seq, nq),
        in_specs=in_specs,
        out_specs=pl.BlockSpec((T, BRANCH_W), lambda b, r: (b * nq + r, 0)),
        out_shape=jax.ShapeDtypeStruct((n_seq * nq * T, BRANCH_W), BF16),
        compiler_params=_cparams(("arbitrary", "arbitrary")),
        name="attention_%d" % nq,
    )(*args)


def _mix_body(n_x, *refs):
    (mod_ref, g_ref, yf_ref, yb_ref, u_ref, dsk_ref, wglu_ref, fc_ref, fl_ref,
     ac_ref, al_ref, vc_ref, vp_ref, vn_ref, cw_ref, gb_ref, wbr_ref, wg_ref, bg_ref, wo_ref, o_ref) = refs[n_x:]
    i = pl.program_id(0)
    x = _load_stream(i, refs[:n_x])
    sh = mod_ref[:, 0:D_MODEL]
    sc = mod_ref[:, D_MODEL:2 * D_MODEL]
    gm = mod_ref[:, 2 * D_MODEL:3 * D_MODEL]
    xn = (_rms(x, g_ref[...]) * (1.0 + sc) + sh).astype(BF16)

    y = jax.nn.gelu(yf_ref[...] + yb_ref[...] + u_ref[...] * dsk_ref[...], approximate=True)
    y = y * _sigmoid(_dot(y.astype(BF16), wglu_ref[...]))

    v = vc_ref[...]
    row = lax.broadcasted_iota(jnp.int32, (T, BRANCH_W), 0)
    is_lat = i >= NT_CTX
    has_prev = jnp.logical_and(is_lat, i % NT_LAT_SEQ != 0)
    has_next = jnp.logical_and(is_lat, i % NT_LAT_SEQ != NT_LAT_SEQ - 1)
    halo_p = jnp.where(has_prev, vp_ref[SUBLANES - 1:SUBLANES, :], 0.0)
    halo_n = jnp.where(has_next, vn_ref[0:1, :], 0.0)
    v_prev = jnp.where(row == 0, halo_p, pltpu.roll(v, 1, 0))
    v_next = jnp.where(row == T - 1, halo_n, pltpu.roll(v, T - 1, 0))
    y_conv = gb_ref[...] * (cw_ref[0:1, :] * v_prev + cw_ref[1:2, :] * v + cw_ref[2:3, :] * v_next)

    branches = (y.astype(BF16), _load_pair(i, fc_ref, fl_ref), _load_pair(i, ac_ref, al_ref),
                y_conv.astype(BF16))
    merged = jnp.zeros((T, D_MODEL), F32)
    for k in range(N_BRANCH):
        cols = slice(k * D_MODEL, (k + 1) * D_MODEL)
        gate = _sigmoid(_dot(xn, wg_ref[:, cols]) + bg_ref[:, cols])
        merged = merged + gate * _dot(branches[k], wbr_ref[k])
    out = _dot(merged.astype(BF16), wo_ref[...])
    o_ref[...] = x + gm * out


def _mix(l, x, mod, p, y_ssm, u_ssm, fft_c, fft_l, att_c, att_l, vconv, g_b):
    tile = lambda width: pl.BlockSpec((T, width), lambda i: (i, 0))
    rb = T // SUBLANES
    n_rb = N_TOK // SUBLANES
    lspec = lambda n: _layer_spec(l, p[n].shape[1:])
    x_specs, x_args = _stream_specs(x, D_MODEL)
    return pl.pallas_call(
        functools.partial(_mix_body, len(x_args)),
        grid=(NT,),
        in_specs=x_specs + [
            _mod_spec(l), lspec('norm_mix_g'),
            pl.BlockSpec((None, T, BRANCH_W), lambda i: (0, i, 0)),
            pl.BlockSpec((None, T, BRANCH_W), lambda i: (1, i, 0)),
            tile(BRANCH_W), lspec('ssm_d'), lspec('w_glu'),
            _ctx_tile_spec(BRANCH_W), _lat_tile_spec(BRANCH_W), _ctx_tile_spec(BRANCH_W), _lat_tile_spec(BRANCH_W),
            tile(BRANCH_W),
            pl.BlockSpec((SUBLANES, BRANCH_W), lambda i: (jnp.maximum(i * rb - 1, 0), 0)),
            pl.BlockSpec((SUBLANES, BRANCH_W), lambda i: (jnp.minimum(i * rb + rb, n_rb - 1), 0)),
            lspec('conv_w'), tile(BRANCH_W),
            lspec('w_branch'), lspec('w_gate'), lspec('b_gate'), lspec('w_out'),
        ],
        out_specs=tile(D_MODEL),
        out_shape=jax.ShapeDtypeStruct((N_TOK, D_MODEL), F32),
        compiler_params=_cparams(("arbitrary",)),
        name="mixer_merge",
    )(*x_args, mod, p['norm_mix_g'], y_ssm, y_ssm, u_ssm, p['ssm_d'], p['w_glu'], fft_c, fft_l, att_c, att_l,
      vconv, vconv, vconv, p['conv_w'], g_b, p['w_branch'], p['w_gate'], p['b_gate'], p['w_out'])


FF_CHUNK = D_FF // 2


def _ffn_body(n_out, x_ref, mod_ref, g_ref, wi_ref, wo_ref, *o_refs):
    i = pl.program_id(0)
    x = x_ref[...]
    sh = mod_ref[:, 3 * D_MODEL:4 * D_MODEL]
    sc = mod_ref[:, 4 * D_MODEL:5 * D_MODEL]
    gf = mod_ref[:, 5 * D_MODEL:6 * D_MODEL]
    xn = (_rms(x, g_ref[...]) * (1.0 + sc) + sh).astype(BF16)
    acc = jnp.zeros((T, D_MODEL), F32)
    for c in range(D_FF // FF_CHUNK):
        lo = c * FF_CHUNK
        g = _dot(xn, wi_ref[:, lo:lo + FF_CHUNK])
        u = _dot(xn, wi_ref[:, D_FF + lo:D_FF + lo + FF_CHUNK])
        h = (g * _sigmoid(g) * u).astype(BF16)
        acc = acc + _dot(h, wo_ref[lo:lo + FF_CHUNK, :])
    _store_stream(i, o_refs[:n_out], x + gf * acc)


def _ffn(l, x, mod, p, split_out):
    lspec = lambda n: _layer_spec(l, p[n].shape[1:])
    tile = pl.BlockSpec((T, D_MODEL), lambda i: (i, 0))
    if split_out:
        out_specs = [_ctx_tile_spec(D_MODEL), _lat_tile_spec(D_MODEL)]
        out_shape = [jax.ShapeDtypeStruct((N_CTX, D_MODEL), F32), jax.ShapeDtypeStruct((N_LAT, D_MODEL), F32)]
    else:
        out_specs = [tile]
        out_shape = [jax.ShapeDtypeStruct((N_TOK, D_MODEL), F32)]
    return pl.pallas_call(
        functools.partial(_ffn_body, len(out_specs)),
        grid=(NT,),
        in_specs=[tile, _mod_spec(l), lspec('norm_ffn_g'), lspec('w_ffn_in'), lspec('w_ffn_out')],
        out_specs=out_specs,
        out_shape=out_shape,
        compiler_params=_cparams(("arbitrary",)),
        name="ffn",
    )(x, mod, p['norm_ffn_g'], p['w_ffn_in'], p['w_ffn_out'])


def _prep_weights(norm_mix_g, norm_ffn_g, w_in, q_a_norm_g, kv_a_norm_g, w_uq, w_ukv, q_norm_g, k_norm_g,
                  ssm_d, w_glu, conv_w, w_branch, w_gate, b_gate, w_out, w_ffn_in, w_ffn_out):
    vec = lambda a: a.reshape(DEPTH, 1, a.shape[-1])
    kpe = w_in[:, :, OFF_KPE:OFF_KPE + ROPE_DIM]
    z = lambda n: jnp.zeros((DEPTH, D_MODEL, n), F32)
    w_in_r = jnp.concatenate([
        w_in[:, :, OFF_SSM:OFF_KPE],
        kpe, z(LANES - ROPE_DIM),
        z(NOPE_DIM), kpe[:, :, ROPE_PERM], z(LANES - QK_DIM),
        z(NOPE_DIM), kpe[:, :, ROPE_PARTNER], z(LANES - QK_DIM),
        w_in[:, :, OFF_CONV:OFF_CONV + 3 * BRANCH_W],
    ], axis=2).astype(BF16)
    wq = w_uq.reshape(DEPTH, Q_RANK, MLA_HEADS, QK_DIM)
    head_pad = lambda a: jnp.pad(a, ((0, 0), (0, 0), (0, 0), (0, HEAD_PAD - a.shape[-1]))).reshape(
        DEPTH, a.shape[1], MLA_HEADS * HEAD_PAD).astype(BF16)
    wuq = head_pad(jnp.concatenate([wq[..., :NOPE_DIM], wq[..., NOPE_DIM:][..., ROPE_PERM]], axis=3))
    wuqs = head_pad(jnp.concatenate([jnp.zeros_like(wq[..., :NOPE_DIM]), wq[..., NOPE_DIM:][..., ROPE_PARTNER]],
                                    axis=3))
    wkv = w_ukv.reshape(DEPTH, KV_RANK, MLA_HEADS, NOPE_DIM + V_DIM)
    wuk = head_pad(wkv[..., :NOPE_DIM])
    wv = wkv[..., NOPE_DIM:]
    even = (jnp.arange(MLA_HEADS) % 2 == 0)[None, None, :, None]
    wva = jnp.where(even, wv, 0.0).reshape(DEPTH, KV_RANK, BRANCH_W).astype(BF16)
    wvb = jnp.where(even, 0.0, wv).reshape(DEPTH, KV_RANK, BRANCH_W).astype(BF16)

    def gain(g, perm):
        gp = jnp.concatenate([g[:, :NOPE_DIM], g[:, NOPE_DIM:][:, perm],
                              jnp.zeros((DEPTH, HEAD_PAD - QK_DIM), F32)], axis=1)
        return gp.reshape(DEPTH, 1, HEAD_PAD)

    return dict(
        norm_mix_g=vec(norm_mix_g), norm_ffn_g=vec(norm_ffn_g), w_in_r=w_in_r,
        q_a_norm_g=vec(q_a_norm_g), kv_a_norm_g=vec(kv_a_norm_g),
        wuq=wuq, wuqs=wuqs, wuk=wuk, wva=wva, wvb=wvb,
        qng=gain(q_norm_g, ROPE_PERM), qngs=gain(q_norm_g, ROPE_PARTNER),
        kng=gain(k_norm_g, ROPE_PERM), kngs=gain(k_norm_g, ROPE_PARTNER),
        ssm_d=vec(ssm_d), w_glu=w_glu.astype(BF16),
        conv_w=jnp.pad(conv_w, ((0, 0), (0, SUBLANES - 3), (0, 0))),
        w_branch=w_branch.astype(BF16), w_gate=w_gate.astype(BF16), b_gate=vec(b_gate),
        w_out=w_out.astype(BF16), w_ffn_in=w_ffn_in.astype(BF16), w_ffn_out=w_ffn_out.astype(BF16),
    )


def _rope_tables():
    rows = DEC_SEQ // GRID_W
    r = jnp.repeat(jnp.arange(rows, dtype=F32), GRID_W)
    c = jnp.tile(jnp.arange(GRID_W, dtype=F32), rows)
    n_freq = ROPE_DIM // 4
    inv = ROPE_THETA ** (-jnp.arange(n_freq, dtype=F32) / n_freq)
    ang = jnp.concatenate([r[:, None] * inv, c[:, None] * inv], axis=-1)
    cos, sin = jnp.cos(ang), jnp.sin(ang)
    ones = jnp.ones((DEC_SEQ, NOPE_DIM), F32)
    zeros = jnp.zeros((DEC_SEQ, NOPE_DIM), F32)
    tail1 = jnp.ones((DEC_SEQ, HEAD_PAD - QK_DIM), F32)
    tail0 = jnp.zeros((DEC_SEQ, HEAD_PAD - QK_DIM), F32)
    rc = jnp.concatenate([ones, cos, cos, tail1], axis=1)
    rsw = jnp.concatenate([zeros, -sin, sin, tail0], axis=1)
    ident = lambda v: jnp.full((T, HEAD_PAD), v, F32)
    return jnp.concatenate([rc, ident(1.0)], axis=0), jnp.concatenate([rsw, ident(0.0)], axis=0)


def kernel(x_prompt, x_sample, cache_ckv, cache_kpe, state_ssm, c, c_ctx, norm_mix_g, norm_ffn_g, w_ada, b_ada,
           w_in, q_a_norm_g, kv_a_norm_g, w_uq, w_ukv, q_norm_g, k_norm_g, ssm_lam_re, ssm_lam_im, ssm_log_dt,
           ssm_b_re, ssm_b_im, ssm_c_re, ssm_c_im, ssm_d, w_glu, conv_w, w_branch, w_gate, b_gate, w_out,
           w_ffn_in, w_ffn_out):
    xp = x_prompt.reshape(N_CTX, D_MODEL)
    xs = x_sample.reshape(N_LAT, D_MODEL)

    cond8 = jnp.concatenate([c_ctx[None, :], c, jnp.zeros((SUBLANES - 1 - DEC_BATCH, D_MODEL), F32)], axis=0)
    mod = _modulation(cond8, w_ada, b_ada).reshape(DEPTH, SUBLANES, 1, 6 * D_MODEL)

    bst, ckt, kst, pwr, pwi = _ssm_prep(ssm_lam_re, ssm_lam_im, ssm_log_dt, ssm_b_re, ssm_b_im,
                                        ssm_c_re, ssm_c_im)
    pm, pmt2 = _scan_perms()
    st = state_ssm.transpose(1, 2, 0, 5, 3, 4).reshape(DEPTH, 2, DEC_BATCH, 2, N_SLABS, 1, SLAB_LANES)
    h0_all = st.transpose(0, 1, 2, 4, 3, 5, 6)

    rc, rsw = _rope_tables()
    cl, sl = _dft_tables(SEQ, 16)
    csl_ctx = jnp.concatenate([cl, -sl], axis=1).astype(BF16)
    cl, sl = _dft_tables(DEC_SEQ, 64)
    csl_lat = jnp.concatenate([cl, -sl], axis=1).astype(BF16)
    cc, sc = _dft_tables(FFT_GW, 8)
    csc = jnp.concatenate([cc, sc], axis=1).astype(BF16)

    p = _prep_weights(norm_mix_g, norm_ffn_g, w_in, q_a_norm_g, kv_a_norm_g, w_uq, w_ukv, q_norm_g, k_norm_g,
                      ssm_d, w_glu, conv_w, w_branch, w_gate, b_gate, w_out, w_ffn_in, w_ffn_out)
    cache_kr = jnp.pad(cache_kpe[..., ROPE_PERM], ((0, 0), (0, 0), (0, 0), (NOPE_DIM, HEAD_PAD - QK_DIM)))
    kc, vca, vcb = _cache_kv(cache_ckv, cache_kr, p['wuk'], p['wva'], p['wvb'], p['kng'])

    lat_blk = N_CTX // DEC_SEQ
    ckv_list, kpe_list, fin_list = [], [], []
    x = (xp, xs)
    for l in range(DEPTH):
        (u_ssm, u_fft, q, k, va, vb, ckv_n, kpe, vconv, g_b) = _pre(l, x, mod, p, rc, rsw)
        ckv_list.append(ckv_n.reshape(BATCH, SEQ, KV_RANK))
        kpe_list.append(kpe.reshape(BATCH, SEQ, ROPE_DIM))

        y_ssm, fin = _ssm_scan(u_ssm, pm, pmt2, bst[l], ckt[l], kst[l], pwr[l], pwi[l], h0_all[l])
        fin_list.append(fin[:, :NS_CTX].reshape(2, BATCH, N_SLABS, 2, SLAB_LANES))

        fft_c = _fnet(u_fft, csl_ctx, csc, SEQ, BATCH, 0)
        fft_l = _fnet(u_fft, csl_lat, csc, DEC_SEQ, DEC_BATCH, NT_CTX)

        ctx_seg = (k, va, vb, (T,), lambda b, r: (b, 0))
        att_c = _attention(q, [ctx_seg], BATCH, 1, 0)
        lat_seg = (k, va, vb, (DEC_SEQ,), lambda b, r: (lat_blk + b, 0))
        cache_seg = (kc[l], vca[l], vcb[l], (None, PAST_LEN), lambda b, r: (b, 0, 0))
        att_l = _attention(q, [lat_seg, cache_seg], DEC_BATCH, NT_LAT_SEQ, NT_CTX)

        x = _mix(l, x, mod, p, y_ssm, u_ssm, fft_c, fft_l, att_c, att_l, vconv, g_b)
        x = _ffn(l, x, mod, p, split_out=(l == DEPTH - 1))
        x = tuple(x) if l == DEPTH - 1 else x[0]

    y_p = x[0].reshape(BATCH, SEQ, D_MODEL)
    y_s = x[1].reshape(DEC_BATCH, DEC_SEQ, D_MODEL)
    new_ckv = jnp.stack(ckv_list, axis=1)
    new_kpe = jnp.stack(kpe_list, axis=1)
    fin = jnp.stack(fin_list, axis=0)
    fin = fin.reshape(DEPTH, 2, BATCH, N_SLABS, 2, 4, SSM_STATE).transpose(2, 0, 1, 3, 5, 6, 4)
    new_ssm = fin.reshape(BATCH, DEPTH, 2, SSM_GROUPS, SSM_STATE, 2)
    return (y_p, y_s, new_ckv, new_kpe, new_ssm)
```

```python
import functools
import math

import numpy as np
import jax
import jax.numpy as jnp
from jax import lax
from jax.experimental import pallas as pl
from jax.experimental.pallas import tpu as pltpu

F32 = jnp.float32
BF16 = jnp.bfloat16

D_MODEL = 1024
BATCH = 32
SEQ = 256
DEPTH = 4
DEC_BATCH = 2
DEC_SEQ = 2048
PAST_LEN = 512
GRID_W = 64
N_BRANCH = 4
BRANCH_W = 512
SSM_GROUP = 16
SSM_GROUPS = 32
SSM_STATE = 64
FFT_GROUPS = 4
FFT_GW = 128
MLA_HEADS = 8
NOPE_DIM = 64
ROPE_DIM = 32
V_DIM = 64
QK_DIM = 96
Q_RANK = 384
KV_RANK = 256
ROPE_THETA = 10000.0
D_FF = 2816
EPS = 1e-6

OFF_SSM = 0
OFF_FFT = 512
OFF_CQ = 1024
OFF_CKV = 1408
OFF_KPE = 1664
OFF_CONV = 1696

LANES = 128
SUBLANES = 8
VMEM_LIMIT_BYTES = 56 * 1024 * 1024

T = 256
N_CTX = BATCH * SEQ
N_LAT = DEC_BATCH * DEC_SEQ
N_TOK = N_CTX + N_LAT
NT_CTX = N_CTX // T
NT_LAT_SEQ = DEC_SEQ // T
NT = N_TOK // T
TT = 512
NTT_CTX = N_CTX // TT
NTT_LAT_SEQ = DEC_SEQ // TT
NTT = N_TOK // TT
N_SEQ = BATCH + DEC_BATCH
HEAD_PAD = LANES
ATTN_SCALE = QK_DIM ** -0.5

Z_SSM = 0
Z_FFT = 512
Z_CQ = 1024
Z_CKV = 1408
Z_KPE = 1664
Z_KROT = 1792
Z_KSW = 1920
Z_HIN = 2048
Z_GB = 2560
Z_GC = 3072
Z_COLS = 3584

SCAN_R = 4
SCAN_M = T // SUBLANES // SCAN_R
BLK_ROWS = T // SCAN_R
SCAN_G = 4
SUPER = SCAN_G * T
NS_CTX = N_CTX // SUPER
NS_LAT_SEQ = DEC_SEQ // SUPER
NS = N_TOK // SUPER
SLAB_LANES = 256
N_SLABS = SSM_GROUPS * SSM_STATE // SLAB_LANES
N_WIN = BRANCH_W // LANES
GC = SSM_GROUPS * SSM_GROUP
ROPE_PERM = np.concatenate([np.arange(0, ROPE_DIM, 2), np.arange(1, ROPE_DIM, 2)])
ROPE_PARTNER = np.concatenate([np.arange(1, ROPE_DIM, 2), np.arange(0, ROPE_DIM, 2)])


def _cparams(sem):
    return pltpu.CompilerParams(dimension_semantics=sem, vmem_limit_bytes=VMEM_LIMIT_BYTES)


def _const_spec(shape):
    nd = len(shape)
    return pl.BlockSpec(shape, lambda *_: (0,) * nd, pipeline_mode=pl.Buffered(1))


def _layer_spec(l, shape):
    nd = len(shape)
    return pl.BlockSpec((None,) + tuple(shape), lambda *_: (l,) + (0,) * nd, pipeline_mode=pl.Buffered(1))


def _mod_spec(l):
    return pl.BlockSpec((None, None, 1, 6 * D_MODEL), lambda i: (l, _cond_row(i), 0, 0))


def _ctx_tile_spec(width):
    return pl.BlockSpec((TT, width), lambda i: (jnp.minimum(i, NTT_CTX - 1), 0))


def _lat_tile_spec(width):
    return pl.BlockSpec((TT, width), lambda i: (jnp.maximum(i - NTT_CTX, 0), 0))


def _tok_tile_spec(width):
    return pl.BlockSpec((TT, width), lambda i: (i, 0))


def _load_pair(i, ctx_ref, lat_ref):
    return jnp.where(i < NTT_CTX, ctx_ref[...], lat_ref[...])


def _stream_specs(x, width):
    if isinstance(x, tuple):
        return [_ctx_tile_spec(width), _lat_tile_spec(width)], list(x)
    return [_tok_tile_spec(width)], [x]


def _load_stream(i, refs):
    return refs[0][...] if len(refs) == 1 else _load_pair(i, *refs)


def _store_stream(i, refs, val):
    if len(refs) == 1:
        refs[0][...] = val
        return
    ctx_ref, lat_ref = refs

    @pl.when(i < NTT_CTX)
    def _():
        ctx_ref[...] = val

    lat_ref[...] = val


def _dot(a, b):
    return jnp.dot(a, b, preferred_element_type=F32)


def _dot_nt(a, b):
    return lax.dot_general(a, b, (((1,), (1,)), ((), ())), preferred_element_type=F32)


def _sigmoid(x):
    return 1.0 / (1.0 + jnp.exp(-x))


def _rms(x, g):
    return x * lax.rsqrt(jnp.mean(x * x, axis=-1, keepdims=True) + EPS) * g


def _cond_row(i):
    return jnp.maximum(i - (NTT_CTX - NTT_LAT_SEQ), 0) // NTT_LAT_SEQ


def _mod_body(c_ref, w_ref, b_ref, o_ref):
    c = c_ref[...]
    s = c * _sigmoid(c)
    w = w_ref[...]
    s_hi = s.astype(BF16)
    s_lo = (s - s_hi.astype(F32)).astype(BF16)
    w_hi = w.astype(BF16)
    w_lo = (w - w_hi.astype(F32)).astype(BF16)
    o_ref[...] = _dot(s_hi, w_hi) + _dot(s_lo, w_hi) + _dot(s_hi, w_lo) + b_ref[...]


def _modulation(cond8, w_ada, b_ada):
    n_col = 6 * D_MODEL // 1024
    return pl.pallas_call(
        _mod_body,
        grid=(DEPTH, n_col),
        in_specs=[
            pl.BlockSpec((SUBLANES, D_MODEL), lambda l, j: (0, 0)),
            pl.BlockSpec((None, D_MODEL, 1024), lambda l, j: (l, 0, j)),
            pl.BlockSpec((None, 1, 1024), lambda l, j: (l, 0, j)),
        ],
        out_specs=pl.BlockSpec((None, SUBLANES, 1024), lambda l, j: (l, 0, j)),
        out_shape=jax.ShapeDtypeStruct((DEPTH, SUBLANES, 6 * D_MODEL), F32),
        compiler_params=_cparams(("parallel", "parallel")),
        name="modulation",
    )(cond8, w_ada, b_ada.reshape(DEPTH, 1, 6 * D_MODEL))


def _dot_nt3(a, b):
    a_hi = a.astype(BF16)
    a_lo = (a - a_hi.astype(F32)).astype(BF16)
    b_hi = b.astype(BF16)
    b_lo = (b - b_hi.astype(F32)).astype(BF16)
    return _dot_nt(a_hi, b_hi) + _dot_nt(a_lo, b_hi) + _dot_nt(a_hi, b_lo)


def _ssm_prep_body(lre_ref, lim_ref, ldt_ref, bre_ref, bim_ref, cre_ref, cim_ref, lre_r, lim_r, ldt_r,
                   ere_ref, eim_ref, bst_ref, ckt_ref, kst_ref, pwr_ref, pwi_ref):
    lam_re = lre_ref[...]
    lam_im = lim_ref[...]
    dt = jnp.exp(ldt_ref[...])
    xr = lam_re * dt
    xi = lam_im * dt

    mag = jnp.exp(xr)
    a_re = mag * jnp.cos(xi)
    a_im = mag * jnp.sin(xi)
    pw = [(jnp.ones_like(a_re), jnp.zeros_like(a_re)), (a_re, a_im)]
    for _ in range(SCAN_R - 1):
        pr, pi = pw[-1]
        pw.append((pr * a_re - pi * a_im, pr * a_im + pi * a_re))
    den = lam_re * lam_re + lam_im * lam_im
    f_re = ((a_re - 1.0) * lam_re + a_im * lam_im) / den
    f_im = (a_im * lam_re - (a_re - 1.0) * lam_im) / den
    bb_re = f_re * bre_ref[...] - f_im * bim_ref[...]
    bb_im = f_re * bim_ref[...] + f_im * bre_ref[...]
    c_re = cre_ref[...]
    c_im = cim_ref[...]
    ere = ere_ref[...]
    eim = eim_ref[...]

    r_g = lax.broadcasted_iota(jnp.int32, (LANES, 4 * SLAB_LANES), 0) // SSM_GROUP
    col = lax.broadcasted_iota(jnp.int32, (LANES, 4 * SLAB_LANES), 1)
    c_g = 4 * (col // (2 * SLAB_LANES)) + (col % SLAB_LANES) // SSM_STATE
    same_group = r_g == c_g

    def embed(v_re, v_im, w):
        rows = slice(w * LANES, (w + 1) * LANES)
        blk = _dot(v_re[rows, :].astype(BF16), ere) + _dot(v_im[rows, :].astype(BF16), eim)
        return jnp.where(same_group, blk, 0.0).astype(BF16)

    for i in range(SCAN_R):
        pr, pi = pw[SCAN_R - 1 - i]
        g_re = pr * bb_re - pi * bb_im
        g_im = pr * bb_im + pi * bb_re
        qr, qi = pw[i + 1]
        w_re = c_re * qr - c_im * qi
        w_im = -(c_re * qi + c_im * qr)
        for w in range(N_WIN):
            bst_ref[w, i * LANES:(i + 1) * LANES, :] = embed(g_re, g_im, w)
            ckt_ref[w, i * LANES:(i + 1) * LANES, :] = embed(w_re, w_im, w)

    kr = lax.broadcasted_iota(jnp.int32, (LANES, LANES), 0) // SSM_GROUP
    kc = lax.broadcasted_iota(jnp.int32, (LANES, LANES), 1) // SSM_GROUP
    k_same = kr == kc
    zero_blk = jnp.zeros((LANES, LANES), BF16)
    for tau in range(SCAN_R):
        pr, pi = pw[tau]
        w_re = c_re * pr - c_im * pi
        w_im = c_re * pi + c_im * pr
        for w in range(N_WIN):
            rows = slice(w * LANES, (w + 1) * LANES)
            kt = _dot_nt3(bb_re[rows, :], w_re[rows, :]) - _dot_nt3(bb_im[rows, :], w_im[rows, :])
            ktm = jnp.where(k_same, kt, 0.0).astype(BF16)
            for i_in in range(SCAN_R - tau):
                i_out = i_in + tau
                kst_ref[w, i_in * LANES:(i_in + 1) * LANES, i_out * LANES:(i_out + 1) * LANES] = ktm
            if tau > 0:
                for i_out in range(tau):
                    i_in = i_out + SCAN_R - tau
                    kst_ref[w, i_in * LANES:(i_in + 1) * LANES, i_out * LANES:(i_out + 1) * LANES] = zero_blk

    shape2 = (SCAN_M, SSM_GROUPS * SSM_STATE)
    n = ((lax.broadcasted_iota(jnp.int32, shape2, 0) + 1) * SCAN_R).astype(F32)
    dt_r = jnp.exp(ldt_r[...])
    e = jnp.exp(n * (lre_r[...] * dt_r))
    th = n * (lim_r[...] * dt_r)
    p_re = e * jnp.cos(th)
    p_im = e * jnp.sin(th)
    for s in range(N_SLABS):
        lanes = slice(s * SLAB_LANES, (s + 1) * SLAB_LANES)
        for m in range(SCAN_M):
            pwr_ref[s, m] = jnp.broadcast_to(p_re[m:m + 1, lanes], (SUBLANES, SLAB_LANES))
            pwi_ref[s, m] = jnp.broadcast_to(p_im[m:m + 1, lanes], (SUBLANES, SLAB_LANES))


def _embed_consts():
    e = np.zeros((2, SSM_STATE, 4 * SLAB_LANES), np.float32)
    for ri in range(2):
        for sg in range(2):
            for g4 in range(4):
                base = sg * 2 * SLAB_LANES + ri * SLAB_LANES + g4 * SSM_STATE
                e[ri, np.arange(SSM_STATE), base + np.arange(SSM_STATE)] = 1.0
    return jnp.asarray(e[0], BF16), jnp.asarray(e[1], BF16)


def _ssm_prep(lam_re, lam_im, log_dt, b_re, b_im, c_re, c_im):
    ld = DEPTH * 2
    gp = SSM_GROUPS * SSM_STATE
    rep = lambda a: jnp.broadcast_to(a.reshape(ld, SSM_GROUPS, 1, -1),
                                     (ld, SSM_GROUPS, SSM_GROUP, SSM_STATE)).reshape(ld, GC, SSM_STATE)
    ldt = log_dt.reshape(ld, SSM_GROUPS, 1)
    row = lambda a: jnp.broadcast_to(a.reshape(ld, SSM_GROUPS, -1), (ld, SSM_GROUPS, SSM_STATE)).reshape(ld, 1, gp)
    bt = lambda b: b.reshape(ld, SSM_GROUPS, SSM_STATE, SSM_GROUP).transpose(0, 1, 3, 2).reshape(ld, GC, SSM_STATE)
    ct = lambda c: c.reshape(ld, GC, SSM_STATE)
    ere, eim = _embed_consts()
    gspec = pl.BlockSpec((None, GC, SSM_STATE), lambda i: (i, 0, 0))
    rspec = pl.BlockSpec((None, 1, gp), lambda i: (i, 0, 0))
    espec = _const_spec((SSM_STATE, 4 * SLAB_LANES))
    tspec = lambda r, c: pl.BlockSpec((None, N_WIN, r, c), lambda i: (i, 0, 0, 0))
    pspec = pl.BlockSpec((None, N_SLABS, SCAN_M, SUBLANES, SLAB_LANES), lambda i: (i, 0, 0, 0, 0))
    outs = pl.pallas_call(
        _ssm_prep_body,
        grid=(ld,),
        in_specs=[gspec] * 7 + [rspec] * 3 + [espec, espec],
        out_specs=[tspec(SCAN_R * LANES, 4 * SLAB_LANES), tspec(SCAN_R * LANES, 4 * SLAB_LANES),
                   tspec(SCAN_R * LANES, SCAN_R * LANES), pspec, pspec],
        out_shape=[jax.ShapeDtypeStruct((ld, N_WIN, SCAN_R * LANES, 4 * SLAB_LANES), BF16),
                   jax.ShapeDtypeStruct((ld, N_WIN, SCAN_R * LANES, 4 * SLAB_LANES), BF16),
                   jax.ShapeDtypeStruct((ld, N_WIN, SCAN_R * LANES, SCAN_R * LANES), BF16),
                   jax.ShapeDtypeStruct((ld, N_SLABS, SCAN_M, SUBLANES, SLAB_LANES), F32),
                   jax.ShapeDtypeStruct((ld, N_SLABS, SCAN_M, SUBLANES, SLAB_LANES), F32)],
        compiler_params=_cparams(("arbitrary",)),
        name="ssm_prep",
    )(rep(lam_re), rep(lam_im), rep(ldt), bt(b_re), bt(b_im), ct(c_re), ct(c_im),
      row(lam_re), row(lam_im), row(ldt), ere, eim)
    return [o.reshape((DEPTH, 2) + o.shape[1:]) for o in outs]


def _scan_perms():
    rho = np.arange(T)
    i, m, q = rho // BLK_ROWS, (rho % BLK_ROWS) // SUBLANES, rho % SUBLANES
    t_of_rho = q * (T // SUBLANES) + SCAN_R * m + i
    pm = np.zeros((2, T, T), np.float32)
    pm[0, rho, t_of_rho] = 1.0
    pm[1, rho, T - 1 - t_of_rho] = 1.0
    pmt = np.transpose(pm, (0, 2, 1))
    pmt2 = np.concatenate([pmt, pmt], axis=2)
    return jnp.asarray(pm, BF16), jnp.asarray(pmt2, BF16)


def _ssm_body(u_ref, pm_ref, pmt2_ref, bst_ref, ckt_ref, kst_ref, pwr_ref, pwi_ref, h0_ref,
              y_ref, fin_ref, up_s, uc_s, x_s, hp_s, st_s):
    d = pl.program_id(0)
    j = pl.program_id(1)
    is_ctx = j < NS_CTX
    first = jnp.logical_or(is_ctx, (j - NS_CTX) % NS_LAT_SEQ == 0)

    @pl.when(first)
    def _():
        st_s[...] = jnp.where(is_ctx, 0.0, jnp.broadcast_to(h0_ref[...], st_s.shape))

    def tile_rows(g):
        gg = jnp.where(d == 1, SCAN_G - 1 - g, g)
        return pl.ds(pl.multiple_of(gg * T, T), T), gg

    pm = pm_ref[...]
    for g in range(SCAN_G):
        up_s[g] = _dot(pm, u_ref[tile_rows(g)[0], :].astype(BF16)).astype(BF16)

    for w in range(N_WIN):
        lanes = slice(w * LANES, (w + 1) * LANES)
        lhs = jnp.concatenate(
            [jnp.concatenate([up_s[g, i * BLK_ROWS:(i + 1) * BLK_ROWS, lanes] for g in range(SCAN_G)], axis=0)
             for i in range(SCAN_R)], axis=1)
        uc_s[w] = lhs
        x_s[w] = _dot(lhs, bst_ref[w])

    row = lax.broadcasted_iota(jnp.int32, (SUBLANES, SLAB_LANES), 0)
    for s in range(N_SLABS):
        w, sg = s // 2, s % 2
        l_re = slice(sg * 2 * SLAB_LANES, sg * 2 * SLAB_LANES + SLAB_LANES)
        l_im = slice(sg * 2 * SLAB_LANES + SLAB_LANES, (sg + 1) * 2 * SLAB_LANES)
        ar = pwr_ref[s, 0]
        ai = pwi_ref[s, 0]
        m_r = pwr_ref[s, SCAN_M - 1]
        m_i = pwi_ref[s, SCAN_M - 1]
        for g in range(SCAN_G):
            base = g * BLK_ROWS
            hr = jnp.zeros((SUBLANES, SLAB_LANES), F32)
            hi = jnp.zeros((SUBLANES, SLAB_LANES), F32)
            for m in range(SCAN_M):
                rows = slice(base + m * SUBLANES, base + (m + 1) * SUBLANES)
                hr, hi = (ar * hr - ai * hi + x_s[w, rows, l_re], ar * hi + ai * hr + x_s[w, rows, l_im])
                if m < SCAN_M - 1:
                    nxt = slice(base + (m + 1) * SUBLANES, base + (m + 2) * SUBLANES)
                    hp_s[w, nxt, l_re] = hr
                    hp_s[w, nxt, l_im] = hi
            pr_ = st_s[s, 0]
            pi_ = st_s[s, 1]
            if g > 0:
                pr_ = jnp.where(is_ctx, 0.0, pr_)
                pi_ = jnp.where(is_ctx, 0.0, pi_)
            cr = jnp.where(row == 0, pltpu.roll(pr_, 1, 0), pltpu.roll(hr, 1, 0))
            ci = jnp.where(row == 0, pltpu.roll(pi_, 1, 0), pltpu.roll(hi, 1, 0))
            qr, qi = m_r, m_i
            for k in (1, 2, 4):
                sr = pltpu.roll(cr, k, 0)
                si = pltpu.roll(ci, k, 0)
                keep = row >= k
                cr, ci = (cr + jnp.where(keep, qr * sr - qi * si, 0.0),
                          ci + jnp.where(keep, qr * si + qi * sr, 0.0))
                qr, qi = qr * qr - qi * qi, 2.0 * qr * qi
            fr = m_r * cr - m_i * ci + hr
            fi = m_r * ci + m_i * cr + hi
            st_s[s, 0] = fr
            st_s[s, 1] = fi
            slot = tile_rows(g)[1]
            fin_ref[slot, s, 0] = fr[SUBLANES - 1:SUBLANES, :]
            fin_ref[slot, s, 1] = fi[SUBLANES - 1:SUBLANES, :]
            hp_s[w, base:base + SUBLANES, l_re] = cr
            hp_s[w, base:base + SUBLANES, l_im] = ci
            for m in range(SCAN_M - 1):
                rows = slice(base + (m + 1) * SUBLANES, base + (m + 2) * SUBLANES)
                pr = pwr_ref[s, m]
                pi = pwi_ref[s, m]
                hp_s[w, rows, l_re] = hp_s[w, rows, l_re] + pr * cr - pi * ci
                hp_s[w, rows, l_im] = hp_s[w, rows, l_im] + pr * ci + pi * cr

    yw = [_dot_nt(hp_s[w].astype(BF16), ckt_ref[w]) + _dot(uc_s[w], kst_ref[w]) for w in range(N_WIN)]
    pmt2 = pmt2_ref[...]
    for g in range(SCAN_G):
        rows = slice(g * BLK_ROWS, (g + 1) * BLK_ROWS)
        y = jnp.concatenate(
            [jnp.concatenate([yw[w][rows, i * LANES:(i + 1) * LANES] for w in range(N_WIN)], axis=1)
             for i in range(SCAN_R)], axis=0)
        y_hi = y.astype(BF16)
        y_lo = (y - y_hi.astype(F32)).astype(BF16)
        y_ref[tile_rows(g)[0], :] = _dot(pmt2, jnp.concatenate([y_hi, y_lo], axis=0))


def _ssm_super(d, j):
    jl = j - NS_CTX
    half = jl % NS_LAT_SEQ
    half = jnp.where(d == 1, NS_LAT_SEQ - 1 - half, half)
    return jnp.where(j < NS_CTX, j, NS_CTX + (jl // NS_LAT_SEQ) * NS_LAT_SEQ + half)


def _ssm_scan(u_ssm, pm, pmt2, bst, ckt, kst, pwr, pwi, h0):
    tab = lambda *shape: pl.BlockSpec((None,) + shape, lambda d, j: (d,) + (0,) * len(shape))
    return pl.pallas_call(
        _ssm_body,
        grid=(2, NS),
        in_specs=[
            pl.BlockSpec((SUPER, BRANCH_W), lambda d, j: (_ssm_super(d, j), 0)),
            tab(T, T), tab(T, 2 * T),
            tab(N_WIN, SCAN_R * LANES, 4 * SLAB_LANES), tab(N_WIN, SCAN_R * LANES, 4 * SLAB_LANES),
            tab(N_WIN, SCAN_R * LANES, SCAN_R * LANES),
            tab(N_SLABS, SCAN_M, SUBLANES, SLAB_LANES), tab(N_SLABS, SCAN_M, SUBLANES, SLAB_LANES),
            pl.BlockSpec((None, None, N_SLABS, 2, 1, SLAB_LANES),
                         lambda d, j: (d, jnp.maximum(j - NS_CTX, 0) // NS_LAT_SEQ, 0, 0, 0, 0)),
        ],
        out_specs=[
            pl.BlockSpec((None, SUPER, BRANCH_W), lambda d, j: (d, _ssm_super(d, j), 0)),
            pl.BlockSpec((None, None, SCAN_G, N_SLABS, 2, 1, SLAB_LANES), lambda d, j: (d, j, 0, 0, 0, 0, 0)),
        ],
        out_shape=[jax.ShapeDtypeStruct((2, N_TOK, BRANCH_W), F32),
                   jax.ShapeDtypeStruct((2, NS, SCAN_G, N_SLABS, 2, 1, SLAB_LANES), F32)],
        scratch_shapes=[
            pltpu.VMEM((SCAN_G, T, BRANCH_W), BF16),
            pltpu.VMEM((N_WIN, SCAN_G * BLK_ROWS, SCAN_R * LANES), BF16),
            pltpu.VMEM((N_WIN, SCAN_G * BLK_ROWS, 4 * SLAB_LANES), F32),
            pltpu.VMEM((N_WIN, SCAN_G * BLK_ROWS, 4 * SLAB_LANES), F32),
            pltpu.VMEM((N_SLABS, 2, SUBLANES, SLAB_LANES), F32),
        ],
        compiler_params=_cparams(("arbitrary", "arbitrary")),
        name="ssm_scan",
    )(u_ssm, pm, pmt2, bst, ckt, kst, pwr, pwi, h0)


def _pre_body(n_x, *refs):
    (mod_ref, g_ref, win_ref, qag_ref, kvg_ref, wuq_ref, wuqs_ref, wuk_ref, wva_ref,
     wvb_ref, qng_ref, qngs_ref, kng_ref, kngs_ref, rc_ref, rsw_ref,
     ussm_ref, ufft_ref, q_ref, k_ref, va_ref, vb_ref, ckv_ref, kpe_ref, vc_ref, gb_ref) = refs[n_x:]
    i = pl.program_id(0)
    x = _load_stream(i, refs[:n_x])
    sh = mod_ref[:, 0:D_MODEL]
    sc = mod_ref[:, D_MODEL:2 * D_MODEL]
    xn = (_rms(x, g_ref[...]) * (1.0 + sc) + sh).astype(BF16)

    def proj(lo, hi):
        return _dot(xn, win_ref[:, lo:hi])

    zk = proj(Z_KPE, Z_KROT)
    krot = proj(Z_KROT, Z_KSW)
    kswp = proj(Z_KSW, Z_HIN)

    rc = rc_ref[...]
    rsw = rsw_ref[...]
    q_gc = qng_ref[...] * rc * ATTN_SCALE
    q_gs = qngs_ref[...] * rsw * ATTN_SCALE
    k_gc = kng_ref[...] * rc
    k_sw = kswp * (kngs_ref[...] * rsw)

    def rstd(v):
        return lax.rsqrt(jnp.sum(v * v, axis=-1, keepdims=True) * (1.0 / QK_DIM) + EPS)

    cqn = _rms(proj(Z_CQ, Z_CKV), qag_ref[...]).astype(BF16)
    qf = _dot(cqn, wuq_ref[...])
    qs = _dot(cqn, wuqs_ref[...])
    ckvn = _rms(proj(Z_CKV, Z_KPE), kvg_ref[...])
    ckvb = ckvn.astype(BF16)
    kf = _dot(ckvb, wuk_ref[...])
    va_ref[...] = _dot(ckvb, wva_ref[...]).astype(BF16)
    vb_ref[...] = _dot(ckvb, wvb_ref[...]).astype(BF16)
    for h in range(MLA_HEADS):
        lanes = slice(h * HEAD_PAD, (h + 1) * HEAD_PAD)
        qh = qf[:, lanes]
        q_ref[:, lanes] = ((qh * q_gc + qs[:, lanes] * q_gs) * rstd(qh)).astype(BF16)
        kh = kf[:, lanes] + krot
        k_ref[:, lanes] = ((kh * k_gc + k_sw) * rstd(kh)).astype(BF16)

    ussm_ref[...] = proj(Z_SSM, Z_FFT)
    ufft_ref[...] = proj(Z_FFT, Z_CQ).astype(BF16)
    h_in = proj(Z_HIN, Z_GB)
    gb_ref[...] = proj(Z_GB, Z_GC)
    vc_ref[...] = proj(Z_GC, Z_COLS) * h_in

    @pl.when(i < NTT_CTX)
    def _():
        ckv_ref[...] = ckvn
        kpe_ref[...] = zk[:, 0:ROPE_DIM]


def _pre(l, x, mod, p, rc, rsw):
    tile = _tok_tile_spec
    rope_blk = lambda i: (jnp.where(i < NTT_CTX, NTT_LAT_SEQ, i % NTT_LAT_SEQ), 0)
    rspec = pl.BlockSpec((TT, LANES), rope_blk)
    out = lambda width, dt: jax.ShapeDtypeStruct((N_TOK, width), dt)
    names = ('norm_mix_g', 'w_in_r', 'q_a_norm_g', 'kv_a_norm_g', 'wuq', 'wuqs', 'wuk', 'wva', 'wvb',
             'qng', 'qngs', 'kng', 'kngs')
    x_specs, x_args = _stream_specs(x, D_MODEL)
    return pl.pallas_call(
        functools.partial(_pre_body, len(x_args)),
        grid=(NTT,),
        in_specs=x_specs + [_mod_spec(l)]
        + [_layer_spec(l, p[n].shape[1:]) for n in names] + [rspec, rspec],
        out_specs=[tile(BRANCH_W), tile(BRANCH_W), tile(MLA_HEADS * HEAD_PAD), tile(MLA_HEADS * HEAD_PAD),
                   tile(BRANCH_W), tile(BRANCH_W),
                   _ctx_tile_spec(KV_RANK), _ctx_tile_spec(ROPE_DIM), tile(BRANCH_W), tile(BRANCH_W)],
        out_shape=[out(BRANCH_W, F32), out(BRANCH_W, BF16), out(MLA_HEADS * HEAD_PAD, BF16),
                   out(MLA_HEADS * HEAD_PAD, BF16), out(BRANCH_W, BF16), out(BRANCH_W, BF16),
                   jax.ShapeDtypeStruct((N_CTX, KV_RANK), F32), jax.ShapeDtypeStruct((N_CTX, ROPE_DIM), F32),
                   out(BRANCH_W, F32), out(BRANCH_W, F32)],
        compiler_params=_cparams(("arbitrary",)),
        name="pre_mixer",
    )(*x_args, mod, *[p[n] for n in names], rc, rsw)


def _kvc_body(ckv_ref, kr_ref, wuk_ref, wva_ref, wvb_ref, kng_ref, k_ref, va_ref, vb_ref):
    ckvb = ckv_ref[...].astype(BF16)
    kf = _dot(ckvb, wuk_ref[...])
    va_ref[...] = _dot(ckvb, wva_ref[...]).astype(BF16)
    vb_ref[...] = _dot(ckvb, wvb_ref[...]).astype(BF16)
    krot = kr_ref[...]
    g = kng_ref[...]
    for h in range(MLA_HEADS):
        lanes = slice(h * HEAD_PAD, (h + 1) * HEAD_PAD)
        kh = kf[:, lanes] + krot
        ss = jnp.sum(kh * kh, axis=-1, keepdims=True)
        k_ref[:, lanes] = (kh * lax.rsqrt(ss * (1.0 / QK_DIM) + EPS) * g).astype(BF16)


def _cache_kv(cache_ckv, cache_kr, wuk, wva, wvb, kng):
    nt = PAST_LEN // T
    wspec = lambda r, c: pl.BlockSpec((None, r, c), lambda l, b, t: (l, 0, 0))
    ospec = lambda c: pl.BlockSpec((None, None, T, c), lambda l, b, t: (l, b, t, 0))
    return pl.pallas_call(
        _kvc_body,
        grid=(DEPTH, DEC_BATCH, nt),
        in_specs=[
            pl.BlockSpec((None, None, T, KV_RANK), lambda l, b, t: (b, l, t, 0)),
            pl.BlockSpec((None, None, T, HEAD_PAD), lambda l, b, t: (b, l, t, 0)),
            wspec(KV_RANK, MLA_HEADS * HEAD_PAD), wspec(KV_RANK, BRANCH_W), wspec(KV_RANK, BRANCH_W),
            wspec(1, HEAD_PAD),
        ],
        out_specs=[ospec(MLA_HEADS * HEAD_PAD), ospec(BRANCH_W), ospec(BRANCH_W)],
        out_shape=[jax.ShapeDtypeStruct((DEPTH, DEC_BATCH, PAST_LEN, MLA_HEADS * HEAD_PAD), BF16),
                   jax.ShapeDtypeStruct((DEPTH, DEC_BATCH, PAST_LEN, BRANCH_W), BF16),
                   jax.ShapeDtypeStruct((DEPTH, DEC_BATCH, PAST_LEN, BRANCH_W), BF16)],
        compiler_params=_cparams(("parallel", "parallel", "parallel")),
        name="cache_kv",
    )(cache_ckv, cache_kr, wuk, wva, wvb, kng)


def _fnet_body(seq_len, u_ref, csl_ref, csc_ref, y_ref, v1_s, v2_s):
    r = pl.program_id(1)

    @pl.when(r == 0)
    def _():
        csc = csc_ref[...]
        for c in range(seq_len // T):
            rows = slice(c * T, (c + 1) * T)
            for g in range(FFT_GROUPS):
                lanes = slice(g * FFT_GW, (g + 1) * FFT_GW)
                v = _dot(u_ref[rows, lanes], csc)
                v1_s[rows, lanes] = v[:, :FFT_GW].astype(BF16)
                v2_s[rows, lanes] = v[:, FFT_GW:].astype(BF16)

    y = _dot(csl_ref[:, :seq_len], v1_s[...]) + _dot(csl_ref[:, seq_len:], v2_s[...])
    y_ref[...] = y.astype(BF16)


def _fnet(u_fft, csl, csc, seq_len, n_seq, tile0):
    nr = seq_len // T
    blk0 = tile0 * T // seq_len
    return pl.pallas_call(
        functools.partial(_fnet_body, seq_len),
        grid=(n_seq, nr),
        in_specs=[
            pl.BlockSpec((seq_len, BRANCH_W), lambda b, r: (blk0 + b, 0)),
            pl.BlockSpec((T, 2 * seq_len), lambda b, r: (r, 0)),
            _const_spec((FFT_GW, 2 * FFT_GW)),
        ],
        out_specs=pl.BlockSpec((T, BRANCH_W), lambda b, r: (b * nr + r, 0)),
        out_shape=jax.ShapeDtypeStruct((n_seq * seq_len, BRANCH_W), BF16),
        scratch_shapes=[pltpu.VMEM((seq_len, BRANCH_W), BF16), pltpu.VMEM((seq_len, BRANCH_W), BF16)],
        compiler_params=_cparams(("arbitrary", "arbitrary")),
        name="fnet_%d" % seq_len,
    )(u_fft, csl, csc)


def _dft_tables(n, split):
    m = jnp.arange(n, dtype=jnp.int32)
    a = jnp.arange(n // split, dtype=jnp.int32) * split
    b = jnp.arange(split, dtype=jnp.int32)
    w = 2.0 * math.pi / n
    ang_a = ((a[:, None] * m[None, :]) % n).astype(F32) * w
    ang_b = ((b[:, None] * m[None, :]) % n).astype(F32) * w
    ca, sa = jnp.cos(ang_a)[:, None, :], jnp.sin(ang_a)[:, None, :]
    cb, sb = jnp.cos(ang_b)[None, :, :], jnp.sin(ang_b)[None, :, :]
    s = 1.0 / math.sqrt(n)
    cos = ((ca * cb - sa * sb) * s).reshape(n, n)
    sin = ((sa * cb + ca * sb) * s).reshape(n, n)
    return cos, sin


def _attn_body(n_seg, q_ref, *refs):
    k_refs = refs[0:n_seg]
    va_refs = refs[n_seg:2 * n_seg]
    vb_refs = refs[2 * n_seg:3 * n_seg]
    o_ref = refs[-1]
    for hp in range(MLA_HEADS // 2):
        out_lanes = slice(hp * LANES, (hp + 1) * LANES)
        acc = jnp.zeros((T, LANES), F32)
        for h, v_refs in ((2 * hp, va_refs), (2 * hp + 1, vb_refs)):
            lanes = slice(h * HEAD_PAD, (h + 1) * HEAD_PAD)
            qh = q_ref[:, lanes]
            ss = [_dot_nt(qh, kr[:, lanes]) for kr in k_refs]
            m = functools.reduce(jnp.maximum, [jnp.max(s, axis=-1, keepdims=True) for s in ss])
            ps = [jnp.exp(s - m) for s in ss]
            l = functools.reduce(lambda a, b: a + b, [jnp.sum(p, axis=-1, keepdims=True) for p in ps])
            o = functools.reduce(lambda a, b: a + b,
                                 [_dot(p.astype(BF16), vr[:, out_lanes]) for p, vr in zip(ps, v_refs)])
            acc = acc + o / l
        o_ref[:, out_lanes] = acc.astype(BF16)


def _attention(q, segs, n_seq, nq, tile0):
    in_specs = [pl.BlockSpec((T, MLA_HEADS * HEAD_PAD), lambda b, r: (tile0 + b * nq + r, 0))]
    args = [q]
    for idx in range(3):
        for seg in segs:
            arr = seg[idx]
            width = arr.shape[-1]
            in_specs.append(pl.BlockSpec(seg[3] + (width,), seg[4]))
            args.append(arr)
    return pl.pallas_call(
        functools.partial(_attn_body, len(segs)),
        grid=(n_seq, nq),
        in_specs=in_specs,
        out_specs=pl.BlockSpec((T, BRANCH_W), lambda b, r: (b * nq + r, 0)),
        out_shape=jax.ShapeDtypeStruct((n_seq * nq * T, BRANCH_W), BF16),
        compiler_params=_cparams(("arbitrary", "arbitrary")),
        name="attention_%d" % nq,
    )(*args)


def _mix_body(n_x, *refs):
    (mod_ref, g_ref, yf_ref, yb_ref, u_ref, dsk_ref, wglu_ref, fc_ref, fl_ref,
     ac_ref, al_ref, vc_ref, vp_ref, vn_ref, cw_ref, gb_ref, wbr_ref, wg_ref, bg_ref, wo_ref, o_ref) = refs[n_x:]
    i = pl.program_id(0)
    x = _load_stream(i, refs[:n_x])
    sh = mod_ref[:, 0:D_MODEL]
    sc = mod_ref[:, D_MODEL:2 * D_MODEL]
    gm = mod_ref[:, 2 * D_MODEL:3 * D_MODEL]
    xn = (_rms(x, g_ref[...]) * (1.0 + sc) + sh).astype(BF16)

    y = jax.nn.gelu(yf_ref[...] + yb_ref[...] + u_ref[...] * dsk_ref[...], approximate=True)
    y = y * _sigmoid(_dot(y.astype(BF16), wglu_ref[...]))

    v = vc_ref[...]
    row = lax.broadcasted_iota(jnp.int32, (TT, BRANCH_W), 0)
    seq_len = jnp.where(i < NTT_CTX, SEQ, DEC_SEQ)
    pos = jnp.bitwise_and(i * TT + row, seq_len - 1)
    v_prev = jnp.where(row == 0, vp_ref[SUBLANES - 1:SUBLANES, :], pltpu.roll(v, 1, 0))
    v_next = jnp.where(row == TT - 1, vn_ref[0:1, :], pltpu.roll(v, TT - 1, 0))
    v_prev = jnp.where(pos == 0, 0.0, v_prev)
    v_next = jnp.where(pos == seq_len - 1, 0.0, v_next)
    y_conv = gb_ref[...] * (cw_ref[0:1, :] * v_prev + cw_ref[1:2, :] * v + cw_ref[2:3, :] * v_next)

    branches = (y.astype(BF16), _load_pair(i, fc_ref, fl_ref), _load_pair(i, ac_ref, al_ref),
                y_conv.astype(BF16))
    merged = jnp.zeros((TT, D_MODEL), F32)
    for k in range(N_BRANCH):
        cols = slice(k * D_MODEL, (k + 1) * D_MODEL)
        gate = _sigmoid(_dot(xn, wg_ref[:, cols]) + bg_ref[:, cols])
        merged = merged + gate * _dot(branches[k], wbr_ref[k])
    out = _dot(merged.astype(BF16), wo_ref[...])
    o_ref[...] = x + gm * out


def _mix(l, x, mod, p, y_ssm, u_ssm, fft_c, fft_l, att_c, att_l, vconv, g_b):
    tile = _tok_tile_spec
    rb = TT // SUBLANES
    n_rb = N_TOK // SUBLANES
    lspec = lambda n: _layer_spec(l, p[n].shape[1:])
    x_specs, x_args = _stream_specs(x, D_MODEL)
    return pl.pallas_call(
        functools.partial(_mix_body, len(x_args)),
        grid=(NTT,),
        in_specs=x_specs + [
            _mod_spec(l), lspec('norm_mix_g'),
            pl.BlockSpec((None, TT, BRANCH_W), lambda i: (0, i, 0)),
            pl.BlockSpec((None, TT, BRANCH_W), lambda i: (1, i, 0)),
            tile(BRANCH_W), lspec('ssm_d'), lspec('w_glu'),
            _ctx_tile_spec(BRANCH_W), _lat_tile_spec(BRANCH_W), _ctx_tile_spec(BRANCH_W), _lat_tile_spec(BRANCH_W),
            tile(BRANCH_W),
            pl.BlockSpec((SUBLANES, BRANCH_W), lambda i: (jnp.maximum(i * rb - 1, 0), 0)),
            pl.BlockSpec((SUBLANES, BRANCH_W), lambda i: (jnp.minimum(i * rb + rb, n_rb - 1), 0)),
            lspec('conv_w'), tile(BRANCH_W),
            lspec('w_branch'), lspec('w_gate'), lspec('b_gate'), lspec('w_out'),
        ],
        out_specs=tile(D_MODEL),
        out_shape=jax.ShapeDtypeStruct((N_TOK, D_MODEL), F32),
        compiler_params=_cparams(("arbitrary",)),
        name="mixer_merge",
    )(*x_args, mod, p['norm_mix_g'], y_ssm, y_ssm, u_ssm, p['ssm_d'], p['w_glu'], fft_c, fft_l, att_c, att_l,
      vconv, vconv, vconv, p['conv_w'], g_b, p['w_branch'], p['w_gate'], p['b_gate'], p['w_out'])


FF_CHUNK = D_FF // 2


def _ffn_body(n_out, x_ref, mod_ref, g_ref, wi_ref, wo_ref, *o_refs):
    i = pl.program_id(0)
    x = x_ref[...]
    sh = mod_ref[:, 3 * D_MODEL:4 * D_MODEL]
    sc = mod_ref[:, 4 * D_MODEL:5 * D_MODEL]
    gf = mod_ref[:, 5 * D_MODEL:6 * D_MODEL]
    xn = (_rms(x, g_ref[...]) * (1.0 + sc) + sh).astype(BF16)
    acc = jnp.zeros((TT, D_MODEL), F32)
    for c in range(D_FF // FF_CHUNK):
        lo = c * FF_CHUNK
        g = _dot(xn, wi_ref[:, lo:lo + FF_CHUNK])
        u = _dot(xn, wi_ref[:, D_FF + lo:D_FF + lo + FF_CHUNK])
        h = (g * _sigmoid(g) * u).astype(BF16)
        acc = acc + _dot(h, wo_ref[lo:lo + FF_CHUNK, :])
    _store_stream(i, o_refs[:n_out], x + gf * acc)


def _ffn(l, x, mod, p, split_out):
    lspec = lambda n: _layer_spec(l, p[n].shape[1:])
    tile = _tok_tile_spec(D_MODEL)
    if split_out:
        out_specs = [_ctx_tile_spec(D_MODEL), _lat_tile_spec(D_MODEL)]
        out_shape = [jax.ShapeDtypeStruct((N_CTX, D_MODEL), F32), jax.ShapeDtypeStruct((N_LAT, D_MODEL), F32)]
    else:
        out_specs = [tile]
        out_shape = [jax.ShapeDtypeStruct((N_TOK, D_MODEL), F32)]
    return pl.pallas_call(
        functools.partial(_ffn_body, len(out_specs)),
        grid=(NTT,),
        in_specs=[tile, _mod_spec(l), lspec('norm_ffn_g'), lspec('w_ffn_in'), lspec('w_ffn_out')],
        out_specs=out_specs,
        out_shape=out_shape,
        compiler_params=_cparams(("arbitrary",)),
        name="ffn",
    )(x, mod, p['norm_ffn_g'], p['w_ffn_in'], p['w_ffn_out'])


def _prep_weights(norm_mix_g, norm_ffn_g, w_in, q_a_norm_g, kv_a_norm_g, w_uq, w_ukv, q_norm_g, k_norm_g,
                  ssm_d, w_glu, conv_w, w_branch, w_gate, b_gate, w_out, w_ffn_in, w_ffn_out):
    vec = lambda a: a.reshape(DEPTH, 1, a.shape[-1])
    kpe = w_in[:, :, OFF_KPE:OFF_KPE + ROPE_DIM]
    z = lambda n: jnp.zeros((DEPTH, D_MODEL, n), F32)
    w_in_r = jnp.concatenate([
        w_in[:, :, OFF_SSM:OFF_KPE],
        kpe, z(LANES - ROPE_DIM),
        z(NOPE_DIM), kpe[:, :, ROPE_PERM], z(LANES - QK_DIM),
        z(NOPE_DIM), kpe[:, :, ROPE_PARTNER], z(LANES - QK_DIM),
        w_in[:, :, OFF_CONV:OFF_CONV + 3 * BRANCH_W],
    ], axis=2).astype(BF16)
    wq = w_uq.reshape(DEPTH, Q_RANK, MLA_HEADS, QK_DIM)
    head_pad = lambda a: jnp.pad(a, ((0, 0), (0, 0), (0, 0), (0, HEAD_PAD - a.shape[-1]))).reshape(
        DEPTH, a.shape[1], MLA_HEADS * HEAD_PAD).astype(BF16)
    wuq = head_pad(jnp.concatenate([wq[..., :NOPE_DIM], wq[..., NOPE_DIM:][..., ROPE_PERM]], axis=3))
    wuqs = head_pad(jnp.concatenate([jnp.zeros_like(wq[..., :NOPE_DIM]), wq[..., NOPE_DIM:][..., ROPE_PARTNER]],
                                    axis=3))
    wkv = w_ukv.reshape(DEPTH, KV_RANK, MLA_HEADS, NOPE_DIM + V_DIM)
    wuk = head_pad(wkv[..., :NOPE_DIM])
    wv = wkv[..., NOPE_DIM:]
    even = (jnp.arange(MLA_HEADS) % 2 == 0)[None, None, :, None]
    wva = jnp.where(even, wv, 0.0).reshape(DEPTH, KV_RANK, BRANCH_W).astype(BF16)
    wvb = jnp.where(even, 0.0, wv).reshape(DEPTH, KV_RANK, BRANCH_W).astype(BF16)

    def gain(g, perm):
        gp = jnp.concatenate([g[:, :NOPE_DIM], g[:, NOPE_DIM:][:, perm],
                              jnp.zeros((DEPTH, HEAD_PAD - QK_DIM), F32)], axis=1)
        return gp.reshape(DEPTH, 1, HEAD_PAD)

    return dict(
        norm_mix_g=vec(norm_mix_g), norm_ffn_g=vec(norm_ffn_g), w_in_r=w_in_r,
        q_a_norm_g=vec(q_a_norm_g), kv_a_norm_g=vec(kv_a_norm_g),
        wuq=wuq, wuqs=wuqs, wuk=wuk, wva=wva, wvb=wvb,
        qng=gain(q_norm_g, ROPE_PERM), qngs=gain(q_norm_g, ROPE_PARTNER),
        kng=gain(k_norm_g, ROPE_PERM), kngs=gain(k_norm_g, ROPE_PARTNER),
        ssm_d=vec(ssm_d), w_glu=w_glu.astype(BF16),
        conv_w=jnp.pad(conv_w, ((0, 0), (0, SUBLANES - 3), (0, 0))),
        w_branch=w_branch.astype(BF16), w_gate=w_gate.astype(BF16), b_gate=vec(b_gate),
        w_out=w_out.astype(BF16), w_ffn_in=w_ffn_in.astype(BF16), w_ffn_out=w_ffn_out.astype(BF16),
    )


def _rope_tables():
    rows = DEC_SEQ // GRID_W
    r = jnp.repeat(jnp.arange(rows, dtype=F32), GRID_W)
    c = jnp.tile(jnp.arange(GRID_W, dtype=F32), rows)
    n_freq = ROPE_DIM // 4
    inv = ROPE_THETA ** (-jnp.arange(n_freq, dtype=F32) / n_freq)
    ang = jnp.concatenate([r[:, None] * inv, c[:, None] * inv], axis=-1)
    cos, sin = jnp.cos(ang), jnp.sin(ang)
    ones = jnp.ones((DEC_SEQ, NOPE_DIM), F32)
    zeros = jnp.zeros((DEC_SEQ, NOPE_DIM), F32)
    tail1 = jnp.ones((DEC_SEQ, HEAD_PAD - QK_DIM), F32)
    tail0 = jnp.zeros((DEC_SEQ, HEAD_PAD - QK_DIM), F32)
    rc = jnp.concatenate([ones, cos, cos, tail1], axis=1)
    rsw = jnp.concatenate([zeros, -sin, sin, tail0], axis=1)
    ident = lambda v: jnp.full((TT, HEAD_PAD), v, F32)
    return jnp.concatenate([rc, ident(1.0)], axis=0), jnp.concatenate([rsw, ident(0.0)], axis=0)


def kernel(x_prompt, x_sample, cache_ckv, cache_kpe, state_ssm, c, c_ctx, norm_mix_g, norm_ffn_g, w_ada, b_ada,
           w_in, q_a_norm_g, kv_a_norm_g, w_uq, w_ukv, q_norm_g, k_norm_g, ssm_lam_re, ssm_lam_im, ssm_log_dt,
           ssm_b_re, ssm_b_im, ssm_c_re, ssm_c_im, ssm_d, w_glu, conv_w, w_branch, w_gate, b_gate, w_out,
           w_ffn_in, w_ffn_out):
    xp = x_prompt.reshape(N_CTX, D_MODEL)
    xs = x_sample.reshape(N_LAT, D_MODEL)

    cond8 = jnp.concatenate([c_ctx[None, :], c, jnp.zeros((SUBLANES - 1 - DEC_BATCH, D_MODEL), F32)], axis=0)
    mod = _modulation(cond8, w_ada, b_ada).reshape(DEPTH, SUBLANES, 1, 6 * D_MODEL)

    bst, ckt, kst, pwr, pwi = _ssm_prep(ssm_lam_re, ssm_lam_im, ssm_log_dt, ssm_b_re, ssm_b_im,
                                        ssm_c_re, ssm_c_im)
    pm, pmt2 = _scan_perms()
    st = state_ssm.transpose(1, 2, 0, 5, 3, 4).reshape(DEPTH, 2, DEC_BATCH, 2, N_SLABS, 1, SLAB_LANES)
    h0_all = st.transpose(0, 1, 2, 4, 3, 5, 6)

    rc, rsw = _rope_tables()
    cl, sl = _dft_tables(SEQ, 16)
    csl_ctx = jnp.concatenate([cl, -sl], axis=1).astype(BF16)
    cl, sl = _dft_tables(DEC_SEQ, 64)
    csl_lat = jnp.concatenate([cl, -sl], axis=1).astype(BF16)
    cc, sc = _dft_tables(FFT_GW, 8)
    csc = jnp.concatenate([cc, sc], axis=1).astype(BF16)

    p = _prep_weights(norm_mix_g, norm_ffn_g, w_in, q_a_norm_g, kv_a_norm_g, w_uq, w_ukv, q_norm_g, k_norm_g,
                      ssm_d, w_glu, conv_w, w_branch, w_gate, b_gate, w_out, w_ffn_in, w_ffn_out)
    cache_kr = jnp.pad(cache_kpe[..., ROPE_PERM], ((0, 0), (0, 0), (0, 0), (NOPE_DIM, HEAD_PAD - QK_DIM)))
    kc, vca, vcb = _cache_kv(cache_ckv, cache_kr, p['wuk'], p['wva'], p['wvb'], p['kng'])

    lat_blk = N_CTX // DEC_SEQ
    ckv_list, kpe_list, fin_list = [], [], []
    x = (xp, xs)
    for l in range(DEPTH):
        (u_ssm, u_fft, q, k, va, vb, ckv_n, kpe, vconv, g_b) = _pre(l, x, mod, p, rc, rsw)
        ckv_list.append(ckv_n.reshape(BATCH, SEQ, KV_RANK))
        kpe_list.append(kpe.reshape(BATCH, SEQ, ROPE_DIM))

        y_ssm, fin = _ssm_scan(u_ssm, pm, pmt2, bst[l], ckt[l], kst[l], pwr[l], pwi[l], h0_all[l])
        fin_list.append(fin[:, :NS_CTX].reshape(2, BATCH, N_SLABS, 2, SLAB_LANES))

        fft_c = _fnet(u_fft, csl_ctx, csc, SEQ, BATCH, 0)
        fft_l = _fnet(u_fft, csl_lat, csc, DEC_SEQ, DEC_BATCH, NT_CTX)

        ctx_seg = (k, va, vb, (T,), lambda b, r: (b, 0))
        att_c = _attention(q, [ctx_seg], BATCH, 1, 0)
        lat_seg = (k, va, vb, (DEC_SEQ,), lambda b, r: (lat_blk + b, 0))
        cache_seg = (kc[l], vca[l], vcb[l], (None, PAST_LEN), lambda b, r: (b, 0, 0))
        att_l = _attention(q, [lat_seg, cache_seg], DEC_BATCH, NT_LAT_SEQ, NT_CTX)

        x = _mix(l, x, mod, p, y_ssm, u_ssm, fft_c, fft_l, att_c, att_l, vconv, g_b)
        x = _ffn(l, x, mod, p, split_out=(l == DEPTH - 1))
        x = tuple(x) if l == DEPTH - 1 else x[0]

    y_p = x[0].reshape(BATCH, SEQ, D_MODEL)
    y_s = x[1].reshape(DEC_BATCH, DEC_SEQ, D_MODEL)
    new_ckv = jnp.stack(ckv_list, axis=1)
    new_kpe = jnp.stack(kpe_list, axis=1)
    fin = jnp.stack(fin_list, axis=0)
    fin = fin.reshape(DEPTH, 2, BATCH, N_SLABS, 2, 4, SSM_STATE).transpose(2, 0, 1, 3, 5, 6, 4)
    new_ssm = fin.reshape(BATCH, DEPTH, 2, SSM_GROUPS, SSM_STATE, 2)
    return (y_p, y_s, new_ckv, new_kpe, new_ssm)
```

```python
import functools
import math

import numpy as np
import jax
import jax.numpy as jnp
from jax import lax
from jax.experimental import pallas as pl
from jax.experimental.pallas import tpu as pltpu

F32 = jnp.float32
BF16 = jnp.bfloat16

D_MODEL = 1024
BATCH = 32
SEQ = 256
DEPTH = 4
DEC_BATCH = 2
DEC_SEQ = 2048
PAST_LEN = 512
GRID_W = 64
N_BRANCH = 4
BRANCH_W = 512
SSM_GROUP = 16
SSM_GROUPS = 32
SSM_STATE = 64
FFT_GROUPS = 4
FFT_GW = 128
MLA_HEADS = 8
NOPE_DIM = 64
ROPE_DIM = 32
V_DIM = 64
QK_DIM = 96
Q_RANK = 384
KV_RANK = 256
ROPE_THETA = 10000.0
D_FF = 2816
EPS = 1e-6

OFF_SSM = 0
OFF_FFT = 512
OFF_CQ = 1024
OFF_CKV = 1408
OFF_KPE = 1664
OFF_CONV = 1696

LANES = 128
SUBLANES = 8
VMEM_LIMIT_BYTES = 56 * 1024 * 1024

T = 256
N_CTX = BATCH * SEQ
N_LAT = DEC_BATCH * DEC_SEQ
N_TOK = N_CTX + N_LAT
NT_CTX = N_CTX // T
NT_LAT_SEQ = DEC_SEQ // T
NT = N_TOK // T
TT = 512
NTT_CTX = N_CTX // TT
NTT_LAT_SEQ = DEC_SEQ // TT
NTT = N_TOK // TT
N_SEQ = BATCH + DEC_BATCH
HEAD_PAD = LANES
ATTN_SCALE = QK_DIM ** -0.5 * math.log2(math.e)


def _ones_column(offset):
    lane = lax.broadcasted_iota(jnp.int32, (1, BRANCH_W), 1)
    return (lane % LANES == offset).astype(F32)

Z_SSM = 0
Z_FFT = 512
Z_CQ = 1024
Z_CKV = 1408
Z_KPE = 1664
Z_KROT = 1792
Z_KSW = 1920
Z_HIN = 2048
Z_GB = 2560
Z_GC = 3072
Z_COLS = 3584

SCAN_R = 4
SCAN_M = T // SUBLANES // SCAN_R
BLK_ROWS = T // SCAN_R
SCAN_G = 4
SUPER = SCAN_G * T
NS_CTX = N_CTX // SUPER
NS_LAT_SEQ = DEC_SEQ // SUPER
NS = N_TOK // SUPER
SLAB_LANES = 256
N_SLABS = SSM_GROUPS * SSM_STATE // SLAB_LANES
N_WIN = BRANCH_W // LANES
GC = SSM_GROUPS * SSM_GROUP
ROPE_PERM = np.concatenate([np.arange(0, ROPE_DIM, 2), np.arange(1, ROPE_DIM, 2)])
ROPE_PARTNER = np.concatenate([np.arange(1, ROPE_DIM, 2), np.arange(0, ROPE_DIM, 2)])


def _cparams(sem):
    return pltpu.CompilerParams(dimension_semantics=sem, vmem_limit_bytes=VMEM_LIMIT_BYTES)


def _const_spec(shape):
    nd = len(shape)
    return pl.BlockSpec(shape, lambda *_: (0,) * nd, pipeline_mode=pl.Buffered(1))


def _layer_spec(l, shape):
    nd = len(shape)
    return pl.BlockSpec((None,) + tuple(shape), lambda *_: (l,) + (0,) * nd, pipeline_mode=pl.Buffered(1))


def _mod_spec(l):
    return pl.BlockSpec((None, None, 1, 6 * D_MODEL), lambda i: (l, _cond_row(i), 0, 0))


def _ctx_tile_spec(width):
    return pl.BlockSpec((TT, width), lambda i: (jnp.minimum(i, NTT_CTX - 1), 0))


def _lat_tile_spec(width):
    return pl.BlockSpec((TT, width), lambda i: (jnp.maximum(i - NTT_CTX, 0), 0))


def _tok_tile_spec(width):
    return pl.BlockSpec((TT, width), lambda i: (i, 0))


def _load_pair(i, ctx_ref, lat_ref):
    return jnp.where(i < NTT_CTX, ctx_ref[...], lat_ref[...])


def _stream_specs(x, width):
    if isinstance(x, tuple):
        return [_ctx_tile_spec(width), _lat_tile_spec(width)], list(x)
    return [_tok_tile_spec(width)], [x]


def _load_stream(i, refs):
    return refs[0][...] if len(refs) == 1 else _load_pair(i, *refs)


def _store_stream(i, refs, val):
    if len(refs) == 1:
        refs[0][...] = val
        return
    ctx_ref, lat_ref = refs

    @pl.when(i < NTT_CTX)
    def _():
        ctx_ref[...] = val

    lat_ref[...] = val


def _dot(a, b):
    return jnp.dot(a, b, preferred_element_type=F32)


def _dot_nt(a, b):
    return lax.dot_general(a, b, (((1,), (1,)), ((), ())), preferred_element_type=F32)


def _sigmoid(x):
    return 1.0 / (1.0 + jnp.exp(-x))


def _rms(x, g):
    return x * lax.rsqrt(jnp.mean(x * x, axis=-1, keepdims=True) + EPS) * g


def _cond_row(i):
    return jnp.maximum(i - (NTT_CTX - NTT_LAT_SEQ), 0) // NTT_LAT_SEQ


def _mod_body(c_ref, w_ref, b_ref, o_ref):
    c = c_ref[...]
    s = c * _sigmoid(c)
    w = w_ref[...]
    s_hi = s.astype(BF16)
    s_lo = (s - s_hi.astype(F32)).astype(BF16)
    w_hi = w.astype(BF16)
    w_lo = (w - w_hi.astype(F32)).astype(BF16)
    o_ref[...] = _dot(s_hi, w_hi) + _dot(s_lo, w_hi) + _dot(s_hi, w_lo) + b_ref[...]


def _modulation(cond8, w_ada, b_ada):
    n_col = 6 * D_MODEL // 1024
    return pl.pallas_call(
        _mod_body,
        grid=(DEPTH, n_col),
        in_specs=[
            pl.BlockSpec((SUBLANES, D_MODEL), lambda l, j: (0, 0)),
            pl.BlockSpec((None, D_MODEL, 1024), lambda l, j: (l, 0, j)),
            pl.BlockSpec((None, 1, 1024), lambda l, j: (l, 0, j)),
        ],
        out_specs=pl.BlockSpec((None, SUBLANES, 1024), lambda l, j: (l, 0, j)),
        out_shape=jax.ShapeDtypeStruct((DEPTH, SUBLANES, 6 * D_MODEL), F32),
        compiler_params=_cparams(("parallel", "parallel")),
        name="modulation",
    )(cond8, w_ada, b_ada.reshape(DEPTH, 1, 6 * D_MODEL))


def _dot_nt3(a, b):
    a_hi = a.astype(BF16)
    a_lo = (a - a_hi.astype(F32)).astype(BF16)
    b_hi = b.astype(BF16)
    b_lo = (b - b_hi.astype(F32)).astype(BF16)
    return _dot_nt(a_hi, b_hi) + _dot_nt(a_lo, b_hi) + _dot_nt(a_hi, b_lo)


def _ssm_prep_body(lre_ref, lim_ref, ldt_ref, bre_ref, bim_ref, cre_ref, cim_ref, lre_r, lim_r, ldt_r,
                   ere_ref, eim_ref, bst_ref, ckt_ref, kst_ref, pwr_ref, pwi_ref):
    lam_re = lre_ref[...]
    lam_im = lim_ref[...]
    dt = jnp.exp(ldt_ref[...])
    xr = lam_re * dt
    xi = lam_im * dt

    mag = jnp.exp(xr)
    a_re = mag * jnp.cos(xi)
    a_im = mag * jnp.sin(xi)
    pw = [(jnp.ones_like(a_re), jnp.zeros_like(a_re)), (a_re, a_im)]
    for _ in range(SCAN_R - 1):
        pr, pi = pw[-1]
        pw.append((pr * a_re - pi * a_im, pr * a_im + pi * a_re))
    den = lam_re * lam_re + lam_im * lam_im
    f_re = ((a_re - 1.0) * lam_re + a_im * lam_im) / den
    f_im = (a_im * lam_re - (a_re - 1.0) * lam_im) / den
    bb_re = f_re * bre_ref[...] - f_im * bim_ref[...]
    bb_im = f_re * bim_ref[...] + f_im * bre_ref[...]
    c_re = cre_ref[...]
    c_im = cim_ref[...]
    ere = ere_ref[...]
    eim = eim_ref[...]

    r_g = lax.broadcasted_iota(jnp.int32, (LANES, 4 * SLAB_LANES), 0) // SSM_GROUP
    col = lax.broadcasted_iota(jnp.int32, (LANES, 4 * SLAB_LANES), 1)
    c_g = 4 * (col // (2 * SLAB_LANES)) + (col % SLAB_LANES) // SSM_STATE
    same_group = r_g == c_g

    def embed(v_re, v_im, w):
        rows = slice(w * LANES, (w + 1) * LANES)
        blk = _dot(v_re[rows, :].astype(BF16), ere) + _dot(v_im[rows, :].astype(BF16), eim)
        return jnp.where(same_group, blk, 0.0).astype(BF16)

    for i in range(SCAN_R):
        pr, pi = pw[SCAN_R - 1 - i]
        g_re = pr * bb_re - pi * bb_im
        g_im = pr * bb_im + pi * bb_re
        qr, qi = pw[i + 1]
        w_re = c_re * qr - c_im * qi
        w_im = -(c_re * qi + c_im * qr)
        for w in range(N_WIN):
            bst_ref[w, i * LANES:(i + 1) * LANES, :] = embed(g_re, g_im, w)
            ckt_ref[w, i * LANES:(i + 1) * LANES, :] = embed(w_re, w_im, w)

    kr = lax.broadcasted_iota(jnp.int32, (LANES, LANES), 0) // SSM_GROUP
    kc = lax.broadcasted_iota(jnp.int32, (LANES, LANES), 1) // SSM_GROUP
    k_same = kr == kc
    zero_blk = jnp.zeros((LANES, LANES), BF16)
    for tau in range(SCAN_R):
        pr, pi = pw[tau]
        w_re = c_re * pr - c_im * pi
        w_im = c_re * pi + c_im * pr
        for w in range(N_WIN):
            rows = slice(w * LANES, (w + 1) * LANES)
            kt = _dot_nt3(bb_re[rows, :], w_re[rows, :]) - _dot_nt3(bb_im[rows, :], w_im[rows, :])
            ktm = jnp.where(k_same, kt, 0.0).astype(BF16)
            for i_in in range(SCAN_R - tau):
                i_out = i_in + tau
                kst_ref[w, i_in * LANES:(i_in + 1) * LANES, i_out * LANES:(i_out + 1) * LANES] = ktm
            if tau > 0:
                for i_out in range(tau):
                    i_in = i_out + SCAN_R - tau
                    kst_ref[w, i_in * LANES:(i_in + 1) * LANES, i_out * LANES:(i_out + 1) * LANES] = zero_blk

    shape2 = (SCAN_M, SSM_GROUPS * SSM_STATE)
    n = ((lax.broadcasted_iota(jnp.int32, shape2, 0) + 1) * SCAN_R).astype(F32)
    dt_r = jnp.exp(ldt_r[...])
    e = jnp.exp(n * (lre_r[...] * dt_r))
    th = n * (lim_r[...] * dt_r)
    p_re = e * jnp.cos(th)
    p_im = e * jnp.sin(th)
    for s in range(N_SLABS):
        lanes = slice(s * SLAB_LANES, (s + 1) * SLAB_LANES)
        for m in range(SCAN_M):
            pwr_ref[s, m] = jnp.broadcast_to(p_re[m:m + 1, lanes], (SUBLANES, SLAB_LANES))
            pwi_ref[s, m] = jnp.broadcast_to(p_im[m:m + 1, lanes], (SUBLANES, SLAB_LANES))


def _embed_consts():
    e = np.zeros((2, SSM_STATE, 4 * SLAB_LANES), np.float32)
    for ri in range(2):
        for sg in range(2):
            for g4 in range(4):
                base = sg * 2 * SLAB_LANES + ri * SLAB_LANES + g4 * SSM_STATE
                e[ri, np.arange(SSM_STATE), base + np.arange(SSM_STATE)] = 1.0
    return jnp.asarray(e[0], BF16), jnp.asarray(e[1], BF16)


def _ssm_prep(lam_re, lam_im, log_dt, b_re, b_im, c_re, c_im):
    ld = DEPTH * 2
    gp = SSM_GROUPS * SSM_STATE
    rep = lambda a: jnp.broadcast_to(a.reshape(ld, SSM_GROUPS, 1, -1),
                                     (ld, SSM_GROUPS, SSM_GROUP, SSM_STATE)).reshape(ld, GC, SSM_STATE)
    ldt = log_dt.reshape(ld, SSM_GROUPS, 1)
    row = lambda a: jnp.broadcast_to(a.reshape(ld, SSM_GROUPS, -1), (ld, SSM_GROUPS, SSM_STATE)).reshape(ld, 1, gp)
    bt = lambda b: b.reshape(ld, SSM_GROUPS, SSM_STATE, SSM_GROUP).transpose(0, 1, 3, 2).reshape(ld, GC, SSM_STATE)
    ct = lambda c: c.reshape(ld, GC, SSM_STATE)
    ere, eim = _embed_consts()
    gspec = pl.BlockSpec((None, GC, SSM_STATE), lambda i: (i, 0, 0))
    rspec = pl.BlockSpec((None, 1, gp), lambda i: (i, 0, 0))
    espec = _const_spec((SSM_STATE, 4 * SLAB_LANES))
    tspec = lambda r, c: pl.BlockSpec((None, N_WIN, r, c), lambda i: (i, 0, 0, 0))
    pspec = pl.BlockSpec((None, N_SLABS, SCAN_M, SUBLANES, SLAB_LANES), lambda i: (i, 0, 0, 0, 0))
    outs = pl.pallas_call(
        _ssm_prep_body,
        grid=(ld,),
        in_specs=[gspec] * 7 + [rspec] * 3 + [espec, espec],
        out_specs=[tspec(SCAN_R * LANES, 4 * SLAB_LANES), tspec(SCAN_R * LANES, 4 * SLAB_LANES),
                   tspec(SCAN_R * LANES, SCAN_R * LANES), pspec, pspec],
        out_shape=[jax.ShapeDtypeStruct((ld, N_WIN, SCAN_R * LANES, 4 * SLAB_LANES), BF16),
                   jax.ShapeDtypeStruct((ld, N_WIN, SCAN_R * LANES, 4 * SLAB_LANES), BF16),
                   jax.ShapeDtypeStruct((ld, N_WIN, SCAN_R * LANES, SCAN_R * LANES), BF16),
                   jax.ShapeDtypeStruct((ld, N_SLABS, SCAN_M, SUBLANES, SLAB_LANES), F32),
                   jax.ShapeDtypeStruct((ld, N_SLABS, SCAN_M, SUBLANES, SLAB_LANES), F32)],
        compiler_params=_cparams(("arbitrary",)),
        name="ssm_prep",
    )(rep(lam_re), rep(lam_im), rep(ldt), bt(b_re), bt(b_im), ct(c_re), ct(c_im),
      row(lam_re), row(lam_im), row(ldt), ere, eim)
    return [o.reshape((DEPTH, 2) + o.shape[1:]) for o in outs]


def _scan_perms():
    rho = np.arange(T)
    i, m, q = rho // BLK_ROWS, (rho % BLK_ROWS) // SUBLANES, rho % SUBLANES
    t_of_rho = q * (T // SUBLANES) + SCAN_R * m + i
    pm = np.zeros((2, T, T), np.float32)
    pm[0, rho, t_of_rho] = 1.0
    pm[1, rho, T - 1 - t_of_rho] = 1.0
    pmt = np.transpose(pm, (0, 2, 1))
    pmt2 = np.concatenate([pmt, pmt], axis=2)
    return jnp.asarray(pm, BF16), jnp.asarray(pmt2, BF16)


def _ssm_body(u_ref, pm_ref, pmt2_ref, bst_ref, ckt_ref, kst_ref, pwr_ref, pwi_ref, h0_ref,
              y_ref, fin_ref, up_s, uc_s, x_s, hp_s, st_s):
    d = pl.program_id(0)
    j = pl.program_id(1)
    is_ctx = j < NS_CTX
    first = jnp.logical_or(is_ctx, (j - NS_CTX) % NS_LAT_SEQ == 0)

    @pl.when(first)
    def _():
        st_s[...] = jnp.where(is_ctx, 0.0, jnp.broadcast_to(h0_ref[...], st_s.shape))

    def tile_rows(g):
        gg = jnp.where(d == 1, SCAN_G - 1 - g, g)
        return pl.ds(pl.multiple_of(gg * T, T), T), gg

    pm = pm_ref[...]
    for g in range(SCAN_G):
        up_s[g] = _dot(pm, u_ref[tile_rows(g)[0], :].astype(BF16)).astype(BF16)

    for w in range(N_WIN):
        lanes = slice(w * LANES, (w + 1) * LANES)
        lhs = jnp.concatenate(
            [jnp.concatenate([up_s[g, i * BLK_ROWS:(i + 1) * BLK_ROWS, lanes] for g in range(SCAN_G)], axis=0)
             for i in range(SCAN_R)], axis=1)
        uc_s[w] = lhs
        x_s[w] = _dot(lhs, bst_ref[w])

    row = lax.broadcasted_iota(jnp.int32, (SUBLANES, SLAB_LANES), 0)
    for s in range(N_SLABS):
        w, sg = s // 2, s % 2
        l_re = slice(sg * 2 * SLAB_LANES, sg * 2 * SLAB_LANES + SLAB_LANES)
        l_im = slice(sg * 2 * SLAB_LANES + SLAB_LANES, (sg + 1) * 2 * SLAB_LANES)
        ar = pwr_ref[s, 0]
        ai = pwi_ref[s, 0]
        m_r = pwr_ref[s, SCAN_M - 1]
        m_i = pwi_ref[s, SCAN_M - 1]
        for g in range(SCAN_G):
            base = g * BLK_ROWS
            hr = jnp.zeros((SUBLANES, SLAB_LANES), F32)
            hi = jnp.zeros((SUBLANES, SLAB_LANES), F32)
            for m in range(SCAN_M):
                rows = slice(base + m * SUBLANES, base + (m + 1) * SUBLANES)
                hr, hi = (ar * hr - ai * hi + x_s[w, rows, l_re], ar * hi + ai * hr + x_s[w, rows, l_im])
                if m < SCAN_M - 1:
                    nxt = slice(base + (m + 1) * SUBLANES, base + (m + 2) * SUBLANES)
                    hp_s[w, nxt, l_re] = hr
                    hp_s[w, nxt, l_im] = hi
            pr_ = st_s[s, 0]
            pi_ = st_s[s, 1]
            if g > 0:
                pr_ = jnp.where(is_ctx, 0.0, pr_)
                pi_ = jnp.where(is_ctx, 0.0, pi_)
            cr = jnp.where(row == 0, pltpu.roll(pr_, 1, 0), pltpu.roll(hr, 1, 0))
            ci = jnp.where(row == 0, pltpu.roll(pi_, 1, 0), pltpu.roll(hi, 1, 0))
            qr, qi = m_r, m_i
            for k in (1, 2, 4):
                sr = pltpu.roll(cr, k, 0)
                si = pltpu.roll(ci, k, 0)
                keep = row >= k
                cr, ci = (cr + jnp.where(keep, qr * sr - qi * si, 0.0),
                          ci + jnp.where(keep, qr * si + qi * sr, 0.0))
                qr, qi = qr * qr - qi * qi, 2.0 * qr * qi
            fr = m_r * cr - m_i * ci + hr
            fi = m_r * ci + m_i * cr + hi
            st_s[s, 0] = fr
            st_s[s, 1] = fi
            slot = tile_rows(g)[1]
            fin_ref[slot, s, 0] = fr[SUBLANES - 1:SUBLANES, :]
            fin_ref[slot, s, 1] = fi[SUBLANES - 1:SUBLANES, :]
            hp_s[w, base:base + SUBLANES, l_re] = cr
            hp_s[w, base:base + SUBLANES, l_im] = ci
            for m in range(SCAN_M - 1):
                rows = slice(base + (m + 1) * SUBLANES, base + (m + 2) * SUBLANES)
                pr = pwr_ref[s, m]
                pi = pwi_ref[s, m]
                hp_s[w, rows, l_re] = hp_s[w, rows, l_re] + pr * cr - pi * ci
                hp_s[w, rows, l_im] = hp_s[w, rows, l_im] + pr * ci + pi * cr

    yw = [_dot_nt(hp_s[w].astype(BF16), ckt_ref[w]) + _dot(uc_s[w], kst_ref[w]) for w in range(N_WIN)]
    pmt2 = pmt2_ref[...]
    for g in range(SCAN_G):
        rows = slice(g * BLK_ROWS, (g + 1) * BLK_ROWS)
        y = jnp.concatenate(
            [jnp.concatenate([yw[w][rows, i * LANES:(i + 1) * LANES] for w in range(N_WIN)], axis=1)
             for i in range(SCAN_R)], axis=0)
        y_hi = y.astype(BF16)
        y_lo = (y - y_hi.astype(F32)).astype(BF16)
        y_ref[tile_rows(g)[0], :] = _dot(pmt2, jnp.concatenate([y_hi, y_lo], axis=0))


def _ssm_super(d, j):
    jl = j - NS_CTX
    half = jl % NS_LAT_SEQ
    half = jnp.where(d == 1, NS_LAT_SEQ - 1 - half, half)
    return jnp.where(j < NS_CTX, j, NS_CTX + (jl // NS_LAT_SEQ) * NS_LAT_SEQ + half)


def _ssm_scan(l, u_ssm, pm, pmt2, bst, ckt, kst, pwr, pwi, h0):
    tab = lambda *shape: pl.BlockSpec((None,) + shape, lambda d, j: (d,) + (0,) * len(shape))
    ltab = lambda *shape: pl.BlockSpec((None, None) + shape, lambda d, j: (l, d) + (0,) * len(shape))
    return pl.pallas_call(
        _ssm_body,
        grid=(2, NS),
        in_specs=[
            pl.BlockSpec((SUPER, BRANCH_W), lambda d, j: (_ssm_super(d, j), 0)),
            tab(T, T), tab(T, 2 * T),
            ltab(N_WIN, SCAN_R * LANES, 4 * SLAB_LANES), ltab(N_WIN, SCAN_R * LANES, 4 * SLAB_LANES),
            ltab(N_WIN, SCAN_R * LANES, SCAN_R * LANES),
            ltab(N_SLABS, SCAN_M, SUBLANES, SLAB_LANES), ltab(N_SLABS, SCAN_M, SUBLANES, SLAB_LANES),
            pl.BlockSpec((None, None, None, N_SLABS, 2, 1, SLAB_LANES),
                         lambda d, j: (l, d, jnp.maximum(j - NS_CTX, 0) // NS_LAT_SEQ, 0, 0, 0, 0)),
        ],
        out_specs=[
            pl.BlockSpec((None, SUPER, BRANCH_W), lambda d, j: (d, _ssm_super(d, j), 0)),
            pl.BlockSpec((None, None, SCAN_G, N_SLABS, 2, 1, SLAB_LANES), lambda d, j: (d, j, 0, 0, 0, 0, 0)),
        ],
        out_shape=[jax.ShapeDtypeStruct((2, N_TOK, BRANCH_W), F32),
                   jax.ShapeDtypeStruct((2, NS, SCAN_G, N_SLABS, 2, 1, SLAB_LANES), F32)],
        scratch_shapes=[
            pltpu.VMEM((SCAN_G, T, BRANCH_W), BF16),
            pltpu.VMEM((N_WIN, SCAN_G * BLK_ROWS, SCAN_R * LANES), BF16),
            pltpu.VMEM((N_WIN, SCAN_G * BLK_ROWS, 4 * SLAB_LANES), F32),
            pltpu.VMEM((N_WIN, SCAN_G * BLK_ROWS, 4 * SLAB_LANES), F32),
            pltpu.VMEM((N_SLABS, 2, SUBLANES, SLAB_LANES), F32),
        ],
        compiler_params=_cparams(("arbitrary", "arbitrary")),
        name="ssm_scan",
    )(u_ssm, pm, pmt2, bst, ckt, kst, pwr, pwi, h0)


def _pre_body(n_x, *refs):
    (mod_ref, g_ref, wmain_ref, wkx_ref, wconv_ref, qag_ref, kvg_ref, wuq_ref, wuqs_ref, wuk_ref, wva_ref,
     wvb_ref, qng_ref, qngs_ref, kng_ref, kngs_ref, rc_ref, rsw_ref,
     ussm_ref, ufft_ref, q_ref, k_ref, va_ref, vb_ref, ckv_ref, kpe_ref, vc_ref, gb_ref) = refs[n_x:]
    i = pl.program_id(0)
    x = _load_stream(i, refs[:n_x])
    sh = mod_ref[:, 0:D_MODEL]
    sc = mod_ref[:, D_MODEL:2 * D_MODEL]
    xn = (_rms(x, g_ref[...]) * (1.0 + sc) + sh).astype(BF16)

    def proj(lo, hi):
        for ref, base, end in ((wmain_ref, Z_SSM, Z_KPE), (wkx_ref, Z_KPE, Z_HIN), (wconv_ref, Z_HIN, Z_COLS)):
            if base <= lo and hi <= end:
                return _dot(xn, ref[:, lo - base:hi - base])
        raise ValueError((lo, hi))

    zk = proj(Z_KPE, Z_KROT)
    krot = proj(Z_KROT, Z_KSW)
    kswp = proj(Z_KSW, Z_HIN)

    rc = rc_ref[...]
    rsw = rsw_ref[...]
    q_gc = qng_ref[...] * rc * ATTN_SCALE
    q_gs = qngs_ref[...] * rsw * ATTN_SCALE
    k_gc = kng_ref[...] * rc
    k_sw = kswp * (kngs_ref[...] * rsw)

    def rstd(v):
        return lax.rsqrt(jnp.sum(v * v, axis=-1, keepdims=True) * (1.0 / QK_DIM) + EPS)

    cqn = _rms(proj(Z_CQ, Z_CKV), qag_ref[...]).astype(BF16)
    qf = _dot(cqn, wuq_ref[...])
    qs = _dot(cqn, wuqs_ref[...])
    ckvn = _rms(proj(Z_CKV, Z_KPE), kvg_ref[...])
    ckvb = ckvn.astype(BF16)
    kf = _dot(ckvb, wuk_ref[...])
    va_ref[...] = (_dot(ckvb, wva_ref[...]) + _ones_column(V_DIM)).astype(BF16)
    vb_ref[...] = (_dot(ckvb, wvb_ref[...]) + _ones_column(0)).astype(BF16)
    for h in range(MLA_HEADS):
        lanes = slice(h * HEAD_PAD, (h + 1) * HEAD_PAD)
        qh = qf[:, lanes]
        q_ref[:, lanes] = ((qh * q_gc + qs[:, lanes] * q_gs) * rstd(qh)).astype(BF16)
        kh = kf[:, lanes] + krot
        k_ref[:, lanes] = ((kh * k_gc + k_sw) * rstd(kh)).astype(BF16)

    ussm_ref[...] = proj(Z_SSM, Z_FFT)
    ufft_ref[...] = proj(Z_FFT, Z_CQ).astype(BF16)
    h_in = proj(Z_HIN, Z_GB)
    gb_ref[...] = proj(Z_GB, Z_GC)
    vc_ref[...] = proj(Z_GC, Z_COLS) * h_in

    @pl.when(i < NTT_CTX)
    def _():
        ckv_ref[...] = ckvn
        kpe_ref[...] = zk[:, 0:ROPE_DIM]


def _pre(l, x, mod, p, rc, rsw):
    tile = _tok_tile_spec
    rope_blk = lambda i: (jnp.where(i < NTT_CTX, NTT_LAT_SEQ, i % NTT_LAT_SEQ), 0)
    rspec = pl.BlockSpec((TT, LANES), rope_blk)
    out = lambda width, dt: jax.ShapeDtypeStruct((N_TOK, width), dt)
    names = ('norm_mix_g', 'w_main', 'w_kx', 'w_conv', 'q_a_norm_g', 'kv_a_norm_g', 'wuq', 'wuqs', 'wuk', 'wva', 'wvb',
             'qng', 'qngs', 'kng', 'kngs')
    x_specs, x_args = _stream_specs(x, D_MODEL)
    return pl.pallas_call(
        functools.partial(_pre_body, len(x_args)),
        grid=(NTT,),
        in_specs=x_specs + [_mod_spec(l)]
        + [_layer_spec(l, p[n].shape[1:]) for n in names] + [rspec, rspec],
        out_specs=[tile(BRANCH_W), tile(BRANCH_W), tile(MLA_HEADS * HEAD_PAD), tile(MLA_HEADS * HEAD_PAD),
                   tile(BRANCH_W), tile(BRANCH_W),
                   _ctx_tile_spec(KV_RANK), _ctx_tile_spec(ROPE_DIM), tile(BRANCH_W), tile(BRANCH_W)],
        out_shape=[out(BRANCH_W, F32), out(BRANCH_W, BF16), out(MLA_HEADS * HEAD_PAD, BF16),
                   out(MLA_HEADS * HEAD_PAD, BF16), out(BRANCH_W, BF16), out(BRANCH_W, BF16),
                   jax.ShapeDtypeStruct((N_CTX, KV_RANK), F32), jax.ShapeDtypeStruct((N_CTX, ROPE_DIM), F32),
                   out(BRANCH_W, F32), out(BRANCH_W, F32)],
        compiler_params=_cparams(("arbitrary",)),
        name="pre_mixer",
    )(*x_args, mod, *[p[n] for n in names], rc, rsw)


def _kvc_body(ckv_ref, kr_ref, wuk_ref, wva_ref, wvb_ref, kng_ref, k_ref, va_ref, vb_ref):
    ckvb = ckv_ref[...].astype(BF16)
    kf = _dot(ckvb, wuk_ref[...])
    va_ref[...] = (_dot(ckvb, wva_ref[...]) + _ones_column(V_DIM)).astype(BF16)
    vb_ref[...] = (_dot(ckvb, wvb_ref[...]) + _ones_column(0)).astype(BF16)
    krot = kr_ref[...]
    g = kng_ref[...]
    for h in range(MLA_HEADS):
        lanes = slice(h * HEAD_PAD, (h + 1) * HEAD_PAD)
        kh = kf[:, lanes] + krot
        ss = jnp.sum(kh * kh, axis=-1, keepdims=True)
        k_ref[:, lanes] = (kh * lax.rsqrt(ss * (1.0 / QK_DIM) + EPS) * g).astype(BF16)


def _cache_kv(cache_ckv, cache_kr, wuk, wva, wvb, kng):
    nt = PAST_LEN // T
    wspec = lambda r, c: pl.BlockSpec((None, r, c), lambda l, b, t: (l, 0, 0))
    ospec = lambda c: pl.BlockSpec((None, None, T, c), lambda l, b, t: (l, b, t, 0))
    return pl.pallas_call(
        _kvc_body,
        grid=(DEPTH, DEC_BATCH, nt),
        in_specs=[
            pl.BlockSpec((None, None, T, KV_RANK), lambda l, b, t: (b, l, t, 0)),
            pl.BlockSpec((None, None, T, HEAD_PAD), lambda l, b, t: (b, l, t, 0)),
            wspec(KV_RANK, MLA_HEADS * HEAD_PAD), wspec(KV_RANK, BRANCH_W), wspec(KV_RANK, BRANCH_W),
            wspec(1, HEAD_PAD),
        ],
        out_specs=[ospec(MLA_HEADS * HEAD_PAD), ospec(BRANCH_W), ospec(BRANCH_W)],
        out_shape=[jax.ShapeDtypeStruct((DEPTH, DEC_BATCH, PAST_LEN, MLA_HEADS * HEAD_PAD), BF16),
                   jax.ShapeDtypeStruct((DEPTH, DEC_BATCH, PAST_LEN, BRANCH_W), BF16),
                   jax.ShapeDtypeStruct((DEPTH, DEC_BATCH, PAST_LEN, BRANCH_W), BF16)],
        compiler_params=_cparams(("parallel", "parallel", "parallel")),
        name="cache_kv",
    )(cache_ckv, cache_kr, wuk, wva, wvb, kng)


def _fnet_body(seq_len, u_ref, csl_ref, csc_ref, y_ref, v1_s, v2_s):
    r = pl.program_id(1)

    @pl.when(r == 0)
    def _():
        csc = csc_ref[...]
        for c in range(seq_len // T):
            rows = slice(c * T, (c + 1) * T)
            for g in range(FFT_GROUPS):
                lanes = slice(g * FFT_GW, (g + 1) * FFT_GW)
                v = _dot(u_ref[rows, lanes], csc)
                v1_s[rows, lanes] = v[:, :FFT_GW].astype(BF16)
                v2_s[rows, lanes] = v[:, FFT_GW:].astype(BF16)

    y = _dot(csl_ref[:, :seq_len], v1_s[...]) + _dot(csl_ref[:, seq_len:], v2_s[...])
    y_ref[...] = y.astype(BF16)


def _fnet(u_fft, csl, csc, seq_len, n_seq, tile0):
    nr = seq_len // T
    blk0 = tile0 * T // seq_len
    return pl.pallas_call(
        functools.partial(_fnet_body, seq_len),
        grid=(n_seq, nr),
        in_specs=[
            pl.BlockSpec((seq_len, BRANCH_W), lambda b, r: (blk0 + b, 0)),
            pl.BlockSpec((T, 2 * seq_len), lambda b, r: (r, 0)),
            _const_spec((FFT_GW, 2 * FFT_GW)),
        ],
        out_specs=pl.BlockSpec((T, BRANCH_W), lambda b, r: (b * nr + r, 0)),
        out_shape=jax.ShapeDtypeStruct((n_seq * seq_len, BRANCH_W), BF16),
        scratch_shapes=[pltpu.VMEM((seq_len, BRANCH_W), BF16), pltpu.VMEM((seq_len, BRANCH_W), BF16)],
        compiler_params=_cparams(("arbitrary", "arbitrary")),
        name="fnet_%d" % seq_len,
    )(u_fft, csl, csc)


def _dft_tables(n, split):
    m = jnp.arange(n, dtype=jnp.int32)
    a = jnp.arange(n // split, dtype=jnp.int32) * split
    b = jnp.arange(split, dtype=jnp.int32)
    w = 2.0 * math.pi / n
    ang_a = ((a[:, None] * m[None, :]) % n).astype(F32) * w
    ang_b = ((b[:, None] * m[None, :]) % n).astype(F32) * w
    ca, sa = jnp.cos(ang_a)[:, None, :], jnp.sin(ang_a)[:, None, :]
    cb, sb = jnp.cos(ang_b)[None, :, :], jnp.sin(ang_b)[None, :, :]
    s = 1.0 / math.sqrt(n)
    cos = ((ca * cb - sa * sb) * s).reshape(n, n)
    sin = ((sa * cb + ca * sb) * s).reshape(n, n)
    return cos, sin


def _attn_body(n_seg, q_ref, *refs):
    k_refs = refs[0:n_seg]
    va_refs = refs[n_seg:2 * n_seg]
    vb_refs = refs[2 * n_seg:3 * n_seg]
    o_ref = refs[-1]
    lane = lax.broadcasted_iota(jnp.int32, (T, LANES), 1)
    for hp in range(MLA_HEADS // 2):
        out_lanes = slice(hp * LANES, (hp + 1) * LANES)
        halves = []
        for h, v_refs, sum_lane in ((2 * hp, va_refs, V_DIM), (2 * hp + 1, vb_refs, 0)):
            lanes = slice(h * HEAD_PAD, (h + 1) * HEAD_PAD)
            qh = q_ref[:, lanes]
            ss = [_dot_nt(qh, kr[:, lanes]) for kr in k_refs]
            m = functools.reduce(jnp.maximum, [jnp.max(s, axis=-1, keepdims=True) for s in ss])
            o = functools.reduce(lambda a, b: a + b,
                                 [_dot(jnp.exp2(s - m).astype(BF16), vr[:, out_lanes]) for s, vr in zip(ss, v_refs)])
            l = jnp.sum(jnp.where(lane == sum_lane, o, 0.0), axis=-1, keepdims=True)
            halves.append(o / l)
        o_ref[:, out_lanes] = jnp.where(lane < V_DIM, halves[0], halves[1]).astype(BF16)


def _attention(q, segs, n_seq, nq, tile0):
    in_specs = [pl.BlockSpec((T, MLA_HEADS * HEAD_PAD), lambda b, r: (tile0 + b * nq + r, 0))]
    args = [q]
    for idx in range(3):
        for seg in segs:
            arr = seg[idx]
            width = arr.shape[-1]
            in_specs.append(pl.BlockSpec(seg[3] + (width,), seg[4]))
            args.append(arr)
    return pl.pallas_call(
        functools.partial(_attn_body, len(segs)),
        grid=(n_seq, nq),
        in_specs=in_specs,
        out_specs=pl.BlockSpec((T, BRANCH_W), lambda b, r: (b * nq + r, 0)),
        out_shape=jax.ShapeDtypeStruct((n_seq * nq * T, BRANCH_W), BF16),
        compiler_params=_cparams(("arbitrary", "arbitrary")),
        name="attention_%d" % nq,
    )(*args)


def _mix_body(n_x, *refs):
    (mod_ref, g_ref, yf_ref, yb_ref, u_ref, dsk_ref, wglu_ref, fc_ref, fl_ref,
     ac_ref, al_ref, vc_ref, vp_ref, vn_ref, cw_ref, gb_ref, wbr_ref, wg_ref, bg_ref, wo_ref, o_ref) = refs[n_x:]
    i = pl.program_id(0)
    x = _load_stream(i, refs[:n_x])
    sh = mod_ref[:, 0:D_MODEL]
    sc = mod_ref[:, D_MODEL:2 * D_MODEL]
    gm = mod_ref[:, 2 * D_MODEL:3 * D_MODEL]
    xn = (_rms(x, g_ref[...]) * (1.0 + sc) + sh).astype(BF16)

    y = jax.nn.gelu(yf_ref[...] + yb_ref[...] + u_ref[...] * dsk_ref[...], approximate=True)
    y = y * _sigmoid(_dot(y.astype(BF16), wglu_ref[...]))

    v = vc_ref[...]
    row = lax.broadcasted_iota(jnp.int32, (TT, BRANCH_W), 0)
    seq_len = jnp.where(i < NTT_CTX, SEQ, DEC_SEQ)
    pos = jnp.bitwise_and(i * TT + row, seq_len - 1)
    v_prev = jnp.where(row == 0, vp_ref[SUBLANES - 1:SUBLANES, :], pltpu.roll(v, 1, 0))
    v_next = jnp.where(row == TT - 1, vn_ref[0:1, :], pltpu.roll(v, TT - 1, 0))
    v_prev = jnp.where(pos == 0, 0.0, v_prev)
    v_next = jnp.where(pos == seq_len - 1, 0.0, v_next)
    y_conv = gb_ref[...] * (cw_ref[0:1, :] * v_prev + cw_ref[1:2, :] * v + cw_ref[2:3, :] * v_next)

    branches = (y.astype(BF16), _load_pair(i, fc_ref, fl_ref), _load_pair(i, ac_ref, al_ref),
                y_conv.astype(BF16))
    merged = jnp.zeros((TT, D_MODEL), F32)
    for k in range(N_BRANCH):
        cols = slice(k * D_MODEL, (k + 1) * D_MODEL)
        gate = _sigmoid(_dot(xn, wg_ref[:, cols]) + bg_ref[:, cols])
        merged = merged + gate * _dot(branches[k], wbr_ref[k])
    out = _dot(merged.astype(BF16), wo_ref[...])
    o_ref[...] = x + gm * out


def _mix(l, x, mod, p, y_ssm, u_ssm, fft_c, fft_l, att_c, att_l, vconv, g_b):
    tile = _tok_tile_spec
    rb = TT // SUBLANES
    n_rb = N_TOK // SUBLANES
    lspec = lambda n: _layer_spec(l, p[n].shape[1:])
    x_specs, x_args = _stream_specs(x, D_MODEL)
    return pl.pallas_call(
        functools.partial(_mix_body, len(x_args)),
        grid=(NTT,),
        in_specs=x_specs + [
            _mod_spec(l), lspec('norm_mix_g'),
            pl.BlockSpec((None, TT, BRANCH_W), lambda i: (0, i, 0)),
            pl.BlockSpec((None, TT, BRANCH_W), lambda i: (1, i, 0)),
            tile(BRANCH_W), lspec('ssm_d'), lspec('w_glu'),
            _ctx_tile_spec(BRANCH_W), _lat_tile_spec(BRANCH_W), _ctx_tile_spec(BRANCH_W), _lat_tile_spec(BRANCH_W),
            tile(BRANCH_W),
            pl.BlockSpec((SUBLANES, BRANCH_W), lambda i: (jnp.maximum(i * rb - 1, 0), 0)),
            pl.BlockSpec((SUBLANES, BRANCH_W), lambda i: (jnp.minimum(i * rb + rb, n_rb - 1), 0)),
            lspec('conv_w'), tile(BRANCH_W),
            lspec('w_branch'), lspec('w_gate'), lspec('b_gate'), lspec('w_out'),
        ],
        out_specs=tile(D_MODEL),
        out_shape=jax.ShapeDtypeStruct((N_TOK, D_MODEL), F32),
        compiler_params=_cparams(("arbitrary",)),
        name="mixer_merge",
    )(*x_args, mod, p['norm_mix_g'], y_ssm, y_ssm, u_ssm, p['ssm_d'], p['w_glu'], fft_c, fft_l, att_c, att_l,
      vconv, vconv, vconv, p['conv_w'], g_b, p['w_branch'], p['w_gate'], p['b_gate'], p['w_out'])


FF_CHUNK = D_FF // 2


def _ffn_body(n_out, x_ref, mod_ref, g_ref, wi_ref, wo_ref, *o_refs):
    i = pl.program_id(0)
    x = x_ref[...]
    sh = mod_ref[:, 3 * D_MODEL:4 * D_MODEL]
    sc = mod_ref[:, 4 * D_MODEL:5 * D_MODEL]
    gf = mod_ref[:, 5 * D_MODEL:6 * D_MODEL]
    xn = (_rms(x, g_ref[...]) * (1.0 + sc) + sh).astype(BF16)
    acc = jnp.zeros((TT, D_MODEL), F32)
    for c in range(D_FF // FF_CHUNK):
        lo = c * FF_CHUNK
        g = _dot(xn, wi_ref[:, lo:lo + FF_CHUNK])
        u = _dot(xn, wi_ref[:, D_FF + lo:D_FF + lo + FF_CHUNK])
        h = (g * _sigmoid(g) * u).astype(BF16)
        acc = acc + _dot(h, wo_ref[lo:lo + FF_CHUNK, :])
    _store_stream(i, o_refs[:n_out], x + gf * acc)


def _ffn(l, x, mod, p, split_out):
    lspec = lambda n: _layer_spec(l, p[n].shape[1:])
    tile = _tok_tile_spec(D_MODEL)
    if split_out:
        out_specs = [_ctx_tile_spec(D_MODEL), _lat_tile_spec(D_MODEL)]
        out_shape = [jax.ShapeDtypeStruct((N_CTX, D_MODEL), F32), jax.ShapeDtypeStruct((N_LAT, D_MODEL), F32)]
    else:
        out_specs = [tile]
        out_shape = [jax.ShapeDtypeStruct((N_TOK, D_MODEL), F32)]
    return pl.pallas_call(
        functools.partial(_ffn_body, len(out_specs)),
        grid=(NTT,),
        in_specs=[tile, _mod_spec(l), lspec('norm_ffn_g'), lspec('w_ffn_in'), lspec('w_ffn_out')],
        out_specs=out_specs,
        out_shape=out_shape,
        compiler_params=_cparams(("arbitrary",)),
        name="ffn",
    )(x, mod, p['norm_ffn_g'], p['w_ffn_in'], p['w_ffn_out'])


def _prep_weights(norm_mix_g, norm_ffn_g, w_in, q_a_norm_g, kv_a_norm_g, w_uq, w_ukv, q_norm_g, k_norm_g,
                  ssm_d, w_glu, conv_w, w_branch, w_gate, b_gate, w_out, w_ffn_in, w_ffn_out):
    vec = lambda a: a.reshape(DEPTH, 1, a.shape[-1])
    kpe = w_in[:, :, OFF_KPE:OFF_KPE + ROPE_DIM]
    z = lambda n: jnp.zeros((DEPTH, D_MODEL, n), F32)
    w_main = w_in[:, :, OFF_SSM:OFF_KPE].astype(BF16)
    w_kx = jnp.concatenate([
        kpe, z(LANES - ROPE_DIM),
        z(NOPE_DIM), kpe[:, :, ROPE_PERM], z(LANES - QK_DIM),
        z(NOPE_DIM), kpe[:, :, ROPE_PARTNER], z(LANES - QK_DIM),
    ], axis=2).astype(BF16)
    w_conv = w_in[:, :, OFF_CONV:OFF_CONV + 3 * BRANCH_W].astype(BF16)
    wq = w_uq.reshape(DEPTH, Q_RANK, MLA_HEADS, QK_DIM)
    head_pad = lambda a: jnp.pad(a, ((0, 0), (0, 0), (0, 0), (0, HEAD_PAD - a.shape[-1]))).reshape(
        DEPTH, a.shape[1], MLA_HEADS * HEAD_PAD).astype(BF16)
    wuq = head_pad(jnp.concatenate([wq[..., :NOPE_DIM], wq[..., NOPE_DIM:][..., ROPE_PERM]], axis=3))
    wuqs = head_pad(jnp.concatenate([jnp.zeros_like(wq[..., :NOPE_DIM]), wq[..., NOPE_DIM:][..., ROPE_PARTNER]],
                                    axis=3))
    wkv = w_ukv.reshape(DEPTH, KV_RANK, MLA_HEADS, NOPE_DIM + V_DIM)
    wuk = head_pad(wkv[..., :NOPE_DIM])
    wv = wkv[..., NOPE_DIM:]
    even = (jnp.arange(MLA_HEADS) % 2 == 0)[None, None, :, None]
    wva = jnp.where(even, wv, 0.0).reshape(DEPTH, KV_RANK, BRANCH_W).astype(BF16)
    wvb = jnp.where(even, 0.0, wv).reshape(DEPTH, KV_RANK, BRANCH_W).astype(BF16)

    def gain(g, perm):
        gp = jnp.concatenate([g[:, :NOPE_DIM], g[:, NOPE_DIM:][:, perm],
                              jnp.zeros((DEPTH, HEAD_PAD - QK_DIM), F32)], axis=1)
        return gp.reshape(DEPTH, 1, HEAD_PAD)

    return dict(
        norm_mix_g=vec(norm_mix_g), norm_ffn_g=vec(norm_ffn_g), w_main=w_main, w_kx=w_kx, w_conv=w_conv,
        q_a_norm_g=vec(q_a_norm_g), kv_a_norm_g=vec(kv_a_norm_g),
        wuq=wuq, wuqs=wuqs, wuk=wuk, wva=wva, wvb=wvb,
        qng=gain(q_norm_g, ROPE_PERM), qngs=gain(q_norm_g, ROPE_PARTNER),
        kng=gain(k_norm_g, ROPE_PERM), kngs=gain(k_norm_g, ROPE_PARTNER),
        ssm_d=vec(ssm_d), w_glu=w_glu.astype(BF16),
        conv_w=jnp.pad(conv_w, ((0, 0), (0, SUBLANES - 3), (0, 0))),
        w_branch=w_branch.astype(BF16), w_gate=w_gate.astype(BF16), b_gate=vec(b_gate),
        w_out=w_out.astype(BF16), w_ffn_in=w_ffn_in.astype(BF16), w_ffn_out=w_ffn_out.astype(BF16),
    )


def _rope_tables():
    rows = DEC_SEQ // GRID_W
    r = jnp.repeat(jnp.arange(rows, dtype=F32), GRID_W)
    c = jnp.tile(jnp.arange(GRID_W, dtype=F32), rows)
    n_freq = ROPE_DIM // 4
    inv = ROPE_THETA ** (-jnp.arange(n_freq, dtype=F32) / n_freq)
    ang = jnp.concatenate([r[:, None] * inv, c[:, None] * inv], axis=-1)
    cos, sin = jnp.cos(ang), jnp.sin(ang)
    ones = jnp.ones((DEC_SEQ, NOPE_DIM), F32)
    zeros = jnp.zeros((DEC_SEQ, NOPE_DIM), F32)
    tail1 = jnp.ones((DEC_SEQ, HEAD_PAD - QK_DIM), F32)
    tail0 = jnp.zeros((DEC_SEQ, HEAD_PAD - QK_DIM), F32)
    rc = jnp.concatenate([ones, cos, cos, tail1], axis=1)
    rsw = jnp.concatenate([zeros, -sin, sin, tail0], axis=1)
    ident = lambda v: jnp.full((TT, HEAD_PAD), v, F32)
    return jnp.concatenate([rc, ident(1.0)], axis=0), jnp.concatenate([rsw, ident(0.0)], axis=0)


def kernel(x_prompt, x_sample, cache_ckv, cache_kpe, state_ssm, c, c_ctx, norm_mix_g, norm_ffn_g, w_ada, b_ada,
           w_in, q_a_norm_g, kv_a_norm_g, w_uq, w_ukv, q_norm_g, k_norm_g, ssm_lam_re, ssm_lam_im, ssm_log_dt,
           ssm_b_re, ssm_b_im, ssm_c_re, ssm_c_im, ssm_d, w_glu, conv_w, w_branch, w_gate, b_gate, w_out,
           w_ffn_in, w_ffn_out):
    xp = x_prompt.reshape(N_CTX, D_MODEL)
    xs = x_sample.reshape(N_LAT, D_MODEL)

    cond8 = jnp.concatenate([c_ctx[None, :], c, jnp.zeros((SUBLANES - 1 - DEC_BATCH, D_MODEL), F32)], axis=0)
    mod = _modulation(cond8, w_ada, b_ada).reshape(DEPTH, SUBLANES, 1, 6 * D_MODEL)

    bst, ckt, kst, pwr, pwi = _ssm_prep(ssm_lam_re, ssm_lam_im, ssm_log_dt, ssm_b_re, ssm_b_im,
                                        ssm_c_re, ssm_c_im)
    pm, pmt2 = _scan_perms()
    st = state_ssm.transpose(1, 2, 0, 5, 3, 4).reshape(DEPTH, 2, DEC_BATCH, 2, N_SLABS, 1, SLAB_LANES)
    h0_all = st.transpose(0, 1, 2, 4, 3, 5, 6)

    rc, rsw = _rope_tables()
    cl, sl = _dft_tables(SEQ, 16)
    csl_ctx = jnp.concatenate([cl, -sl], axis=1).astype(BF16)
    cl, sl = _dft_tables(DEC_SEQ, 64)
    csl_lat = jnp.concatenate([cl, -sl], axis=1).astype(BF16)
    cc, sc = _dft_tables(FFT_GW, 8)
    csc = jnp.concatenate([cc, sc], axis=1).astype(BF16)

    p = _prep_weights(norm_mix_g, norm_ffn_g, w_in, q_a_norm_g, kv_a_norm_g, w_uq, w_ukv, q_norm_g, k_norm_g,
                      ssm_d, w_glu, conv_w, w_branch, w_gate, b_gate, w_out, w_ffn_in, w_ffn_out)
    cache_kr = jnp.pad(cache_kpe[..., ROPE_PERM], ((0, 0), (0, 0), (0, 0), (NOPE_DIM, HEAD_PAD - QK_DIM)))
    kc, vca, vcb = _cache_kv(cache_ckv, cache_kr, p['wuk'], p['wva'], p['wvb'], p['kng'])

    lat_blk = N_CTX // DEC_SEQ
    ckv_list, kpe_list, fin_list = [], [], []
    x = (xp, xs)
    for l in range(DEPTH):
        (u_ssm, u_fft, q, k, va, vb, ckv_n, kpe, vconv, g_b) = _pre(l, x, mod, p, rc, rsw)
        ckv_list.append(ckv_n.reshape(BATCH, SEQ, KV_RANK))
        kpe_list.append(kpe.reshape(BATCH, SEQ, ROPE_DIM))

        y_ssm, fin = _ssm_scan(l, u_ssm, pm, pmt2, bst, ckt, kst, pwr, pwi, h0_all)
        fin_list.append(fin[:, :NS_CTX].reshape(2, BATCH, N_SLABS, 2, SLAB_LANES))

        fft_c = _fnet(u_fft, csl_ctx, csc, SEQ, BATCH, 0)
        fft_l = _fnet(u_fft, csl_lat, csc, DEC_SEQ, DEC_BATCH, NT_CTX)

        ctx_seg = (k, va, vb, (T,), lambda b, r: (b, 0))
        att_c = _attention(q, [ctx_seg], BATCH, 1, 0)
        lat_seg = (k, va, vb, (DEC_SEQ,), lambda b, r: (lat_blk + b, 0))
        cache_seg = (kc, vca, vcb, (None, None, PAST_LEN), functools.partial(lambda l_, b, r: (l_, b, 0, 0), l))
        att_l = _attention(q, [lat_seg, cache_seg], DEC_BATCH, NT_LAT_SEQ, NT_CTX)

        x = _mix(l, x, mod, p, y_ssm, u_ssm, fft_c, fft_l, att_c, att_l, vconv, g_b)
        x = _ffn(l, x, mod, p, split_out=(l == DEPTH - 1))
        x = tuple(x) if l == DEPTH - 1 else x[0]

    y_p = x[0].reshape(BATCH, SEQ, D_MODEL)
    y_s = x[1].reshape(DEC_BATCH, DEC_SEQ, D_MODEL)
    new_ckv = jnp.stack(ckv_list, axis=1)
    new_kpe = jnp.stack(kpe_list, axis=1)
    fin = jnp.stack(fin_list, axis=0)
    fin = fin.reshape(DEPTH, 2, BATCH, N_SLABS, 2, 4, SSM_STATE).transpose(2, 0, 1, 3, 5, 6, 4)
    new_ssm = fin.reshape(BATCH, DEPTH, 2, SSM_GROUPS, SSM_STATE, 2)
    return (y_p, y_s, new_ckv, new_kpe, new_ssm)
```

```python
import functools
import math

import numpy as np
import jax
import jax.numpy as jnp
from jax import lax
from jax.experimental import pallas as pl
from jax.experimental.pallas import tpu as pltpu

F32 = jnp.float32
BF16 = jnp.bfloat16

D_MODEL = 1024
BATCH = 32
SEQ = 256
DEPTH = 4
DEC_BATCH = 2
DEC_SEQ = 2048
PAST_LEN = 512
GRID_W = 64
N_BRANCH = 4
BRANCH_W = 512
SSM_GROUP = 16
SSM_GROUPS = 32
SSM_STATE = 64
FFT_GROUPS = 4
FFT_GW = 128
MLA_HEADS = 8
NOPE_DIM = 64
ROPE_DIM = 32
V_DIM = 64
QK_DIM = 96
Q_RANK = 384
KV_RANK = 256
ROPE_THETA = 10000.0
D_FF = 2816
EPS = 1e-6

OFF_SSM = 0
OFF_FFT = 512
OFF_CQ = 1024
OFF_CKV = 1408
OFF_KPE = 1664
OFF_CONV = 1696

LANES = 128
SUBLANES = 8
VMEM_LIMIT_BYTES = 56 * 1024 * 1024

T = 256
N_CTX = BATCH * SEQ
N_LAT = DEC_BATCH * DEC_SEQ
N_TOK = N_CTX + N_LAT
NT_CTX = N_CTX // T
NT_LAT_SEQ = DEC_SEQ // T
NT = N_TOK // T
TT = 512
TQ_LAT = 512
NTT_CTX = N_CTX // TT
NTT_LAT_SEQ = DEC_SEQ // TT
NTT = N_TOK // TT
N_SEQ = BATCH + DEC_BATCH
HEAD_PAD = LANES
ATTN_SCALE = QK_DIM ** -0.5 * math.log2(math.e)


def _ones_column(offset):
    lane = lax.broadcasted_iota(jnp.int32, (1, BRANCH_W), 1)
    return (lane % LANES == offset).astype(F32)

Z_SSM = 0
Z_FFT = 512
Z_CQ = 1024
Z_CKV = 1408
Z_KPE = 1664
Z_KROT = 1792
Z_KSW = 1920
Z_HIN = 2048
Z_GB = 2560
Z_GC = 3072
Z_COLS = 3584

SCAN_R = 4
SCAN_M = T // SUBLANES // SCAN_R
BLK_ROWS = T // SCAN_R
SCAN_G = 4
SUPER = SCAN_G * T
NS_CTX = N_CTX // SUPER
NS_LAT_SEQ = DEC_SEQ // SUPER
NS = N_TOK // SUPER
SLAB_LANES = 256
N_SLABS = SSM_GROUPS * SSM_STATE // SLAB_LANES
N_WIN = BRANCH_W // LANES
GC = SSM_GROUPS * SSM_GROUP
ROPE_PERM = np.concatenate([np.arange(0, ROPE_DIM, 2), np.arange(1, ROPE_DIM, 2)])
ROPE_PARTNER = np.concatenate([np.arange(1, ROPE_DIM, 2), np.arange(0, ROPE_DIM, 2)])


def _cparams(sem):
    return pltpu.CompilerParams(dimension_semantics=sem, vmem_limit_bytes=VMEM_LIMIT_BYTES)


def _const_spec(shape):
    nd = len(shape)
    return pl.BlockSpec(shape, lambda *_: (0,) * nd, pipeline_mode=pl.Buffered(1))


def _layer_spec(l, shape):
    nd = len(shape)
    return pl.BlockSpec((None,) + tuple(shape), lambda *_: (l,) + (0,) * nd, pipeline_mode=pl.Buffered(1))


def _mod_spec(l):
    return pl.BlockSpec((None, None, 1, 6 * D_MODEL), lambda i: (l, _cond_row(i), 0, 0))


def _ctx_tile_spec(width):
    return pl.BlockSpec((TT, width), lambda i: (jnp.minimum(i, NTT_CTX - 1), 0))


def _lat_tile_spec(width):
    return pl.BlockSpec((TT, width), lambda i: (jnp.maximum(i - NTT_CTX, 0), 0))


def _tok_tile_spec(width):
    return pl.BlockSpec((TT, width), lambda i: (i, 0))


def _load_pair(i, ctx_ref, lat_ref):
    return jnp.where(i < NTT_CTX, ctx_ref[...], lat_ref[...])


def _stream_specs(x, width):
    if isinstance(x, tuple):
        return [_ctx_tile_spec(width), _lat_tile_spec(width)], list(x)
    return [_tok_tile_spec(width)], [x]


def _load_stream(i, refs):
    return refs[0][...] if len(refs) == 1 else _load_pair(i, *refs)


def _store_stream(i, refs, val):
    if len(refs) == 1:
        refs[0][...] = val
        return
    ctx_ref, lat_ref = refs

    @pl.when(i < NTT_CTX)
    def _():
        ctx_ref[...] = val

    lat_ref[...] = val


def _dot(a, b):
    return jnp.dot(a, b, preferred_element_type=F32)


def _dot_nt(a, b):
    return lax.dot_general(a, b, (((1,), (1,)), ((), ())), preferred_element_type=F32)


def _sigmoid(x):
    return 1.0 / (1.0 + jnp.exp(-x))


def _rms(x, g):
    return x * lax.rsqrt(jnp.mean(x * x, axis=-1, keepdims=True) + EPS) * g


def _cond_row(i):
    return jnp.maximum(i - (NTT_CTX - NTT_LAT_SEQ), 0) // NTT_LAT_SEQ


def _mod_body(c_ref, w_ref, b_ref, o_ref):
    c = c_ref[...]
    s = c * _sigmoid(c)
    w = w_ref[...]
    s_hi = s.astype(BF16)
    s_lo = (s - s_hi.astype(F32)).astype(BF16)
    w_hi = w.astype(BF16)
    w_lo = (w - w_hi.astype(F32)).astype(BF16)
    o_ref[...] = _dot(s_hi, w_hi) + _dot(s_lo, w_hi) + _dot(s_hi, w_lo) + b_ref[...]


def _modulation(cond8, w_ada, b_ada):
    n_col = 6 * D_MODEL // 1024
    return pl.pallas_call(
        _mod_body,
        grid=(DEPTH, n_col),
        in_specs=[
            pl.BlockSpec((SUBLANES, D_MODEL), lambda l, j: (0, 0)),
            pl.BlockSpec((None, D_MODEL, 1024), lambda l, j: (l, 0, j)),
            pl.BlockSpec((None, 1, 1024), lambda l, j: (l, 0, j)),
        ],
        out_specs=pl.BlockSpec((None, SUBLANES, 1024), lambda l, j: (l, 0, j)),
        out_shape=jax.ShapeDtypeStruct((DEPTH, SUBLANES, 6 * D_MODEL), F32),
        compiler_params=_cparams(("parallel", "parallel")),
        name="modulation",
    )(cond8, w_ada, b_ada.reshape(DEPTH, 1, 6 * D_MODEL))


def _dot_nt3(a, b):
    a_hi = a.astype(BF16)
    a_lo = (a - a_hi.astype(F32)).astype(BF16)
    b_hi = b.astype(BF16)
    b_lo = (b - b_hi.astype(F32)).astype(BF16)
    return _dot_nt(a_hi, b_hi) + _dot_nt(a_lo, b_hi) + _dot_nt(a_hi, b_lo)


def _ssm_prep_body(lre_ref, lim_ref, ldt_ref, bre_ref, bim_ref, cre_ref, cim_ref, lre_r, lim_r, ldt_r,
                   ere_ref, eim_ref, bst_ref, ckt_ref, kst_ref, pwr_ref, pwi_ref):
    lam_re = lre_ref[...]
    lam_im = lim_ref[...]
    dt = jnp.exp(ldt_ref[...])
    xr = lam_re * dt
    xi = lam_im * dt

    mag = jnp.exp(xr)
    a_re = mag * jnp.cos(xi)
    a_im = mag * jnp.sin(xi)
    pw = [(jnp.ones_like(a_re), jnp.zeros_like(a_re)), (a_re, a_im)]
    for _ in range(SCAN_R - 1):
        pr, pi = pw[-1]
        pw.append((pr * a_re - pi * a_im, pr * a_im + pi * a_re))
    den = lam_re * lam_re + lam_im * lam_im
    f_re = ((a_re - 1.0) * lam_re + a_im * lam_im) / den
    f_im = (a_im * lam_re - (a_re - 1.0) * lam_im) / den
    bb_re = f_re * bre_ref[...] - f_im * bim_ref[...]
    bb_im = f_re * bim_ref[...] + f_im * bre_ref[...]
    c_re = cre_ref[...]
    c_im = cim_ref[...]
    ere = ere_ref[...]
    eim = eim_ref[...]

    r_g = lax.broadcasted_iota(jnp.int32, (LANES, 4 * SLAB_LANES), 0) // SSM_GROUP
    col = lax.broadcasted_iota(jnp.int32, (LANES, 4 * SLAB_LANES), 1)
    c_g = 4 * (col // (2 * SLAB_LANES)) + (col % SLAB_LANES) // SSM_STATE
    same_group = r_g == c_g

    def embed(v_re, v_im, w):
        rows = slice(w * LANES, (w + 1) * LANES)
        blk = _dot(v_re[rows, :].astype(BF16), ere) + _dot(v_im[rows, :].astype(BF16), eim)
        return jnp.where(same_group, blk, 0.0).astype(BF16)

    for i in range(SCAN_R):
        pr, pi = pw[SCAN_R - 1 - i]
        g_re = pr * bb_re - pi * bb_im
        g_im = pr * bb_im + pi * bb_re
        qr, qi = pw[i + 1]
        w_re = c_re * qr - c_im * qi
        w_im = -(c_re * qi + c_im * qr)
        for w in range(N_WIN):
            bst_ref[w, i * LANES:(i + 1) * LANES, :] = embed(g_re, g_im, w)
            ckt_ref[w, i * LANES:(i + 1) * LANES, :] = embed(w_re, w_im, w)

    kr = lax.broadcasted_iota(jnp.int32, (LANES, LANES), 0) // SSM_GROUP
    kc = lax.broadcasted_iota(jnp.int32, (LANES, LANES), 1) // SSM_GROUP
    k_same = kr == kc
    zero_blk = jnp.zeros((LANES, LANES), BF16)
    for tau in range(SCAN_R):
        pr, pi = pw[tau]
        w_re = c_re * pr - c_im * pi
        w_im = c_re * pi + c_im * pr
        for w in range(N_WIN):
            rows = slice(w * LANES, (w + 1) * LANES)
            kt = _dot_nt3(bb_re[rows, :], w_re[rows, :]) - _dot_nt3(bb_im[rows, :], w_im[rows, :])
            ktm = jnp.where(k_same, kt, 0.0).astype(BF16)
            for i_in in range(SCAN_R - tau):
                i_out = i_in + tau
                kst_ref[w, i_in * LANES:(i_in + 1) * LANES, i_out * LANES:(i_out + 1) * LANES] = ktm
            if tau > 0:
                for i_out in range(tau):
                    i_in = i_out + SCAN_R - tau
                    kst_ref[w, i_in * LANES:(i_in + 1) * LANES, i_out * LANES:(i_out + 1) * LANES] = zero_blk

    shape2 = (SCAN_M, SSM_GROUPS * SSM_STATE)
    n = ((lax.broadcasted_iota(jnp.int32, shape2, 0) + 1) * SCAN_R).astype(F32)
    dt_r = jnp.exp(ldt_r[...])
    e = jnp.exp(n * (lre_r[...] * dt_r))
    th = n * (lim_r[...] * dt_r)
    p_re = e * jnp.cos(th)
    p_im = e * jnp.sin(th)
    for s in range(N_SLABS):
        lanes = slice(s * SLAB_LANES, (s + 1) * SLAB_LANES)
        for m in range(SCAN_M):
            pwr_ref[s, m] = jnp.broadcast_to(p_re[m:m + 1, lanes], (SUBLANES, SLAB_LANES))
            pwi_ref[s, m] = jnp.broadcast_to(p_im[m:m + 1, lanes], (SUBLANES, SLAB_LANES))


def _embed_consts():
    e = np.zeros((2, SSM_STATE, 4 * SLAB_LANES), np.float32)
    for ri in range(2):
        for sg in range(2):
            for g4 in range(4):
                base = sg * 2 * SLAB_LANES + ri * SLAB_LANES + g4 * SSM_STATE
                e[ri, np.arange(SSM_STATE), base + np.arange(SSM_STATE)] = 1.0
    return jnp.asarray(e[0], BF16), jnp.asarray(e[1], BF16)


def _ssm_prep(lam_re, lam_im, log_dt, b_re, b_im, c_re, c_im):
    ld = DEPTH * 2
    gp = SSM_GROUPS * SSM_STATE
    rep = lambda a: jnp.broadcast_to(a.reshape(ld, SSM_GROUPS, 1, -1),
                                     (ld, SSM_GROUPS, SSM_GROUP, SSM_STATE)).reshape(ld, GC, SSM_STATE)
    ldt = log_dt.reshape(ld, SSM_GROUPS, 1)
    row = lambda a: jnp.broadcast_to(a.reshape(ld, SSM_GROUPS, -1), (ld, SSM_GROUPS, SSM_STATE)).reshape(ld, 1, gp)
    bt = lambda b: b.reshape(ld, SSM_GROUPS, SSM_STATE, SSM_GROUP).transpose(0, 1, 3, 2).reshape(ld, GC, SSM_STATE)
    ct = lambda c: c.reshape(ld, GC, SSM_STATE)
    ere, eim = _embed_consts()
    gspec = pl.BlockSpec((None, GC, SSM_STATE), lambda i: (i, 0, 0))
    rspec = pl.BlockSpec((None, 1, gp), lambda i: (i, 0, 0))
    espec = _const_spec((SSM_STATE, 4 * SLAB_LANES))
    tspec = lambda r, c: pl.BlockSpec((None, N_WIN, r, c), lambda i: (i, 0, 0, 0))
    pspec = pl.BlockSpec((None, N_SLABS, SCAN_M, SUBLANES, SLAB_LANES), lambda i: (i, 0, 0, 0, 0))
    outs = pl.pallas_call(
        _ssm_prep_body,
        grid=(ld,),
        in_specs=[gspec] * 7 + [rspec] * 3 + [espec, espec],
        out_specs=[tspec(SCAN_R * LANES, 4 * SLAB_LANES), tspec(SCAN_R * LANES, 4 * SLAB_LANES),
                   tspec(SCAN_R * LANES, SCAN_R * LANES), pspec, pspec],
        out_shape=[jax.ShapeDtypeStruct((ld, N_WIN, SCAN_R * LANES, 4 * SLAB_LANES), BF16),
                   jax.ShapeDtypeStruct((ld, N_WIN, SCAN_R * LANES, 4 * SLAB_LANES), BF16),
                   jax.ShapeDtypeStruct((ld, N_WIN, SCAN_R * LANES, SCAN_R * LANES), BF16),
                   jax.ShapeDtypeStruct((ld, N_SLABS, SCAN_M, SUBLANES, SLAB_LANES), F32),
                   jax.ShapeDtypeStruct((ld, N_SLABS, SCAN_M, SUBLANES, SLAB_LANES), F32)],
        compiler_params=_cparams(("arbitrary",)),
        name="ssm_prep",
    )(rep(lam_re), rep(lam_im), rep(ldt), bt(b_re), bt(b_im), ct(c_re), ct(c_im),
      row(lam_re), row(lam_im), row(ldt), ere, eim)
    return [o.reshape((DEPTH, 2) + o.shape[1:]) for o in outs]


def _scan_perms():
    rho = np.arange(T)
    i, m, q = rho // BLK_ROWS, (rho % BLK_ROWS) // SUBLANES, rho % SUBLANES
    t_of_rho = q * (T // SUBLANES) + SCAN_R * m + i
    pm = np.zeros((2, T, T), np.float32)
    pm[0, rho, t_of_rho] = 1.0
    pm[1, rho, T - 1 - t_of_rho] = 1.0
    pmt = np.transpose(pm, (0, 2, 1))
    pmt2 = np.concatenate([pmt, pmt], axis=2)
    return jnp.asarray(pm, BF16), jnp.asarray(pmt2, BF16)


def _ssm_body(u_ref, pm_ref, pmt2_ref, bst_ref, ckt_ref, kst_ref, pwr_ref, pwi_ref, h0_ref,
              y_ref, fin_ref, up_s, uc_s, x_s, hp_s, st_s):
    d = pl.program_id(0)
    j = pl.program_id(1)
    is_ctx = j < NS_CTX
    first = jnp.logical_or(is_ctx, (j - NS_CTX) % NS_LAT_SEQ == 0)

    @pl.when(first)
    def _():
        st_s[...] = jnp.where(is_ctx, 0.0, jnp.broadcast_to(h0_ref[...], st_s.shape))

    def tile_rows(g):
        gg = jnp.where(d == 1, SCAN_G - 1 - g, g)
        return pl.ds(pl.multiple_of(gg * T, T), T), gg

    pm = pm_ref[...]
    for g in range(SCAN_G):
        up_s[g] = _dot(pm, u_ref[tile_rows(g)[0], :].astype(BF16)).astype(BF16)

    for w in range(N_WIN):
        lanes = slice(w * LANES, (w + 1) * LANES)
        lhs = jnp.concatenate(
            [jnp.concatenate([up_s[g, i * BLK_ROWS:(i + 1) * BLK_ROWS, lanes] for g in range(SCAN_G)], axis=0)
             for i in range(SCAN_R)], axis=1)
        uc_s[w] = lhs
        x_s[w] = _dot(lhs, bst_ref[w])

    row = lax.broadcasted_iota(jnp.int32, (SUBLANES, SLAB_LANES), 0)
    for s in range(N_SLABS):
        w, sg = s // 2, s % 2
        l_re = slice(sg * 2 * SLAB_LANES, sg * 2 * SLAB_LANES + SLAB_LANES)
        l_im = slice(sg * 2 * SLAB_LANES + SLAB_LANES, (sg + 1) * 2 * SLAB_LANES)
        ar = pwr_ref[s, 0]
        ai = pwi_ref[s, 0]
        m_r = pwr_ref[s, SCAN_M - 1]
        m_i = pwi_ref[s, SCAN_M - 1]
        for g in range(SCAN_G):
            base = g * BLK_ROWS
            hr = jnp.zeros((SUBLANES, SLAB_LANES), F32)
            hi = jnp.zeros((SUBLANES, SLAB_LANES), F32)
            for m in range(SCAN_M):
                rows = slice(base + m * SUBLANES, base + (m + 1) * SUBLANES)
                hr, hi = (ar * hr - ai * hi + x_s[w, rows, l_re], ar * hi + ai * hr + x_s[w, rows, l_im])
                if m < SCAN_M - 1:
                    nxt = slice(base + (m + 1) * SUBLANES, base + (m + 2) * SUBLANES)
                    hp_s[w, nxt, l_re] = hr
                    hp_s[w, nxt, l_im] = hi
            pr_ = st_s[s, 0]
            pi_ = st_s[s, 1]
            if g > 0:
                pr_ = jnp.where(is_ctx, 0.0, pr_)
                pi_ = jnp.where(is_ctx, 0.0, pi_)
            cr = jnp.where(row == 0, pltpu.roll(pr_, 1, 0), pltpu.roll(hr, 1, 0))
            ci = jnp.where(row == 0, pltpu.roll(pi_, 1, 0), pltpu.roll(hi, 1, 0))
            qr, qi = m_r, m_i
            for k in (1, 2, 4):
                sr = pltpu.roll(cr, k, 0)
                si = pltpu.roll(ci, k, 0)
                keep = row >= k
                cr, ci = (cr + jnp.where(keep, qr * sr - qi * si, 0.0),
                          ci + jnp.where(keep, qr * si + qi * sr, 0.0))
                qr, qi = qr * qr - qi * qi, 2.0 * qr * qi
            fr = m_r * cr - m_i * ci + hr
            fi = m_r * ci + m_i * cr + hi
            st_s[s, 0] = fr
            st_s[s, 1] = fi
            slot = tile_rows(g)[1]
            fin_ref[slot, s, 0] = fr[SUBLANES - 1:SUBLANES, :]
            fin_ref[slot, s, 1] = fi[SUBLANES - 1:SUBLANES, :]
            hp_s[w, base:base + SUBLANES, l_re] = cr
            hp_s[w, base:base + SUBLANES, l_im] = ci
            for m in range(SCAN_M - 1):
                rows = slice(base + (m + 1) * SUBLANES, base + (m + 2) * SUBLANES)
                pr = pwr_ref[s, m]
                pi = pwi_ref[s, m]
                hp_s[w, rows, l_re] = hp_s[w, rows, l_re] + pr * cr - pi * ci
                hp_s[w, rows, l_im] = hp_s[w, rows, l_im] + pr * ci + pi * cr

    yw = [_dot_nt(hp_s[w].astype(BF16), ckt_ref[w]) + _dot(uc_s[w], kst_ref[w]) for w in range(N_WIN)]
    pmt2 = pmt2_ref[...]
    for g in range(SCAN_G):
        rows = slice(g * BLK_ROWS, (g + 1) * BLK_ROWS)
        y = jnp.concatenate(
            [jnp.concatenate([yw[w][rows, i * LANES:(i + 1) * LANES] for w in range(N_WIN)], axis=1)
             for i in range(SCAN_R)], axis=0)
        y_hi = y.astype(BF16)
        y_lo = (y - y_hi.astype(F32)).astype(BF16)
        y_ref[tile_rows(g)[0], :] = _dot(pmt2, jnp.concatenate([y_hi, y_lo], axis=0))


def _ssm_super(d, j):
    jl = j - NS_CTX
    half = jl % NS_LAT_SEQ
    half = jnp.where(d == 1, NS_LAT_SEQ - 1 - half, half)
    return jnp.where(j < NS_CTX, j, NS_CTX + (jl // NS_LAT_SEQ) * NS_LAT_SEQ + half)


def _ssm_scan(l, u_ssm, pm, pmt2, bst, ckt, kst, pwr, pwi, h0):
    tab = lambda *shape: pl.BlockSpec((None,) + shape, lambda d, j: (d,) + (0,) * len(shape))
    ltab = lambda *shape: pl.BlockSpec((None, None) + shape, lambda d, j: (l, d) + (0,) * len(shape))
    return pl.pallas_call(
        _ssm_body,
        grid=(2, NS),
        in_specs=[
            pl.BlockSpec((SUPER, BRANCH_W), lambda d, j: (_ssm_super(d, j), 0)),
            tab(T, T), tab(T, 2 * T),
            ltab(N_WIN, SCAN_R * LANES, 4 * SLAB_LANES), ltab(N_WIN, SCAN_R * LANES, 4 * SLAB_LANES),
            ltab(N_WIN, SCAN_R * LANES, SCAN_R * LANES),
            ltab(N_SLABS, SCAN_M, SUBLANES, SLAB_LANES), ltab(N_SLABS, SCAN_M, SUBLANES, SLAB_LANES),
            pl.BlockSpec((None, None, None, N_SLABS, 2, 1, SLAB_LANES),
                         lambda d, j: (l, d, jnp.maximum(j - NS_CTX, 0) // NS_LAT_SEQ, 0, 0, 0, 0)),
        ],
        out_specs=[
            pl.BlockSpec((None, SUPER, BRANCH_W), lambda d, j: (d, _ssm_super(d, j), 0)),
            pl.BlockSpec((None, None, SCAN_G, N_SLABS, 2, 1, SLAB_LANES), lambda d, j: (d, j, 0, 0, 0, 0, 0)),
        ],
        out_shape=[jax.ShapeDtypeStruct((2, N_TOK, BRANCH_W), F32),
                   jax.ShapeDtypeStruct((2, NS, SCAN_G, N_SLABS, 2, 1, SLAB_LANES), F32)],
        scratch_shapes=[
            pltpu.VMEM((SCAN_G, T, BRANCH_W), BF16),
            pltpu.VMEM((N_WIN, SCAN_G * BLK_ROWS, SCAN_R * LANES), BF16),
            pltpu.VMEM((N_WIN, SCAN_G * BLK_ROWS, 4 * SLAB_LANES), F32),
            pltpu.VMEM((N_WIN, SCAN_G * BLK_ROWS, 4 * SLAB_LANES), F32),
            pltpu.VMEM((N_SLABS, 2, SUBLANES, SLAB_LANES), F32),
        ],
        compiler_params=_cparams(("arbitrary", "arbitrary")),
        name="ssm_scan",
    )(u_ssm, pm, pmt2, bst, ckt, kst, pwr, pwi, h0)


def _pre_body(n_x, *refs):
    (mod_ref, g_ref, wmain_ref, wkx_ref, wconv_ref, qag_ref, kvg_ref, wuq_ref, wuqs_ref, wuk_ref, wva_ref,
     wvb_ref, qng_ref, qngs_ref, kng_ref, kngs_ref, rc_ref, rsw_ref,
     ussm_ref, ufft_ref, q_ref, k_ref, va_ref, vb_ref, ckv_ref, kpe_ref, vc_ref, gb_ref) = refs[n_x:]
    i = pl.program_id(0)
    x = _load_stream(i, refs[:n_x])
    sh = mod_ref[:, 0:D_MODEL]
    sc = mod_ref[:, D_MODEL:2 * D_MODEL]
    xn = (_rms(x, g_ref[...]) * (1.0 + sc) + sh).astype(BF16)

    def proj(lo, hi):
        for ref, base, end in ((wmain_ref, Z_SSM, Z_KPE), (wkx_ref, Z_KPE, Z_HIN), (wconv_ref, Z_HIN, Z_COLS)):
            if base <= lo and hi <= end:
                return _dot(xn, ref[:, lo - base:hi - base])
        raise ValueError((lo, hi))

    zk = proj(Z_KPE, Z_KROT)
    krot = proj(Z_KROT, Z_KSW)
    kswp = proj(Z_KSW, Z_HIN)

    rc = rc_ref[...]
    rsw = rsw_ref[...]
    q_gc = qng_ref[...] * rc * ATTN_SCALE
    q_gs = qngs_ref[...] * rsw * ATTN_SCALE
    k_gc = kng_ref[...] * rc
    k_sw = kswp * (kngs_ref[...] * rsw)

    def rstd(v):
        return lax.rsqrt(jnp.sum(v * v, axis=-1, keepdims=True) * (1.0 / QK_DIM) + EPS)

    cqn = _rms(proj(Z_CQ, Z_CKV), qag_ref[...]).astype(BF16)
    qf = _dot(cqn, wuq_ref[...])
    qs = _dot(cqn, wuqs_ref[...])
    ckvn = _rms(proj(Z_CKV, Z_KPE), kvg_ref[...])
    ckvb = ckvn.astype(BF16)
    kf = _dot(ckvb, wuk_ref[...])
    va_ref[...] = (_dot(ckvb, wva_ref[...]) + _ones_column(V_DIM)).astype(BF16)
    vb_ref[...] = (_dot(ckvb, wvb_ref[...]) + _ones_column(0)).astype(BF16)
    for h in range(MLA_HEADS):
        lanes = slice(h * HEAD_PAD, (h + 1) * HEAD_PAD)
        qh = qf[:, lanes]
        q_ref[:, lanes] = ((qh * q_gc + qs[:, lanes] * q_gs) * rstd(qh)).astype(BF16)
        kh = kf[:, lanes] + krot
        k_ref[:, lanes] = ((kh * k_gc + k_sw) * rstd(kh)).astype(BF16)

    ussm_ref[...] = proj(Z_SSM, Z_FFT)
    ufft_ref[...] = proj(Z_FFT, Z_CQ).astype(BF16)
    h_in = proj(Z_HIN, Z_GB)
    gb_ref[...] = proj(Z_GB, Z_GC)
    vc_ref[...] = proj(Z_GC, Z_COLS) * h_in

    @pl.when(i < NTT_CTX)
    def _():
        ckv_ref[...] = ckvn
        kpe_ref[...] = zk[:, 0:ROPE_DIM]


def _pre(l, x, mod, p, rc, rsw):
    tile = _tok_tile_spec
    rope_blk = lambda i: (jnp.where(i < NTT_CTX, NTT_LAT_SEQ, i % NTT_LAT_SEQ), 0)
    rspec = pl.BlockSpec((TT, LANES), rope_blk)
    out = lambda width, dt: jax.ShapeDtypeStruct((N_TOK, width), dt)
    names = ('norm_mix_g', 'w_main', 'w_kx', 'w_conv', 'q_a_norm_g', 'kv_a_norm_g', 'wuq', 'wuqs', 'wuk', 'wva', 'wvb',
             'qng', 'qngs', 'kng', 'kngs')
    x_specs, x_args = _stream_specs(x, D_MODEL)
    return pl.pallas_call(
        functools.partial(_pre_body, len(x_args)),
        grid=(NTT,),
        in_specs=x_specs + [_mod_spec(l)]
        + [_layer_spec(l, p[n].shape[1:]) for n in names] + [rspec, rspec],
        out_specs=[tile(BRANCH_W), tile(BRANCH_W), tile(MLA_HEADS * HEAD_PAD), tile(MLA_HEADS * HEAD_PAD),
                   tile(BRANCH_W), tile(BRANCH_W),
                   _ctx_tile_spec(KV_RANK), _ctx_tile_spec(ROPE_DIM), tile(BRANCH_W), tile(BRANCH_W)],
        out_shape=[out(BRANCH_W, F32), out(BRANCH_W, BF16), out(MLA_HEADS * HEAD_PAD, BF16),
                   out(MLA_HEADS * HEAD_PAD, BF16), out(BRANCH_W, BF16), out(BRANCH_W, BF16),
                   jax.ShapeDtypeStruct((N_CTX, KV_RANK), F32), jax.ShapeDtypeStruct((N_CTX, ROPE_DIM), F32),
                   out(BRANCH_W, F32), out(BRANCH_W, F32)],
        compiler_params=_cparams(("arbitrary",)),
        name="pre_mixer",
    )(*x_args, mod, *[p[n] for n in names], rc, rsw)


def _kvc_body(ckv_ref, kr_ref, wuk_ref, wva_ref, wvb_ref, kng_ref, k_ref, va_ref, vb_ref):
    ckvb = ckv_ref[...].astype(BF16)
    kf = _dot(ckvb, wuk_ref[...])
    va_ref[...] = (_dot(ckvb, wva_ref[...]) + _ones_column(V_DIM)).astype(BF16)
    vb_ref[...] = (_dot(ckvb, wvb_ref[...]) + _ones_column(0)).astype(BF16)
    krot = kr_ref[...]
    g = kng_ref[...]
    for h in range(MLA_HEADS):
        lanes = slice(h * HEAD_PAD, (h + 1) * HEAD_PAD)
        kh = kf[:, lanes] + krot
        ss = jnp.sum(kh * kh, axis=-1, keepdims=True)
        k_ref[:, lanes] = (kh * lax.rsqrt(ss * (1.0 / QK_DIM) + EPS) * g).astype(BF16)


def _cache_kv(cache_ckv, cache_kr, wuk, wva, wvb, kng):
    nt = PAST_LEN // T
    wspec = lambda r, c: pl.BlockSpec((None, r, c), lambda l, b, t: (l, 0, 0))
    ospec = lambda c: pl.BlockSpec((None, None, T, c), lambda l, b, t: (l, b, t, 0))
    return pl.pallas_call(
        _kvc_body,
        grid=(DEPTH, DEC_BATCH, nt),
        in_specs=[
            pl.BlockSpec((None, None, T, KV_RANK), lambda l, b, t: (b, l, t, 0)),
            pl.BlockSpec((None, None, T, HEAD_PAD), lambda l, b, t: (b, l, t, 0)),
            wspec(KV_RANK, MLA_HEADS * HEAD_PAD), wspec(KV_RANK, BRANCH_W), wspec(KV_RANK, BRANCH_W),
            wspec(1, HEAD_PAD),
        ],
        out_specs=[ospec(MLA_HEADS * HEAD_PAD), ospec(BRANCH_W), ospec(BRANCH_W)],
        out_shape=[jax.ShapeDtypeStruct((DEPTH, DEC_BATCH, PAST_LEN, MLA_HEADS * HEAD_PAD), BF16),
                   jax.ShapeDtypeStruct((DEPTH, DEC_BATCH, PAST_LEN, BRANCH_W), BF16),
                   jax.ShapeDtypeStruct((DEPTH, DEC_BATCH, PAST_LEN, BRANCH_W), BF16)],
        compiler_params=_cparams(("parallel", "parallel", "parallel")),
        name="cache_kv",
    )(cache_ckv, cache_kr, wuk, wva, wvb, kng)


def _fnet_body(seq_len, ns, u_ref, csl_ref, csc_ref, y_ref, v1_s, v2_s):
    r = pl.program_id(1)

    @pl.when(r == 0)
    def _():
        csc = csc_ref[...]
        for c in range(ns * seq_len // T):
            rows = slice(c * T, (c + 1) * T)
            for g in range(FFT_GROUPS):
                lanes = slice(g * FFT_GW, (g + 1) * FFT_GW)
                v = _dot(u_ref[rows, lanes], csc)
                v1_s[rows, lanes] = v[:, :FFT_GW].astype(BF16)
                v2_s[rows, lanes] = v[:, FFT_GW:].astype(BF16)

    for s in range(ns):
        rows = slice(s * seq_len, (s + 1) * seq_len)
        y = _dot(csl_ref[:, :seq_len], v1_s[rows, :]) + _dot(csl_ref[:, seq_len:], v2_s[rows, :])
        y_ref[s * T:(s + 1) * T, :] = y.astype(BF16)


def _fnet(u_fft, csl, csc, seq_len, n_seq, tile0, ns=1):
    nr = seq_len // T
    assert ns == 1 or nr == 1
    blk0 = tile0 * T // (ns * seq_len)
    return pl.pallas_call(
        functools.partial(_fnet_body, seq_len, ns),
        grid=(n_seq // ns, nr),
        in_specs=[
            pl.BlockSpec((ns * seq_len, BRANCH_W), lambda b, r: (blk0 + b, 0)),
            pl.BlockSpec((T, 2 * seq_len), lambda b, r: (r, 0)),
            _const_spec((FFT_GW, 2 * FFT_GW)),
        ],
        out_specs=pl.BlockSpec((ns * T, BRANCH_W), lambda b, r: (b * nr + r, 0)),
        out_shape=jax.ShapeDtypeStruct((n_seq * seq_len, BRANCH_W), BF16),
        scratch_shapes=[pltpu.VMEM((ns * seq_len, BRANCH_W), BF16), pltpu.VMEM((ns * seq_len, BRANCH_W), BF16)],
        compiler_params=_cparams(("arbitrary", "arbitrary")),
        name="fnet_%d" % seq_len,
    )(u_fft, csl, csc)


def _dft_tables(n, split):
    m = jnp.arange(n, dtype=jnp.int32)
    a = jnp.arange(n // split, dtype=jnp.int32) * split
    b = jnp.arange(split, dtype=jnp.int32)
    w = 2.0 * math.pi / n
    ang_a = ((a[:, None] * m[None, :]) % n).astype(F32) * w
    ang_b = ((b[:, None] * m[None, :]) % n).astype(F32) * w
    ca, sa = jnp.cos(ang_a)[:, None, :], jnp.sin(ang_a)[:, None, :]
    cb, sb = jnp.cos(ang_b)[None, :, :], jnp.sin(ang_b)[None, :, :]
    s = 1.0 / math.sqrt(n)
    cos = ((ca * cb - sa * sb) * s).reshape(n, n)
    sin = ((sa * cb + ca * sb) * s).reshape(n, n)
    return cos, sin


def _attn_body(n_seg, q_ref, *refs):
    k_refs = refs[0:n_seg]
    va_refs = refs[n_seg:2 * n_seg]
    vb_refs = refs[2 * n_seg:3 * n_seg]
    o_ref = refs[-1]
    lane = lax.broadcasted_iota(jnp.int32, (q_ref.shape[0], LANES), 1)
    for hp in range(MLA_HEADS // 2):
        out_lanes = slice(hp * LANES, (hp + 1) * LANES)
        halves = []
        for h, v_refs, sum_lane in ((2 * hp, va_refs, V_DIM), (2 * hp + 1, vb_refs, 0)):
            lanes = slice(h * HEAD_PAD, (h + 1) * HEAD_PAD)
            qh = q_ref[:, lanes]
            ss = [_dot_nt(qh, kr[:, lanes]) for kr in k_refs]
            m = functools.reduce(jnp.maximum, [jnp.max(s, axis=-1, keepdims=True) for s in ss])
            o = functools.reduce(lambda a, b: a + b,
                                 [_dot(jnp.exp2(s - m).astype(BF16), vr[:, out_lanes]) for s, vr in zip(ss, v_refs)])
            l = jnp.sum(jnp.where(lane == sum_lane, o, 0.0), axis=-1, keepdims=True)
            halves.append(o / l)
        o_ref[:, out_lanes] = jnp.where(lane < V_DIM, halves[0], halves[1]).astype(BF16)


def _attention(q, segs, n_seq, nq, row0, tq):
    tile0 = row0 // tq
    in_specs = [pl.BlockSpec((tq, MLA_HEADS * HEAD_PAD), lambda b, r: (tile0 + b * nq + r, 0))]
    args = [q]
    for idx in range(3):
        for seg in segs:
            arr = seg[idx]
            width = arr.shape[-1]
            in_specs.append(pl.BlockSpec(seg[3] + (width,), seg[4]))
            args.append(arr)
    return pl.pallas_call(
        functools.partial(_attn_body, len(segs)),
        grid=(n_seq, nq),
        in_specs=in_specs,
        out_specs=pl.BlockSpec((tq, BRANCH_W), lambda b, r: (b * nq + r, 0)),
        out_shape=jax.ShapeDtypeStruct((n_seq * nq * tq, BRANCH_W), BF16),
        compiler_params=_cparams(("arbitrary", "arbitrary")),
        name="attention_%d" % nq,
    )(*args)


def _mix_body(n_x, *refs):
    (mod_ref, g_ref, yf_ref, yb_ref, u_ref, dsk_ref, wglu_ref, fc_ref, fl_ref,
     ac_ref, al_ref, vc_ref, vp_ref, vn_ref, cw_ref, gb_ref, wbr_ref, wg_ref, bg_ref, wo_ref, o_ref) = refs[n_x:]
    i = pl.program_id(0)
    x = _load_stream(i, refs[:n_x])
    sh = mod_ref[:, 0:D_MODEL]
    sc = mod_ref[:, D_MODEL:2 * D_MODEL]
    gm = mod_ref[:, 2 * D_MODEL:3 * D_MODEL]
    xn = (_rms(x, g_ref[...]) * (1.0 + sc) + sh).astype(BF16)

    y = jax.nn.gelu(yf_ref[...] + yb_ref[...] + u_ref[...] * dsk_ref[...], approximate=True)
    y = y * _sigmoid(_dot(y.astype(BF16), wglu_ref[...]))

    v = vc_ref[...]
    row = lax.broadcasted_iota(jnp.int32, (TT, BRANCH_W), 0)
    seq_len = jnp.where(i < NTT_CTX, SEQ, DEC_SEQ)
    pos = jnp.bitwise_and(i * TT + row, seq_len - 1)
    v_prev = jnp.where(row == 0, vp_ref[SUBLANES - 1:SUBLANES, :], pltpu.roll(v, 1, 0))
    v_next = jnp.where(row == TT - 1, vn_ref[0:1, :], pltpu.roll(v, TT - 1, 0))
    v_prev = jnp.where(pos == 0, 0.0, v_prev)
    v_next = jnp.where(pos == seq_len - 1, 0.0, v_next)
    y_conv = gb_ref[...] * (cw_ref[0:1, :] * v_prev + cw_ref[1:2, :] * v + cw_ref[2:3, :] * v_next)

    branches = (y.astype(BF16), _load_pair(i, fc_ref, fl_ref), _load_pair(i, ac_ref, al_ref),
                y_conv.astype(BF16))
    merged = jnp.zeros((TT, D_MODEL), F32)
    for k in range(N_BRANCH):
        cols = slice(k * D_MODEL, (k + 1) * D_MODEL)
        gate = _sigmoid(_dot(xn, wg_ref[:, cols]) + bg_ref[:, cols])
        merged = merged + gate * _dot(branches[k], wbr_ref[k])
    out = _dot(merged.astype(BF16), wo_ref[...])
    o_ref[...] = x + gm * out


def _mix(l, x, mod, p, y_ssm, u_ssm, fft_c, fft_l, att_c, att_l, vconv, g_b):
    tile = _tok_tile_spec
    rb = TT // SUBLANES
    n_rb = N_TOK // SUBLANES
    lspec = lambda n: _layer_spec(l, p[n].shape[1:])
    x_specs, x_args = _stream_specs(x, D_MODEL)
    return pl.pallas_call(
        functools.partial(_mix_body, len(x_args)),
        grid=(NTT,),
        in_specs=x_specs + [
            _mod_spec(l), lspec('norm_mix_g'),
            pl.BlockSpec((None, TT, BRANCH_W), lambda i: (0, i, 0)),
            pl.BlockSpec((None, TT, BRANCH_W), lambda i: (1, i, 0)),
            tile(BRANCH_W), lspec('ssm_d'), lspec('w_glu'),
            _ctx_tile_spec(BRANCH_W), _lat_tile_spec(BRANCH_W), _ctx_tile_spec(BRANCH_W), _lat_tile_spec(BRANCH_W),
            tile(BRANCH_W),
            pl.BlockSpec((SUBLANES, BRANCH_W), lambda i: (jnp.maximum(i * rb - 1, 0), 0)),
            pl.BlockSpec((SUBLANES, BRANCH_W), lambda i: (jnp.minimum(i * rb + rb, n_rb - 1), 0)),
            lspec('conv_w'), tile(BRANCH_W),
            lspec('w_branch'), lspec('w_gate'), lspec('b_gate'), lspec('w_out'),
        ],
        out_specs=tile(D_MODEL),
        out_shape=jax.ShapeDtypeStruct((N_TOK, D_MODEL), F32),
        compiler_params=_cparams(("arbitrary",)),
        name="mixer_merge",
    )(*x_args, mod, p['norm_mix_g'], y_ssm, y_ssm, u_ssm, p['ssm_d'], p['w_glu'], fft_c, fft_l, att_c, att_l,
      vconv, vconv, vconv, p['conv_w'], g_b, p['w_branch'], p['w_gate'], p['b_gate'], p['w_out'])


FF_CHUNK = D_FF // 2


def _ffn_body(n_out, x_ref, mod_ref, g_ref, wi_ref, wo_ref, *o_refs):
    i = pl.program_id(0)
    x = x_ref[...]
    sh = mod_ref[:, 3 * D_MODEL:4 * D_MODEL]
    sc = mod_ref[:, 4 * D_MODEL:5 * D_MODEL]
    gf = mod_ref[:, 5 * D_MODEL:6 * D_MODEL]
    xn = (_rms(x, g_ref[...]) * (1.0 + sc) + sh).astype(BF16)
    acc = jnp.zeros((TT, D_MODEL), F32)
    for c in range(D_FF // FF_CHUNK):
        lo = c * FF_CHUNK
        g = _dot(xn, wi_ref[:, lo:lo + FF_CHUNK])
        u = _dot(xn, wi_ref[:, D_FF + lo:D_FF + lo + FF_CHUNK])
        h = (g * _sigmoid(g) * u).astype(BF16)
        acc = acc + _dot(h, wo_ref[lo:lo + FF_CHUNK, :])
    _store_stream(i, o_refs[:n_out], x + gf * acc)


def _ffn(l, x, mod, p, split_out):
    lspec = lambda n: _layer_spec(l, p[n].shape[1:])
    tile = _tok_tile_spec(D_MODEL)
    if split_out:
        out_specs = [_ctx_tile_spec(D_MODEL), _lat_tile_spec(D_MODEL)]
        out_shape = [jax.ShapeDtypeStruct((N_CTX, D_MODEL), F32), jax.ShapeDtypeStruct((N_LAT, D_MODEL), F32)]
    else:
        out_specs = [tile]
        out_shape = [jax.ShapeDtypeStruct((N_TOK, D_MODEL), F32)]
    return pl.pallas_call(
        functools.partial(_ffn_body, len(out_specs)),
        grid=(NTT,),
        in_specs=[tile, _mod_spec(l), lspec('norm_ffn_g'), lspec('w_ffn_in'), lspec('w_ffn_out')],
        out_specs=out_specs,
        out_shape=out_shape,
        compiler_params=_cparams(("arbitrary",)),
        name="ffn",
    )(x, mod, p['norm_ffn_g'], p['w_ffn_in'], p['w_ffn_out'])


def _prep_weights(norm_mix_g, norm_ffn_g, w_in, q_a_norm_g, kv_a_norm_g, w_uq, w_ukv, q_norm_g, k_norm_g,
                  ssm_d, w_glu, conv_w, w_branch, w_gate, b_gate, w_out, w_ffn_in, w_ffn_out):
    vec = lambda a: a.reshape(DEPTH, 1, a.shape[-1])
    kpe = w_in[:, :, OFF_KPE:OFF_KPE + ROPE_DIM]
    z = lambda n: jnp.zeros((DEPTH, D_MODEL, n), F32)
    w_main = w_in[:, :, OFF_SSM:OFF_KPE].astype(BF16)
    w_kx = jnp.concatenate([
        kpe, z(LANES - ROPE_DIM),
        z(NOPE_DIM), kpe[:, :, ROPE_PERM], z(LANES - QK_DIM),
        z(NOPE_DIM), kpe[:, :, ROPE_PARTNER], z(LANES - QK_DIM),
    ], axis=2).astype(BF16)
    w_conv = w_in[:, :, OFF_CONV:OFF_CONV + 3 * BRANCH_W].astype(BF16)
    wq = w_uq.reshape(DEPTH, Q_RANK, MLA_HEADS, QK_DIM)
    head_pad = lambda a: jnp.pad(a, ((0, 0), (0, 0), (0, 0), (0, HEAD_PAD - a.shape[-1]))).reshape(
        DEPTH, a.shape[1], MLA_HEADS * HEAD_PAD).astype(BF16)
    wuq = head_pad(jnp.concatenate([wq[..., :NOPE_DIM], wq[..., NOPE_DIM:][..., ROPE_PERM]], axis=3))
    wuqs = head_pad(jnp.concatenate([jnp.zeros_like(wq[..., :NOPE_DIM]), wq[..., NOPE_DIM:][..., ROPE_PARTNER]],
                                    axis=3))
    wkv = w_ukv.reshape(DEPTH, KV_RANK, MLA_HEADS, NOPE_DIM + V_DIM)
    wuk = head_pad(wkv[..., :NOPE_DIM])
    wv = wkv[..., NOPE_DIM:]
    even = (jnp.arange(MLA_HEADS) % 2 == 0)[None, None, :, None]
    wva = jnp.where(even, wv, 0.0).reshape(DEPTH, KV_RANK, BRANCH_W).astype(BF16)
    wvb = jnp.where(even, 0.0, wv).reshape(DEPTH, KV_RANK, BRANCH_W).astype(BF16)

    def gain(g, perm):
        gp = jnp.concatenate([g[:, :NOPE_DIM], g[:, NOPE_DIM:][:, perm],
                              jnp.zeros((DEPTH, HEAD_PAD - QK_DIM), F32)], axis=1)
        return gp.reshape(DEPTH, 1, HEAD_PAD)

    return dict(
        norm_mix_g=vec(norm_mix_g), norm_ffn_g=vec(norm_ffn_g), w_main=w_main, w_kx=w_kx, w_conv=w_conv,
        q_a_norm_g=vec(q_a_norm_g), kv_a_norm_g=vec(kv_a_norm_g),
        wuq=wuq, wuqs=wuqs, wuk=wuk, wva=wva, wvb=wvb,
        qng=gain(q_norm_g, ROPE_PERM), qngs=gain(q_norm_g, ROPE_PARTNER),
        kng=gain(k_norm_g, ROPE_PERM), kngs=gain(k_norm_g, ROPE_PARTNER),
        ssm_d=vec(ssm_d), w_glu=w_glu.astype(BF16),
        conv_w=jnp.pad(conv_w, ((0, 0), (0, SUBLANES - 3), (0, 0))),
        w_branch=w_branch.astype(BF16), w_gate=w_gate.astype(BF16), b_gate=vec(b_gate),
        w_out=w_out.astype(BF16), w_ffn_in=w_ffn_in.astype(BF16), w_ffn_out=w_ffn_out.astype(BF16),
    )


def _rope_tables():
    rows = DEC_SEQ // GRID_W
    r = jnp.repeat(jnp.arange(rows, dtype=F32), GRID_W)
    c = jnp.tile(jnp.arange(GRID_W, dtype=F32), rows)
    n_freq = ROPE_DIM // 4
    inv = ROPE_THETA ** (-jnp.arange(n_freq, dtype=F32) / n_freq)
    ang = jnp.concatenate([r[:, None] * inv, c[:, None] * inv], axis=-1)
    cos, sin = jnp.cos(ang), jnp.sin(ang)
    ones = jnp.ones((DEC_SEQ, NOPE_DIM), F32)
    zeros = jnp.zeros((DEC_SEQ, NOPE_DIM), F32)
    tail1 = jnp.ones((DEC_SEQ, HEAD_PAD - QK_DIM), F32)
    tail0 = jnp.zeros((DEC_SEQ, HEAD_PAD - QK_DIM), F32)
    rc = jnp.concatenate([ones, cos, cos, tail1], axis=1)
    rsw = jnp.concatenate([zeros, -sin, sin, tail0], axis=1)
    ident = lambda v: jnp.full((TT, HEAD_PAD), v, F32)
    return jnp.concatenate([rc, ident(1.0)], axis=0), jnp.concatenate([rsw, ident(0.0)], axis=0)


def kernel(x_prompt, x_sample, cache_ckv, cache_kpe, state_ssm, c, c_ctx, norm_mix_g, norm_ffn_g, w_ada, b_ada,
           w_in, q_a_norm_g, kv_a_norm_g, w_uq, w_ukv, q_norm_g, k_norm_g, ssm_lam_re, ssm_lam_im, ssm_log_dt,
           ssm_b_re, ssm_b_im, ssm_c_re, ssm_c_im, ssm_d, w_glu, conv_w, w_branch, w_gate, b_gate, w_out,
           w_ffn_in, w_ffn_out):
    xp = x_prompt.reshape(N_CTX, D_MODEL)
    xs = x_sample.reshape(N_LAT, D_MODEL)

    cond8 = jnp.concatenate([c_ctx[None, :], c, jnp.zeros((SUBLANES - 1 - DEC_BATCH, D_MODEL), F32)], axis=0)
    mod = _modulation(cond8, w_ada, b_ada).reshape(DEPTH, SUBLANES, 1, 6 * D_MODEL)

    bst, ckt, kst, pwr, pwi = _ssm_prep(ssm_lam_re, ssm_lam_im, ssm_log_dt, ssm_b_re, ssm_b_im,
                                        ssm_c_re, ssm_c_im)
    pm, pmt2 = _scan_perms()
    st = state_ssm.transpose(1, 2, 0, 5, 3, 4).reshape(DEPTH, 2, DEC_BATCH, 2, N_SLABS, 1, SLAB_LANES)
    h0_all = st.transpose(0, 1, 2, 4, 3, 5, 6)

    rc, rsw = _rope_tables()
    cl, sl = _dft_tables(SEQ, 16)
    csl_ctx = jnp.concatenate([cl, -sl], axis=1).astype(BF16)
    cl, sl = _dft_tables(DEC_SEQ, 64)
    csl_lat = jnp.concatenate([cl, -sl], axis=1).astype(BF16)
    cc, sc = _dft_tables(FFT_GW, 8)
    csc = jnp.concatenate([cc, sc], axis=1).astype(BF16)

    p = _prep_weights(norm_mix_g, norm_ffn_g, w_in, q_a_norm_g, kv_a_norm_g, w_uq, w_ukv, q_norm_g, k_norm_g,
                      ssm_d, w_glu, conv_w, w_branch, w_gate, b_gate, w_out, w_ffn_in, w_ffn_out)
    cache_kr = jnp.pad(cache_kpe[..., ROPE_PERM], ((0, 0), (0, 0), (0, 0), (NOPE_DIM, HEAD_PAD - QK_DIM)))
    kc, vca, vcb = _cache_kv(cache_ckv, cache_kr, p['wuk'], p['wva'], p['wvb'], p['kng'])

    lat_blk = N_CTX // DEC_SEQ
    ckv_list, kpe_list, fin_list = [], [], []
    x = (xp, xs)
    for l in range(DEPTH):
        (u_ssm, u_fft, q, k, va, vb, ckv_n, kpe, vconv, g_b) = _pre(l, x, mod, p, rc, rsw)
        ckv_list.append(ckv_n.reshape(BATCH, SEQ, KV_RANK))
        kpe_list.append(kpe.reshape(BATCH, SEQ, ROPE_DIM))

        y_ssm, fin = _ssm_scan(l, u_ssm, pm, pmt2, bst, ckt, kst, pwr, pwi, h0_all)
        fin_list.append(fin[:, :NS_CTX].reshape(2, BATCH, N_SLABS, 2, SLAB_LANES))

        fft_c = _fnet(u_fft, csl_ctx, csc, SEQ, BATCH, 0, ns=4)
        fft_l = _fnet(u_fft, csl_lat, csc, DEC_SEQ, DEC_BATCH, NT_CTX)

        ctx_seg = (k, va, vb, (T,), lambda b, r: (b, 0))
        att_c = _attention(q, [ctx_seg], BATCH, 1, 0, SEQ)
        lat_seg = (k, va, vb, (DEC_SEQ,), lambda b, r: (lat_blk + b, 0))
        cache_seg = (kc, vca, vcb, (None, None, PAST_LEN), functools.partial(lambda l_, b, r: (l_, b, 0, 0), l))
        att_l = _attention(q, [lat_seg, cache_seg], DEC_BATCH, DEC_SEQ // TQ_LAT, N_CTX, TQ_LAT)

        x = _mix(l, x, mod, p, y_ssm, u_ssm, fft_c, fft_l, att_c, att_l, vconv, g_b)
        x = _ffn(l, x, mod, p, split_out=(l == DEPTH - 1))
        x = tuple(x) if l == DEPTH - 1 else x[0]

    y_p = x[0].reshape(BATCH, SEQ, D_MODEL)
    y_s = x[1].reshape(DEC_BATCH, DEC_SEQ, D_MODEL)
    new_ckv = jnp.stack(ckv_list, axis=1)
    new_kpe = jnp.stack(kpe_list, axis=1)
    fin = jnp.stack(fin_list, axis=0)
    fin = fin.reshape(DEPTH, 2, BATCH, N_SLABS, 2, 4, SSM_STATE).transpose(2, 0, 1, 3, 5, 6, 4)
    new_ssm = fin.reshape(BATCH, DEPTH, 2, SSM_GROUPS, SSM_STATE, 2)
    return (y_p, y_s, new_ckv, new_kpe, new_ssm)
```

```python
import functools
import math

import numpy as np
import jax
import jax.numpy as jnp
from jax import lax
from jax.experimental import pallas as pl
from jax.experimental.pallas import tpu as pltpu

F32 = jnp.float32
BF16 = jnp.bfloat16

D_MODEL = 1024
BATCH = 32
SEQ = 256
DEPTH = 4
DEC_BATCH = 2
DEC_SEQ = 2048
PAST_LEN = 512
GRID_W = 64
N_BRANCH = 4
BRANCH_W = 512
SSM_GROUP = 16
SSM_GROUPS = 32
SSM_STATE = 64
FFT_GROUPS = 4
FFT_GW = 128
MLA_HEADS = 8
NOPE_DIM = 64
ROPE_DIM = 32
V_DIM = 64
QK_DIM = 96
Q_RANK = 384
KV_RANK = 256
ROPE_THETA = 10000.0
D_FF = 2816
EPS = 1e-6

OFF_SSM = 0
OFF_FFT = 512
OFF_CQ = 1024
OFF_CKV = 1408
OFF_KPE = 1664
OFF_CONV = 1696

LANES = 128
SUBLANES = 8
VMEM_LIMIT_BYTES = 56 * 1024 * 1024

T = 256
N_CTX = BATCH * SEQ
N_LAT = DEC_BATCH * DEC_SEQ
N_TOK = N_CTX + N_LAT
NT_CTX = N_CTX // T
NT_LAT_SEQ = DEC_SEQ // T
NT = N_TOK // T
TT = 512
CTX_ATT_SEQS = 4
TQ_LAT = 512
NTT_CTX = N_CTX // TT
NTT_LAT_SEQ = DEC_SEQ // TT
NTT = N_TOK // TT
N_SEQ = BATCH + DEC_BATCH
HEAD_PAD = LANES
ATTN_SCALE = QK_DIM ** -0.5 * math.log2(math.e)


def _ones_column(offset):
    lane = lax.broadcasted_iota(jnp.int32, (1, BRANCH_W), 1)
    return (lane % LANES == offset).astype(F32)

Z_SSM = 0
Z_FFT = 512
Z_CQ = 1024
Z_CKV = 1408
Z_KPE = 1664
Z_KROT = 1792
Z_KSW = 1920
Z_HIN = 2048
Z_GB = 2560
Z_GC = 3072
Z_COLS = 3584

SCAN_R = 4
SCAN_M = T // SUBLANES // SCAN_R
BLK_ROWS = T // SCAN_R
SCAN_G = 4
SUPER = SCAN_G * T
NS_CTX = N_CTX // SUPER
NS_LAT_SEQ = DEC_SEQ // SUPER
NS = N_TOK // SUPER
SLAB_LANES = 256
N_SLABS = SSM_GROUPS * SSM_STATE // SLAB_LANES
N_WIN = BRANCH_W // LANES
GC = SSM_GROUPS * SSM_GROUP
ROPE_PERM = np.concatenate([np.arange(0, ROPE_DIM, 2), np.arange(1, ROPE_DIM, 2)])
ROPE_PARTNER = np.concatenate([np.arange(1, ROPE_DIM, 2), np.arange(0, ROPE_DIM, 2)])


def _cparams(sem):
    return pltpu.CompilerParams(dimension_semantics=sem, vmem_limit_bytes=VMEM_LIMIT_BYTES)


def _const_spec(shape):
    nd = len(shape)
    return pl.BlockSpec(shape, lambda *_: (0,) * nd, pipeline_mode=pl.Buffered(1))


def _layer_spec(l, shape):
    nd = len(shape)
    return pl.BlockSpec((None,) + tuple(shape), lambda *_: (l,) + (0,) * nd, pipeline_mode=pl.Buffered(1))


def _mod_spec(l):
    return pl.BlockSpec((None, None, 1, 6 * D_MODEL), lambda i: (l, _cond_row(i), 0, 0))


def _ctx_tile_spec(width):
    return pl.BlockSpec((TT, width), lambda i: (jnp.minimum(i, NTT_CTX - 1), 0))


def _lat_tile_spec(width):
    return pl.BlockSpec((TT, width), lambda i: (jnp.maximum(i - NTT_CTX, 0), 0))


def _tok_tile_spec(width):
    return pl.BlockSpec((TT, width), lambda i: (i, 0))


def _load_pair(i, ctx_ref, lat_ref):
    return jnp.where(i < NTT_CTX, ctx_ref[...], lat_ref[...])


def _stream_specs(x, width):
    if isinstance(x, tuple):
        return [_ctx_tile_spec(width), _lat_tile_spec(width)], list(x)
    return [_tok_tile_spec(width)], [x]


def _load_stream(i, refs):
    return refs[0][...] if len(refs) == 1 else _load_pair(i, *refs)


def _store_stream(i, refs, val):
    if len(refs) == 1:
        refs[0][...] = val
        return
    ctx_ref, lat_ref = refs

    @pl.when(i < NTT_CTX)
    def _():
        ctx_ref[...] = val

    lat_ref[...] = val


def _dot(a, b):
    return jnp.dot(a, b, preferred_element_type=F32)


def _dot_nt(a, b):
    return lax.dot_general(a, b, (((1,), (1,)), ((), ())), preferred_element_type=F32)


def _sigmoid(x):
    return 1.0 / (1.0 + jnp.exp(-x))


def _rms(x, g):
    return x * lax.rsqrt(jnp.mean(x * x, axis=-1, keepdims=True) + EPS) * g


def _cond_row(i):
    return jnp.maximum(i - (NTT_CTX - NTT_LAT_SEQ), 0) // NTT_LAT_SEQ


def _mod_body(c_ref, w_ref, b_ref, o_ref):
    c = c_ref[...]
    s = c * _sigmoid(c)
    w = w_ref[...]
    s_hi = s.astype(BF16)
    s_lo = (s - s_hi.astype(F32)).astype(BF16)
    w_hi = w.astype(BF16)
    w_lo = (w - w_hi.astype(F32)).astype(BF16)
    o_ref[...] = _dot(s_hi, w_hi) + _dot(s_lo, w_hi) + _dot(s_hi, w_lo) + b_ref[...]


def _modulation(cond8, w_ada, b_ada):
    n_col = 6 * D_MODEL // 1024
    return pl.pallas_call(
        _mod_body,
        grid=(DEPTH, n_col),
        in_specs=[
            pl.BlockSpec((SUBLANES, D_MODEL), lambda l, j: (0, 0)),
            pl.BlockSpec((None, D_MODEL, 1024), lambda l, j: (l, 0, j)),
            pl.BlockSpec((None, 1, 1024), lambda l, j: (l, 0, j)),
        ],
        out_specs=pl.BlockSpec((None, SUBLANES, 1024), lambda l, j: (l, 0, j)),
        out_shape=jax.ShapeDtypeStruct((DEPTH, SUBLANES, 6 * D_MODEL), F32),
        compiler_params=_cparams(("parallel", "parallel")),
        name="modulation",
    )(cond8, w_ada, b_ada.reshape(DEPTH, 1, 6 * D_MODEL))


def _dot_nt3(a, b):
    a_hi = a.astype(BF16)
    a_lo = (a - a_hi.astype(F32)).astype(BF16)
    b_hi = b.astype(BF16)
    b_lo = (b - b_hi.astype(F32)).astype(BF16)
    return _dot_nt(a_hi, b_hi) + _dot_nt(a_lo, b_hi) + _dot_nt(a_hi, b_lo)


def _ssm_prep_body(lre_ref, lim_ref, ldt_ref, bre_ref, bim_ref, cre_ref, cim_ref, lre_r, lim_r, ldt_r,
                   ere_ref, eim_ref, bst_ref, ckt_ref, kst_ref, pwr_ref, pwi_ref):
    lam_re = lre_ref[...]
    lam_im = lim_ref[...]
    dt = jnp.exp(ldt_ref[...])
    xr = lam_re * dt
    xi = lam_im * dt

    mag = jnp.exp(xr)
    a_re = mag * jnp.cos(xi)
    a_im = mag * jnp.sin(xi)
    pw = [(jnp.ones_like(a_re), jnp.zeros_like(a_re)), (a_re, a_im)]
    for _ in range(SCAN_R - 1):
        pr, pi = pw[-1]
        pw.append((pr * a_re - pi * a_im, pr * a_im + pi * a_re))
    den = lam_re * lam_re + lam_im * lam_im
    f_re = ((a_re - 1.0) * lam_re + a_im * lam_im) / den
    f_im = (a_im * lam_re - (a_re - 1.0) * lam_im) / den
    bb_re = f_re * bre_ref[...] - f_im * bim_ref[...]
    bb_im = f_re * bim_ref[...] + f_im * bre_ref[...]
    c_re = cre_ref[...]
    c_im = cim_ref[...]
    ere = ere_ref[...]
    eim = eim_ref[...]

    r_g = lax.broadcasted_iota(jnp.int32, (LANES, 4 * SLAB_LANES), 0) // SSM_GROUP
    col = lax.broadcasted_iota(jnp.int32, (LANES, 4 * SLAB_LANES), 1)
    c_g = 4 * (col // (2 * SLAB_LANES)) + (col % SLAB_LANES) // SSM_STATE
    same_group = r_g == c_g

    def embed(v_re, v_im, w):
        rows = slice(w * LANES, (w + 1) * LANES)
        blk = _dot(v_re[rows, :].astype(BF16), ere) + _dot(v_im[rows, :].astype(BF16), eim)
        return jnp.where(same_group, blk, 0.0).astype(BF16)

    for i in range(SCAN_R):
        pr, pi = pw[SCAN_R - 1 - i]
        g_re = pr * bb_re - pi * bb_im
        g_im = pr * bb_im + pi * bb_re
        qr, qi = pw[i + 1]
        w_re = c_re * qr - c_im * qi
        w_im = -(c_re * qi + c_im * qr)
        for w in range(N_WIN):
            bst_ref[w, i * LANES:(i + 1) * LANES, :] = embed(g_re, g_im, w)
            ckt_ref[w, i * LANES:(i + 1) * LANES, :] = embed(w_re, w_im, w)

    kr = lax.broadcasted_iota(jnp.int32, (LANES, LANES), 0) // SSM_GROUP
    kc = lax.broadcasted_iota(jnp.int32, (LANES, LANES), 1) // SSM_GROUP
    k_same = kr == kc
    zero_blk = jnp.zeros((LANES, LANES), BF16)
    for tau in range(SCAN_R):
        pr, pi = pw[tau]
        w_re = c_re * pr - c_im * pi
        w_im = c_re * pi + c_im * pr
        for w in range(N_WIN):
            rows = slice(w * LANES, (w + 1) * LANES)
            kt = _dot_nt3(bb_re[rows, :], w_re[rows, :]) - _dot_nt3(bb_im[rows, :], w_im[rows, :])
            ktm = jnp.where(k_same, kt, 0.0).astype(BF16)
            for i_in in range(SCAN_R - tau):
                i_out = i_in + tau
                kst_ref[w, i_in * LANES:(i_in + 1) * LANES, i_out * LANES:(i_out + 1) * LANES] = ktm
            if tau > 0:
                for i_out in range(tau):
                    i_in = i_out + SCAN_R - tau
                    kst_ref[w, i_in * LANES:(i_in + 1) * LANES, i_out * LANES:(i_out + 1) * LANES] = zero_blk

    shape2 = (SCAN_M, SSM_GROUPS * SSM_STATE)
    n = ((lax.broadcasted_iota(jnp.int32, shape2, 0) + 1) * SCAN_R).astype(F32)
    dt_r = jnp.exp(ldt_r[...])
    e = jnp.exp(n * (lre_r[...] * dt_r))
    th = n * (lim_r[...] * dt_r)
    p_re = e * jnp.cos(th)
    p_im = e * jnp.sin(th)
    for s in range(N_SLABS):
        lanes = slice(s * SLAB_LANES, (s + 1) * SLAB_LANES)
        for m in range(SCAN_M):
            pwr_ref[s, m] = jnp.broadcast_to(p_re[m:m + 1, lanes], (SUBLANES, SLAB_LANES))
            pwi_ref[s, m] = jnp.broadcast_to(p_im[m:m + 1, lanes], (SUBLANES, SLAB_LANES))


def _embed_consts():
    e = np.zeros((2, SSM_STATE, 4 * SLAB_LANES), np.float32)
    for ri in range(2):
        for sg in range(2):
            for g4 in range(4):
                base = sg * 2 * SLAB_LANES + ri * SLAB_LANES + g4 * SSM_STATE
                e[ri, np.arange(SSM_STATE), base + np.arange(SSM_STATE)] = 1.0
    return jnp.asarray(e[0], BF16), jnp.asarray(e[1], BF16)


def _ssm_prep(lam_re, lam_im, log_dt, b_re, b_im, c_re, c_im):
    ld = DEPTH * 2
    gp = SSM_GROUPS * SSM_STATE
    rep = lambda a: jnp.broadcast_to(a.reshape(ld, SSM_GROUPS, 1, -1),
                                     (ld, SSM_GROUPS, SSM_GROUP, SSM_STATE)).reshape(ld, GC, SSM_STATE)
    ldt = log_dt.reshape(ld, SSM_GROUPS, 1)
    row = lambda a: jnp.broadcast_to(a.reshape(ld, SSM_GROUPS, -1), (ld, SSM_GROUPS, SSM_STATE)).reshape(ld, 1, gp)
    bt = lambda b: b.reshape(ld, SSM_GROUPS, SSM_STATE, SSM_GROUP).transpose(0, 1, 3, 2).reshape(ld, GC, SSM_STATE)
    ct = lambda c: c.reshape(ld, GC, SSM_STATE)
    ere, eim = _embed_consts()
    gspec = pl.BlockSpec((None, GC, SSM_STATE), lambda i: (i, 0, 0))
    rspec = pl.BlockSpec((None, 1, gp), lambda i: (i, 0, 0))
    espec = _const_spec((SSM_STATE, 4 * SLAB_LANES))
    tspec = lambda r, c: pl.BlockSpec((None, N_WIN, r, c), lambda i: (i, 0, 0, 0))
    pspec = pl.BlockSpec((None, N_SLABS, SCAN_M, SUBLANES, SLAB_LANES), lambda i: (i, 0, 0, 0, 0))
    outs = pl.pallas_call(
        _ssm_prep_body,
        grid=(ld,),
        in_specs=[gspec] * 7 + [rspec] * 3 + [espec, espec],
        out_specs=[tspec(SCAN_R * LANES, 4 * SLAB_LANES), tspec(SCAN_R * LANES, 4 * SLAB_LANES),
                   tspec(SCAN_R * LANES, SCAN_R * LANES), pspec, pspec],
        out_shape=[jax.ShapeDtypeStruct((ld, N_WIN, SCAN_R * LANES, 4 * SLAB_LANES), BF16),
                   jax.ShapeDtypeStruct((ld, N_WIN, SCAN_R * LANES, 4 * SLAB_LANES), BF16),
                   jax.ShapeDtypeStruct((ld, N_WIN, SCAN_R * LANES, SCAN_R * LANES), BF16),
                   jax.ShapeDtypeStruct((ld, N_SLABS, SCAN_M, SUBLANES, SLAB_LANES), F32),
                   jax.ShapeDtypeStruct((ld, N_SLABS, SCAN_M, SUBLANES, SLAB_LANES), F32)],
        compiler_params=_cparams(("arbitrary",)),
        name="ssm_prep",
    )(rep(lam_re), rep(lam_im), rep(ldt), bt(b_re), bt(b_im), ct(c_re), ct(c_im),
      row(lam_re), row(lam_im), row(ldt), ere, eim)
    return [o.reshape((DEPTH, 2) + o.shape[1:]) for o in outs]


def _scan_perms():
    rho = np.arange(T)
    i, m, q = rho // BLK_ROWS, (rho % BLK_ROWS) // SUBLANES, rho % SUBLANES
    t_of_rho = q * (T // SUBLANES) + SCAN_R * m + i
    pm = np.zeros((2, T, T), np.float32)
    pm[0, rho, t_of_rho] = 1.0
    pm[1, rho, T - 1 - t_of_rho] = 1.0
    pmt = np.transpose(pm, (0, 2, 1))
    pmt2 = np.concatenate([pmt, pmt], axis=2)
    return jnp.asarray(pm, BF16), jnp.asarray(pmt2, BF16)


def _ssm_body(u_ref, pm_ref, pmt2_ref, bst_ref, ckt_ref, kst_ref, pwr_ref, pwi_ref, h0_ref,
              y_ref, fin_ref, up_s, uc_s, x_s, hp_s, st_s):
    d = pl.program_id(0)
    j = pl.program_id(1)
    is_ctx = j < NS_CTX
    first = jnp.logical_or(is_ctx, (j - NS_CTX) % NS_LAT_SEQ == 0)

    @pl.when(first)
    def _():
        st_s[...] = jnp.where(is_ctx, 0.0, jnp.broadcast_to(h0_ref[...], st_s.shape))

    def tile_rows(g):
        gg = jnp.where(d == 1, SCAN_G - 1 - g, g)
        return pl.ds(pl.multiple_of(gg * T, T), T), gg

    pm = pm_ref[...]
    for g in range(SCAN_G):
        up_s[g] = _dot(pm, u_ref[tile_rows(g)[0], :].astype(BF16)).astype(BF16)

    for w in range(N_WIN):
        lanes = slice(w * LANES, (w + 1) * LANES)
        lhs = jnp.concatenate(
            [jnp.concatenate([up_s[g, i * BLK_ROWS:(i + 1) * BLK_ROWS, lanes] for g in range(SCAN_G)], axis=0)
             for i in range(SCAN_R)], axis=1)
        uc_s[w] = lhs
        x_s[w] = _dot(lhs, bst_ref[w])

    row = lax.broadcasted_iota(jnp.int32, (SUBLANES, SLAB_LANES), 0)
    for s in range(N_SLABS):
        w, sg = s // 2, s % 2
        l_re = slice(sg * 2 * SLAB_LANES, sg * 2 * SLAB_LANES + SLAB_LANES)
        l_im = slice(sg * 2 * SLAB_LANES + SLAB_LANES, (sg + 1) * 2 * SLAB_LANES)
        ar = pwr_ref[s, 0]
        ai = pwi_ref[s, 0]
        m_r = pwr_ref[s, SCAN_M - 1]
        m_i = pwi_ref[s, SCAN_M - 1]
        for g in range(SCAN_G):
            base = g * BLK_ROWS
            hr = jnp.zeros((SUBLANES, SLAB_LANES), F32)
            hi = jnp.zeros((SUBLANES, SLAB_LANES), F32)
            for m in range(SCAN_M):
                rows = slice(base + m * SUBLANES, base + (m + 1) * SUBLANES)
                hr, hi = (ar * hr - ai * hi + x_s[w, rows, l_re], ar * hi + ai * hr + x_s[w, rows, l_im])
                if m < SCAN_M - 1:
                    nxt = slice(base + (m + 1) * SUBLANES, base + (m + 2) * SUBLANES)
                    hp_s[w, nxt, l_re] = hr
                    hp_s[w, nxt, l_im] = hi
            pr_ = st_s[s, 0]
            pi_ = st_s[s, 1]
            if g > 0:
                pr_ = jnp.where(is_ctx, 0.0, pr_)
                pi_ = jnp.where(is_ctx, 0.0, pi_)
            cr = jnp.where(row == 0, pltpu.roll(pr_, 1, 0), pltpu.roll(hr, 1, 0))
            ci = jnp.where(row == 0, pltpu.roll(pi_, 1, 0), pltpu.roll(hi, 1, 0))
            qr, qi = m_r, m_i
            for k in (1, 2, 4):
                sr = pltpu.roll(cr, k, 0)
                si = pltpu.roll(ci, k, 0)
                keep = row >= k
                cr, ci = (cr + jnp.where(keep, qr * sr - qi * si, 0.0),
                          ci + jnp.where(keep, qr * si + qi * sr, 0.0))
                qr, qi = qr * qr - qi * qi, 2.0 * qr * qi
            fr = m_r * cr - m_i * ci + hr
            fi = m_r * ci + m_i * cr + hi
            st_s[s, 0] = fr
            st_s[s, 1] = fi
            slot = tile_rows(g)[1]
            fin_ref[slot, s, 0] = fr[SUBLANES - 1:SUBLANES, :]
            fin_ref[slot, s, 1] = fi[SUBLANES - 1:SUBLANES, :]
            hp_s[w, base:base + SUBLANES, l_re] = cr
            hp_s[w, base:base + SUBLANES, l_im] = ci
            for m in range(SCAN_M - 1):
                rows = slice(base + (m + 1) * SUBLANES, base + (m + 2) * SUBLANES)
                pr = pwr_ref[s, m]
                pi = pwi_ref[s, m]
                hp_s[w, rows, l_re] = hp_s[w, rows, l_re] + pr * cr - pi * ci
                hp_s[w, rows, l_im] = hp_s[w, rows, l_im] + pr * ci + pi * cr

    yw = [_dot_nt(hp_s[w].astype(BF16), ckt_ref[w]) + _dot(uc_s[w], kst_ref[w]) for w in range(N_WIN)]
    pmt2 = pmt2_ref[...]
    for g in range(SCAN_G):
        rows = slice(g * BLK_ROWS, (g + 1) * BLK_ROWS)
        y = jnp.concatenate(
            [jnp.concatenate([yw[w][rows, i * LANES:(i + 1) * LANES] for w in range(N_WIN)], axis=1)
             for i in range(SCAN_R)], axis=0)
        y_hi = y.astype(BF16)
        y_lo = (y - y_hi.astype(F32)).astype(BF16)
        y_ref[tile_rows(g)[0], :] = _dot(pmt2, jnp.concatenate([y_hi, y_lo], axis=0))


def _ssm_super(d, j):
    jl = j - NS_CTX
    half = jl % NS_LAT_SEQ
    half = jnp.where(d == 1, NS_LAT_SEQ - 1 - half, half)
    return jnp.where(j < NS_CTX, j, NS_CTX + (jl // NS_LAT_SEQ) * NS_LAT_SEQ + half)


def _ssm_scan(l, u_ssm, pm, pmt2, bst, ckt, kst, pwr, pwi, h0):
    tab = lambda *shape: pl.BlockSpec((None,) + shape, lambda d, j: (d,) + (0,) * len(shape))
    ltab = lambda *shape: pl.BlockSpec((None, None) + shape, lambda d, j: (l, d) + (0,) * len(shape))
    return pl.pallas_call(
        _ssm_body,
        grid=(2, NS),
        in_specs=[
            pl.BlockSpec((SUPER, BRANCH_W), lambda d, j: (_ssm_super(d, j), 0)),
            tab(T, T), tab(T, 2 * T),
            ltab(N_WIN, SCAN_R * LANES, 4 * SLAB_LANES), ltab(N_WIN, SCAN_R * LANES, 4 * SLAB_LANES),
            ltab(N_WIN, SCAN_R * LANES, SCAN_R * LANES),
            ltab(N_SLABS, SCAN_M, SUBLANES, SLAB_LANES), ltab(N_SLABS, SCAN_M, SUBLANES, SLAB_LANES),
            pl.BlockSpec((None, None, None, N_SLABS, 2, 1, SLAB_LANES),
                         lambda d, j: (l, d, jnp.maximum(j - NS_CTX, 0) // NS_LAT_SEQ, 0, 0, 0, 0)),
        ],
        out_specs=[
            pl.BlockSpec((None, SUPER, BRANCH_W), lambda d, j: (d, _ssm_super(d, j), 0)),
            pl.BlockSpec((None, None, SCAN_G, N_SLABS, 2, 1, SLAB_LANES), lambda d, j: (d, j, 0, 0, 0, 0, 0)),
        ],
        out_shape=[jax.ShapeDtypeStruct((2, N_TOK, BRANCH_W), F32),
                   jax.ShapeDtypeStruct((2, NS, SCAN_G, N_SLABS, 2, 1, SLAB_LANES), F32)],
        scratch_shapes=[
            pltpu.VMEM((SCAN_G, T, BRANCH_W), BF16),
            pltpu.VMEM((N_WIN, SCAN_G * BLK_ROWS, SCAN_R * LANES), BF16),
            pltpu.VMEM((N_WIN, SCAN_G * BLK_ROWS, 4 * SLAB_LANES), F32),
            pltpu.VMEM((N_WIN, SCAN_G * BLK_ROWS, 4 * SLAB_LANES), F32),
            pltpu.VMEM((N_SLABS, 2, SUBLANES, SLAB_LANES), F32),
        ],
        compiler_params=_cparams(("arbitrary", "arbitrary")),
        name="ssm_scan",
    )(u_ssm, pm, pmt2, bst, ckt, kst, pwr, pwi, h0)


def _pre_body(n_x, *refs):
    (mod_ref, g_ref, wmain_ref, wkx_ref, wconv_ref, qag_ref, kvg_ref, wuq_ref, wuqs_ref, wuk_ref, wva_ref,
     wvb_ref, qng_ref, qngs_ref, kng_ref, kngs_ref, rc_ref, rsw_ref,
     ussm_ref, ufft_ref, q_ref, k_ref, va_ref, vb_ref, ckv_ref, kpe_ref, vc_ref, gb_ref) = refs[n_x:]
    i = pl.program_id(0)
    x = _load_stream(i, refs[:n_x])
    sh = mod_ref[:, 0:D_MODEL]
    sc = mod_ref[:, D_MODEL:2 * D_MODEL]
    xn = (_rms(x, g_ref[...]) * (1.0 + sc) + sh).astype(BF16)

    def proj(lo, hi):
        for ref, base, end in ((wmain_ref, Z_SSM, Z_KPE), (wkx_ref, Z_KPE, Z_HIN), (wconv_ref, Z_HIN, Z_COLS)):
            if base <= lo and hi <= end:
                return _dot(xn, ref[:, lo - base:hi - base])
        raise ValueError((lo, hi))

    zk = proj(Z_KPE, Z_KROT)
    krot = proj(Z_KROT, Z_KSW)
    kswp = proj(Z_KSW, Z_HIN)

    rc = rc_ref[...]
    rsw = rsw_ref[...]
    q_gc = qng_ref[...] * rc * ATTN_SCALE
    q_gs = qngs_ref[...] * rsw * ATTN_SCALE
    k_gc = kng_ref[...] * rc
    k_sw = kswp * (kngs_ref[...] * rsw)

    def rstd(v):
        return lax.rsqrt(jnp.sum(v * v, axis=-1, keepdims=True) * (1.0 / QK_DIM) + EPS)

    cqn = _rms(proj(Z_CQ, Z_CKV), qag_ref[...]).astype(BF16)
    qf = _dot(cqn, wuq_ref[...])
    qs = _dot(cqn, wuqs_ref[...])
    ckvn = _rms(proj(Z_CKV, Z_KPE), kvg_ref[...])
    ckvb = ckvn.astype(BF16)
    kf = _dot(ckvb, wuk_ref[...])
    va_ref[...] = (_dot(ckvb, wva_ref[...]) + _ones_column(V_DIM)).astype(BF16)
    vb_ref[...] = (_dot(ckvb, wvb_ref[...]) + _ones_column(0)).astype(BF16)
    for h in range(MLA_HEADS):
        lanes = slice(h * HEAD_PAD, (h + 1) * HEAD_PAD)
        qh = qf[:, lanes]
        q_ref[:, lanes] = ((qh * q_gc + qs[:, lanes] * q_gs) * rstd(qh)).astype(BF16)
        kh = kf[:, lanes] + krot
        k_ref[:, lanes] = ((kh * k_gc + k_sw) * rstd(kh)).astype(BF16)

    ussm_ref[...] = proj(Z_SSM, Z_FFT)
    ufft_ref[...] = proj(Z_FFT, Z_CQ).astype(BF16)
    h_in = proj(Z_HIN, Z_GB)
    gb_ref[...] = proj(Z_GB, Z_GC)
    vc_ref[...] = proj(Z_GC, Z_COLS) * h_in

    @pl.when(i < NTT_CTX)
    def _():
        ckv_ref[...] = ckvn
        kpe_ref[...] = zk[:, 0:ROPE_DIM]


def _pre(l, x, mod, p, rc, rsw):
    tile = _tok_tile_spec
    rope_blk = lambda i: (jnp.where(i < NTT_CTX, NTT_LAT_SEQ, i % NTT_LAT_SEQ), 0)
    rspec = pl.BlockSpec((TT, LANES), rope_blk)
    out = lambda width, dt: jax.ShapeDtypeStruct((N_TOK, width), dt)
    names = ('norm_mix_g', 'w_main', 'w_kx', 'w_conv', 'q_a_norm_g', 'kv_a_norm_g', 'wuq', 'wuqs', 'wuk', 'wva', 'wvb',
             'qng', 'qngs', 'kng', 'kngs')
    x_specs, x_args = _stream_specs(x, D_MODEL)
    return pl.pallas_call(
        functools.partial(_pre_body, len(x_args)),
        grid=(NTT,),
        in_specs=x_specs + [_mod_spec(l)]
        + [_layer_spec(l, p[n].shape[1:]) for n in names] + [rspec, rspec],
        out_specs=[tile(BRANCH_W), tile(BRANCH_W), tile(MLA_HEADS * HEAD_PAD), tile(MLA_HEADS * HEAD_PAD),
                   tile(BRANCH_W), tile(BRANCH_W),
                   _ctx_tile_spec(KV_RANK), _ctx_tile_spec(ROPE_DIM), tile(BRANCH_W), tile(BRANCH_W)],
        out_shape=[out(BRANCH_W, F32), out(BRANCH_W, BF16), out(MLA_HEADS * HEAD_PAD, BF16),
                   out(MLA_HEADS * HEAD_PAD, BF16), out(BRANCH_W, BF16), out(BRANCH_W, BF16),
                   jax.ShapeDtypeStruct((N_CTX, KV_RANK), F32), jax.ShapeDtypeStruct((N_CTX, ROPE_DIM), F32),
                   out(BRANCH_W, F32), out(BRANCH_W, F32)],
        compiler_params=_cparams(("arbitrary",)),
        name="pre_mixer",
    )(*x_args, mod, *[p[n] for n in names], rc, rsw)


def _kvc_body(ckv_ref, kr_ref, wuk_ref, wva_ref, wvb_ref, kng_ref, k_ref, va_ref, vb_ref):
    ckvb = ckv_ref[...].astype(BF16)
    kf = _dot(ckvb, wuk_ref[...])
    va_ref[...] = (_dot(ckvb, wva_ref[...]) + _ones_column(V_DIM)).astype(BF16)
    vb_ref[...] = (_dot(ckvb, wvb_ref[...]) + _ones_column(0)).astype(BF16)
    krot = kr_ref[...]
    g = kng_ref[...]
    for h in range(MLA_HEADS):
        lanes = slice(h * HEAD_PAD, (h + 1) * HEAD_PAD)
        kh = kf[:, lanes] + krot
        ss = jnp.sum(kh * kh, axis=-1, keepdims=True)
        k_ref[:, lanes] = (kh * lax.rsqrt(ss * (1.0 / QK_DIM) + EPS) * g).astype(BF16)


def _cache_kv(cache_ckv, cache_kr, wuk, wva, wvb, kng):
    nt = PAST_LEN // T
    wspec = lambda r, c: pl.BlockSpec((None, r, c), lambda l, b, t: (l, 0, 0))
    ospec = lambda c: pl.BlockSpec((None, None, T, c), lambda l, b, t: (l, b, t, 0))
    return pl.pallas_call(
        _kvc_body,
        grid=(DEPTH, DEC_BATCH, nt),
        in_specs=[
            pl.BlockSpec((None, None, T, KV_RANK), lambda l, b, t: (b, l, t, 0)),
            pl.BlockSpec((None, None, T, HEAD_PAD), lambda l, b, t: (b, l, t, 0)),
            wspec(KV_RANK, MLA_HEADS * HEAD_PAD), wspec(KV_RANK, BRANCH_W), wspec(KV_RANK, BRANCH_W),
            wspec(1, HEAD_PAD),
        ],
        out_specs=[ospec(MLA_HEADS * HEAD_PAD), ospec(BRANCH_W), ospec(BRANCH_W)],
        out_shape=[jax.ShapeDtypeStruct((DEPTH, DEC_BATCH, PAST_LEN, MLA_HEADS * HEAD_PAD), BF16),
                   jax.ShapeDtypeStruct((DEPTH, DEC_BATCH, PAST_LEN, BRANCH_W), BF16),
                   jax.ShapeDtypeStruct((DEPTH, DEC_BATCH, PAST_LEN, BRANCH_W), BF16)],
        compiler_params=_cparams(("parallel", "parallel", "parallel")),
        name="cache_kv",
    )(cache_ckv, cache_kr, wuk, wva, wvb, kng)


def _fnet_body(seq_len, ns, u_ref, csl_ref, csc_ref, y_ref, v1_s, v2_s):
    r = pl.program_id(1)

    @pl.when(r == 0)
    def _():
        csc = csc_ref[...]
        for c in range(ns * seq_len // T):
            rows = slice(c * T, (c + 1) * T)
            for g in range(FFT_GROUPS):
                lanes = slice(g * FFT_GW, (g + 1) * FFT_GW)
                v = _dot(u_ref[rows, lanes], csc)
                v1_s[rows, lanes] = v[:, :FFT_GW].astype(BF16)
                v2_s[rows, lanes] = v[:, FFT_GW:].astype(BF16)

    for s in range(ns):
        rows = slice(s * seq_len, (s + 1) * seq_len)
        y = _dot(csl_ref[:, :seq_len], v1_s[rows, :]) + _dot(csl_ref[:, seq_len:], v2_s[rows, :])
        y_ref[s * T:(s + 1) * T, :] = y.astype(BF16)


def _fnet(u_fft, csl, csc, seq_len, n_seq, tile0, ns=1):
    nr = seq_len // T
    assert ns == 1 or nr == 1
    blk0 = tile0 * T // (ns * seq_len)
    return pl.pallas_call(
        functools.partial(_fnet_body, seq_len, ns),
        grid=(n_seq // ns, nr),
        in_specs=[
            pl.BlockSpec((ns * seq_len, BRANCH_W), lambda b, r: (blk0 + b, 0)),
            pl.BlockSpec((T, 2 * seq_len), lambda b, r: (r, 0)),
            _const_spec((FFT_GW, 2 * FFT_GW)),
        ],
        out_specs=pl.BlockSpec((ns * T, BRANCH_W), lambda b, r: (b * nr + r, 0)),
        out_shape=jax.ShapeDtypeStruct((n_seq * seq_len, BRANCH_W), BF16),
        scratch_shapes=[pltpu.VMEM((ns * seq_len, BRANCH_W), BF16), pltpu.VMEM((ns * seq_len, BRANCH_W), BF16)],
        compiler_params=_cparams(("arbitrary", "arbitrary")),
        name="fnet_%d" % seq_len,
    )(u_fft, csl, csc)


def _dft_tables(n, split):
    m = jnp.arange(n, dtype=jnp.int32)
    a = jnp.arange(n // split, dtype=jnp.int32) * split
    b = jnp.arange(split, dtype=jnp.int32)
    w = 2.0 * math.pi / n
    ang_a = ((a[:, None] * m[None, :]) % n).astype(F32) * w
    ang_b = ((b[:, None] * m[None, :]) % n).astype(F32) * w
    ca, sa = jnp.cos(ang_a)[:, None, :], jnp.sin(ang_a)[:, None, :]
    cb, sb = jnp.cos(ang_b)[None, :, :], jnp.sin(ang_b)[None, :, :]
    s = 1.0 / math.sqrt(n)
    cos = ((ca * cb - sa * sb) * s).reshape(n, n)
    sin = ((sa * cb + ca * sb) * s).reshape(n, n)
    return cos, sin


def _attn_body(n_seg, ns, q_ref, *refs):
    k_refs = refs[0:n_seg]
    va_refs = refs[n_seg:2 * n_seg]
    vb_refs = refs[2 * n_seg:3 * n_seg]
    o_ref = refs[-1]
    tq = q_ref.shape[0] // ns
    lane = lax.broadcasted_iota(jnp.int32, (tq, LANES), 1)
    for sq in range(ns):
        q_rows = slice(sq * tq, (sq + 1) * tq)
        k_rows = [slice(sq * (kr.shape[0] // ns), (sq + 1) * (kr.shape[0] // ns)) for kr in k_refs]
        for hp in range(MLA_HEADS // 2):
            out_lanes = slice(hp * LANES, (hp + 1) * LANES)
            halves = []
            for h, v_refs, sum_lane in ((2 * hp, va_refs, V_DIM), (2 * hp + 1, vb_refs, 0)):
                lanes = slice(h * HEAD_PAD, (h + 1) * HEAD_PAD)
                qh = q_ref[q_rows, lanes]
                ss = [_dot_nt(qh, kr[rows, lanes]) for kr, rows in zip(k_refs, k_rows)]
                m = functools.reduce(jnp.maximum, [jnp.max(s, axis=-1, keepdims=True) for s in ss])
                o = functools.reduce(lambda a, b: a + b,
                                     [_dot(jnp.exp2(s - m).astype(BF16), vr[rows, out_lanes])
                                      for s, vr, rows in zip(ss, v_refs, k_rows)])
                l = jnp.sum(jnp.where(lane == sum_lane, o, 0.0), axis=-1, keepdims=True)
                halves.append(o / l)
            o_ref[q_rows, out_lanes] = jnp.where(lane < V_DIM, halves[0], halves[1]).astype(BF16)


def _attention(q, segs, n_seq, nq, row0, tq, ns=1):
    tile0 = row0 // tq
    in_specs = [pl.BlockSpec((tq, MLA_HEADS * HEAD_PAD), lambda b, r: (tile0 + b * nq + r, 0))]
    args = [q]
    for idx in range(3):
        for seg in segs:
            arr = seg[idx]
            width = arr.shape[-1]
            in_specs.append(pl.BlockSpec(seg[3] + (width,), seg[4]))
            args.append(arr)
    return pl.pallas_call(
        functools.partial(_attn_body, len(segs), ns),
        grid=(n_seq, nq),
        in_specs=in_specs,
        out_specs=pl.BlockSpec((tq, BRANCH_W), lambda b, r: (b * nq + r, 0)),
        out_shape=jax.ShapeDtypeStruct((n_seq * nq * tq, BRANCH_W), BF16),
        compiler_params=_cparams(("arbitrary", "arbitrary")),
        name="attention_%d" % nq,
    )(*args)


def _mix_body(n_x, *refs):
    (mod_ref, g_ref, yf_ref, yb_ref, u_ref, dsk_ref, wglu_ref, fc_ref, fl_ref,
     ac_ref, al_ref, vc_ref, vp_ref, vn_ref, cw_ref, gb_ref, wbr_ref, wg_ref, bg_ref, wo_ref, o_ref) = refs[n_x:]
    i = pl.program_id(0)
    x = _load_stream(i, refs[:n_x])
    sh = mod_ref[:, 0:D_MODEL]
    sc = mod_ref[:, D_MODEL:2 * D_MODEL]
    gm = mod_ref[:, 2 * D_MODEL:3 * D_MODEL]
    xn = (_rms(x, g_ref[...]) * (1.0 + sc) + sh).astype(BF16)

    y = jax.nn.gelu(yf_ref[...] + yb_ref[...] + u_ref[...] * dsk_ref[...], approximate=True)
    y = y * _sigmoid(_dot(y.astype(BF16), wglu_ref[...]))

    v = vc_ref[...]
    row = lax.broadcasted_iota(jnp.int32, (TT, BRANCH_W), 0)
    seq_len = jnp.where(i < NTT_CTX, SEQ, DEC_SEQ)
    pos = jnp.bitwise_and(i * TT + row, seq_len - 1)
    v_prev = jnp.where(row == 0, vp_ref[SUBLANES - 1:SUBLANES, :], pltpu.roll(v, 1, 0))
    v_next = jnp.where(row == TT - 1, vn_ref[0:1, :], pltpu.roll(v, TT - 1, 0))
    v_prev = jnp.where(pos == 0, 0.0, v_prev)
    v_next = jnp.where(pos == seq_len - 1, 0.0, v_next)
    y_conv = gb_ref[...] * (cw_ref[0:1, :] * v_prev + cw_ref[1:2, :] * v + cw_ref[2:3, :] * v_next)

    branches = (y.astype(BF16), _load_pair(i, fc_ref, fl_ref), _load_pair(i, ac_ref, al_ref),
                y_conv.astype(BF16))
    gates = _sigmoid(_dot(xn, wg_ref[...]) + bg_ref[...])
    merged = jnp.zeros((TT, D_MODEL), F32)
    for k in range(N_BRANCH):
        merged = merged + gates[:, k * D_MODEL:(k + 1) * D_MODEL] * _dot(branches[k], wbr_ref[k])
    out = _dot(merged.astype(BF16), wo_ref[...])
    o_ref[...] = x + gm * out


def _mix(l, x, mod, p, y_ssm, u_ssm, fft_c, fft_l, att_c, att_l, vconv, g_b):
    tile = _tok_tile_spec
    rb = TT // SUBLANES
    n_rb = N_TOK // SUBLANES
    lspec = lambda n: _layer_spec(l, p[n].shape[1:])
    x_specs, x_args = _stream_specs(x, D_MODEL)
    return pl.pallas_call(
        functools.partial(_mix_body, len(x_args)),
        grid=(NTT,),
        in_specs=x_specs + [
            _mod_spec(l), lspec('norm_mix_g'),
            pl.BlockSpec((None, TT, BRANCH_W), lambda i: (0, i, 0)),
            pl.BlockSpec((None, TT, BRANCH_W), lambda i: (1, i, 0)),
            tile(BRANCH_W), lspec('ssm_d'), lspec('w_glu'),
            _ctx_tile_spec(BRANCH_W), _lat_tile_spec(BRANCH_W), _ctx_tile_spec(BRANCH_W), _lat_tile_spec(BRANCH_W),
            tile(BRANCH_W),
            pl.BlockSpec((SUBLANES, BRANCH_W), lambda i: (jnp.maximum(i * rb - 1, 0), 0)),
            pl.BlockSpec((SUBLANES, BRANCH_W), lambda i: (jnp.minimum(i * rb + rb, n_rb - 1), 0)),
            lspec('conv_w'), tile(BRANCH_W),
            lspec('w_branch'), lspec('w_gate'), lspec('b_gate'), lspec('w_out'),
        ],
        out_specs=tile(D_MODEL),
        out_shape=jax.ShapeDtypeStruct((N_TOK, D_MODEL), F32),
        compiler_params=_cparams(("arbitrary",)),
        name="mixer_merge",
    )(*x_args, mod, p['norm_mix_g'], y_ssm, y_ssm, u_ssm, p['ssm_d'], p['w_glu'], fft_c, fft_l, att_c, att_l,
      vconv, vconv, vconv, p['conv_w'], g_b, p['w_branch'], p['w_gate'], p['b_gate'], p['w_out'])


FF_CHUNK = D_FF


def _ffn_body(n_out, x_ref, mod_ref, g_ref, wi_ref, wo_ref, *o_refs):
    i = pl.program_id(0)
    x = x_ref[...]
    sh = mod_ref[:, 3 * D_MODEL:4 * D_MODEL]
    sc = mod_ref[:, 4 * D_MODEL:5 * D_MODEL]
    gf = mod_ref[:, 5 * D_MODEL:6 * D_MODEL]
    xn = (_rms(x, g_ref[...]) * (1.0 + sc) + sh).astype(BF16)
    acc = jnp.zeros((TT, D_MODEL), F32)
    for c in range(D_FF // FF_CHUNK):
        lo = c * FF_CHUNK
        g = _dot(xn, wi_ref[:, lo:lo + FF_CHUNK])
        u = _dot(xn, wi_ref[:, D_FF + lo:D_FF + lo + FF_CHUNK])
        h = (g * _sigmoid(g) * u).astype(BF16)
        acc = acc + _dot(h, wo_ref[lo:lo + FF_CHUNK, :])
    _store_stream(i, o_refs[:n_out], x + gf * acc)


def _ffn(l, x, mod, p, split_out):
    lspec = lambda n: _layer_spec(l, p[n].shape[1:])
    tile = _tok_tile_spec(D_MODEL)
    if split_out:
        out_specs = [_ctx_tile_spec(D_MODEL), _lat_tile_spec(D_MODEL)]
        out_shape = [jax.ShapeDtypeStruct((N_CTX, D_MODEL), F32), jax.ShapeDtypeStruct((N_LAT, D_MODEL), F32)]
    else:
        out_specs = [tile]
        out_shape = [jax.ShapeDtypeStruct((N_TOK, D_MODEL), F32)]
    return pl.pallas_call(
        functools.partial(_ffn_body, len(out_specs)),
        grid=(NTT,),
        in_specs=[tile, _mod_spec(l), lspec('norm_ffn_g'), lspec('w_ffn_in'), lspec('w_ffn_out')],
        out_specs=out_specs,
        out_shape=out_shape,
        compiler_params=_cparams(("arbitrary",)),
        name="ffn",
    )(x, mod, p['norm_ffn_g'], p['w_ffn_in'], p['w_ffn_out'])


def _prep_weights(norm_mix_g, norm_ffn_g, w_in, q_a_norm_g, kv_a_norm_g, w_uq, w_ukv, q_norm_g, k_norm_g,
                  ssm_d, w_glu, conv_w, w_branch, w_gate, b_gate, w_out, w_ffn_in, w_ffn_out):
    vec = lambda a: a.reshape(DEPTH, 1, a.shape[-1])
    kpe = w_in[:, :, OFF_KPE:OFF_KPE + ROPE_DIM]
    z = lambda n: jnp.zeros((DEPTH, D_MODEL, n), F32)
    w_main = w_in[:, :, OFF_SSM:OFF_KPE].astype(BF16)
    w_kx = jnp.concatenate([
        kpe, z(LANES - ROPE_DIM),
        z(NOPE_DIM), kpe[:, :, ROPE_PERM], z(LANES - QK_DIM),
        z(NOPE_DIM), kpe[:, :, ROPE_PARTNER], z(LANES - QK_DIM),
    ], axis=2).astype(BF16)
    w_conv = w_in[:, :, OFF_CONV:OFF_CONV + 3 * BRANCH_W].astype(BF16)
    wq = w_uq.reshape(DEPTH, Q_RANK, MLA_HEADS, QK_DIM)
    head_pad = lambda a: jnp.pad(a, ((0, 0), (0, 0), (0, 0), (0, HEAD_PAD - a.shape[-1]))).reshape(
        DEPTH, a.shape[1], MLA_HEADS * HEAD_PAD).astype(BF16)
    wuq = head_pad(jnp.concatenate([wq[..., :NOPE_DIM], wq[..., NOPE_DIM:][..., ROPE_PERM]], axis=3))
    wuqs = head_pad(jnp.concatenate([jnp.zeros_like(wq[..., :NOPE_DIM]), wq[..., NOPE_DIM:][..., ROPE_PARTNER]],
                                    axis=3))
    wkv = w_ukv.reshape(DEPTH, KV_RANK, MLA_HEADS, NOPE_DIM + V_DIM)
    wuk = head_pad(wkv[..., :NOPE_DIM])
    wv = wkv[..., NOPE_DIM:]
    even = (jnp.arange(MLA_HEADS) % 2 == 0)[None, None, :, None]
    wva = jnp.where(even, wv, 0.0).reshape(DEPTH, KV_RANK, BRANCH_W).astype(BF16)
    wvb = jnp.where(even, 0.0, wv).reshape(DEPTH, KV_RANK, BRANCH_W).astype(BF16)

    def gain(g, perm):
        gp = jnp.concatenate([g[:, :NOPE_DIM], g[:, NOPE_DIM:][:, perm],
                              jnp.zeros((DEPTH, HEAD_PAD - QK_DIM), F32)], axis=1)
        return gp.reshape(DEPTH, 1, HEAD_PAD)

    return dict(
        norm_mix_g=vec(norm_mix_g), norm_ffn_g=vec(norm_ffn_g), w_main=w_main, w_kx=w_kx, w_conv=w_conv,
        q_a_norm_g=vec(q_a_norm_g), kv_a_norm_g=vec(kv_a_norm_g),
        wuq=wuq, wuqs=wuqs, wuk=wuk, wva=wva, wvb=wvb,
        qng=gain(q_norm_g, ROPE_PERM), qngs=gain(q_norm_g, ROPE_PARTNER),
        kng=gain(k_norm_g, ROPE_PERM), kngs=gain(k_norm_g, ROPE_PARTNER),
        ssm_d=vec(ssm_d), w_glu=w_glu.astype(BF16),
        conv_w=jnp.pad(conv_w, ((0, 0), (0, SUBLANES - 3), (0, 0))),
        w_branch=w_branch.astype(BF16), w_gate=w_gate.astype(BF16), b_gate=vec(b_gate),
        w_out=w_out.astype(BF16), w_ffn_in=w_ffn_in.astype(BF16), w_ffn_out=w_ffn_out.astype(BF16),
    )


def _rope_tables():
    rows = DEC_SEQ // GRID_W
    r = jnp.repeat(jnp.arange(rows, dtype=F32), GRID_W)
    c = jnp.tile(jnp.arange(GRID_W, dtype=F32), rows)
    n_freq = ROPE_DIM // 4
    inv = ROPE_THETA ** (-jnp.arange(n_freq, dtype=F32) / n_freq)
    ang = jnp.concatenate([r[:, None] * inv, c[:, None] * inv], axis=-1)
    cos, sin = jnp.cos(ang), jnp.sin(ang)
    ones = jnp.ones((DEC_SEQ, NOPE_DIM), F32)
    zeros = jnp.zeros((DEC_SEQ, NOPE_DIM), F32)
    tail1 = jnp.ones((DEC_SEQ, HEAD_PAD - QK_DIM), F32)
    tail0 = jnp.zeros((DEC_SEQ, HEAD_PAD - QK_DIM), F32)
    rc = jnp.concatenate([ones, cos, cos, tail1], axis=1)
    rsw = jnp.concatenate([zeros, -sin, sin, tail0], axis=1)
    ident = lambda v: jnp.full((TT, HEAD_PAD), v, F32)
    return jnp.concatenate([rc, ident(1.0)], axis=0), jnp.concatenate([rsw, ident(0.0)], axis=0)


def kernel(x_prompt, x_sample, cache_ckv, cache_kpe, state_ssm, c, c_ctx, norm_mix_g, norm_ffn_g, w_ada, b_ada,
           w_in, q_a_norm_g, kv_a_norm_g, w_uq, w_ukv, q_norm_g, k_norm_g, ssm_lam_re, ssm_lam_im, ssm_log_dt,
           ssm_b_re, ssm_b_im, ssm_c_re, ssm_c_im, ssm_d, w_glu, conv_w, w_branch, w_gate, b_gate, w_out,
           w_ffn_in, w_ffn_out):
    xp = x_prompt.reshape(N_CTX, D_MODEL)
    xs = x_sample.reshape(N_LAT, D_MODEL)

    cond8 = jnp.concatenate([c_ctx[None, :], c, jnp.zeros((SUBLANES - 1 - DEC_BATCH, D_MODEL), F32)], axis=0)
    mod = _modulation(cond8, w_ada, b_ada).reshape(DEPTH, SUBLANES, 1, 6 * D_MODEL)

    bst, ckt, kst, pwr, pwi = _ssm_prep(ssm_lam_re, ssm_lam_im, ssm_log_dt, ssm_b_re, ssm_b_im,
                                        ssm_c_re, ssm_c_im)
    pm, pmt2 = _scan_perms()
    st = state_ssm.transpose(1, 2, 0, 5, 3, 4).reshape(DEPTH, 2, DEC_BATCH, 2, N_SLABS, 1, SLAB_LANES)
    h0_all = st.transpose(0, 1, 2, 4, 3, 5, 6)

    rc, rsw = _rope_tables()
    cl, sl = _dft_tables(SEQ, 16)
    csl_ctx = jnp.concatenate([cl, -sl], axis=1).astype(BF16)
    cl, sl = _dft_tables(DEC_SEQ, 64)
    csl_lat = jnp.concatenate([cl, -sl], axis=1).astype(BF16)
    cc, sc = _dft_tables(FFT_GW, 8)
    csc = jnp.concatenate([cc, sc], axis=1).astype(BF16)

    p = _prep_weights(norm_mix_g, norm_ffn_g, w_in, q_a_norm_g, kv_a_norm_g, w_uq, w_ukv, q_norm_g, k_norm_g,
                      ssm_d, w_glu, conv_w, w_branch, w_gate, b_gate, w_out, w_ffn_in, w_ffn_out)
    cache_kr = jnp.pad(cache_kpe[..., ROPE_PERM], ((0, 0), (0, 0), (0, 0), (NOPE_DIM, HEAD_PAD - QK_DIM)))
    kc, vca, vcb = _cache_kv(cache_ckv, cache_kr, p['wuk'], p['wva'], p['wvb'], p['kng'])

    lat_blk = N_CTX // DEC_SEQ
    ckv_list, kpe_list, fin_list = [], [], []
    x = (xp, xs)
    for l in range(DEPTH):
        (u_ssm, u_fft, q, k, va, vb, ckv_n, kpe, vconv, g_b) = _pre(l, x, mod, p, rc, rsw)
        ckv_list.append(ckv_n.reshape(BATCH, SEQ, KV_RANK))
        kpe_list.append(kpe.reshape(BATCH, SEQ, ROPE_DIM))

        y_ssm, fin = _ssm_scan(l, u_ssm, pm, pmt2, bst, ckt, kst, pwr, pwi, h0_all)
        fin_list.append(fin[:, :NS_CTX].reshape(2, BATCH, N_SLABS, 2, SLAB_LANES))

        fft_c = _fnet(u_fft, csl_ctx, csc, SEQ, BATCH, 0, ns=4)
        fft_l = _fnet(u_fft, csl_lat, csc, DEC_SEQ, DEC_BATCH, NT_CTX)

        ctx_seg = (k, va, vb, (CTX_ATT_SEQS * SEQ,), lambda b, r: (b, 0))
        att_c = _attention(q, [ctx_seg], BATCH // CTX_ATT_SEQS, 1, 0, CTX_ATT_SEQS * SEQ, ns=CTX_ATT_SEQS)
        lat_seg = (k, va, vb, (DEC_SEQ,), lambda b, r: (lat_blk + b, 0))
        cache_seg = (kc, vca, vcb, (None, None, PAST_LEN), functools.partial(lambda l_, b, r: (l_, b, 0, 0), l))
        att_l = _attention(q, [lat_seg, cache_seg], DEC_BATCH, DEC_SEQ // TQ_LAT, N_CTX, TQ_LAT)

        x = _mix(l, x, mod, p, y_ssm, u_ssm, fft_c, fft_l, att_c, att_l, vconv, g_b)
        x = _ffn(l, x, mod, p, split_out=(l == DEPTH - 1))
        x = tuple(x) if l == DEPTH - 1 else x[0]

    y_p = x[0].reshape(BATCH, SEQ, D_MODEL)
    y_s = x[1].reshape(DEC_BATCH, DEC_SEQ, D_MODEL)
    new_ckv = jnp.stack(ckv_list, axis=1)
    new_kpe = jnp.stack(kpe_list, axis=1)
    fin = jnp.stack(fin_list, axis=0)
    fin = fin.reshape(DEPTH, 2, BATCH, N_SLABS, 2, 4, SSM_STATE).transpose(2, 0, 1, 3, 5, 6, 4)
    new_ssm = fin.reshape(BATCH, DEPTH, 2, SSM_GROUPS, SSM_STATE, 2)
    return (y_p, y_s, new_ckv, new_kpe, new_ssm)
```
